```python
import math, functools
import jax, jax.numpy as jnp
from jax import lax
import numpy as np

D_MODEL = 2048
BATCH = 16
SEQ = 2048
DEPTH = 2
DEC_BATCH = 8
DEC_SEQ = 64
PAST_LEN = 4096

CHUNK = 64
A_HEADS = 16
A_HEAD_DIM = 64
A_WIDTH = A_HEADS * A_HEAD_DIM
DECAY_RANK = 64
ICLR_RANK = 64
GATE_RANK = 128
LNX_EPS = 64e-5
B_HEADS = 8
B_QK_DIM = 128
B_V_DIM = 256
B_QK_WIDTH = B_HEADS * B_QK_DIM
B_V_WIDTH = B_HEADS * B_V_DIM
ROPE_BASE = 10000.0
GN_EPS = 1e-5
D_FF = ((8 * D_MODEL // 3 + 255) // 256) * 256
RMS_EPS = 1e-6
A_SHIFT_WIDTH = 3 * A_WIDTH + DECAY_RANK + ICLR_RANK + GATE_RANK
B_IN_WIDTH = 2 * B_QK_WIDTH + 2 * B_V_WIDTH
IN_WIDTH = A_SHIFT_WIDTH + B_IN_WIDTH + 2 * D_MODEL
A_SPLITS = (A_WIDTH, 2 * A_WIDTH, 3 * A_WIDTH, 3 * A_WIDTH + DECAY_RANK, 3 * A_WIDTH + DECAY_RANK + ICLR_RANK)
B_SPLITS = (B_QK_WIDTH, 2 * B_QK_WIDTH, 2 * B_QK_WIDTH + B_V_WIDTH)

kernel_name = "hybrid_rwkv7_retention_stream_step"


def rms_norm(x, g):
    x32 = x.astype(jnp.float32)
    y = x32 * lax.rsqrt(jnp.mean(x32 * x32, axis=-1, keepdims=True) + RMS_EPS)
    return (y * g.astype(jnp.float32)).astype(x.dtype)


def head_norm(y, eps):
    mean = jnp.mean(y, axis=-1, keepdims=True)
    var = jnp.mean(jnp.square(y - mean), axis=-1, keepdims=True)
    return (y - mean) * lax.rsqrt(var + eps)


def wkv7_step(S, inp):
    r_t, w_t, k_t, v_t, a_t, b_t = inp
    sa = jnp.einsum('bhij,bhj->bhi', S, a_t)
    S = S * w_t[:, :, None, :] + sa[..., :, None] * b_t[..., None, :] + v_t[..., :, None] * k_t[..., None, :]
    y = jnp.einsum('bhij,bhj->bhi', S, r_t)
    return S, y


def rwkv7_branch(za, shift_prev, S_prev, mu, w0, w_dup, a0, w_aup, w_gup, k_k, k_a, r_k, lnx_g, lnx_b):
    B, T, _ = za.shape
    f32 = jnp.float32
    prev = jnp.concatenate([shift_prev[:, None, :].astype(za.dtype), za[:, :-1]], axis=1)
    zs = (za + mu * (prev - za)).astype(f32)
    r, k, v, wd, ad, gd = jnp.split(zs, A_SPLITS, axis=-1)
    log_w = -jax.nn.softplus(-(w0 + jnp.tanh(wd) @ w_dup)) - 0.5
    decay = jnp.exp(-jnp.exp(log_w))
    a = jax.nn.sigmoid(a0 + ad @ w_aup)
    g = jax.nn.sigmoid(gd) @ w_gup
    kk = k * k_k
    k = k * (1.0 + (a - 1.0) * k_a)
    hs = lambda t: t.reshape(B, T, A_HEADS, A_HEAD_DIM)
    r, k, v, kk, a, decay = hs(r), hs(k), hs(v), hs(kk), hs(a), hs(decay)
    kk = kk / jnp.maximum(jnp.sqrt(jnp.sum(kk * kk, axis=-1, keepdims=True)), 1e-12)
    tm = lambda t: jnp.moveaxis(t, 1, 0)
    S_final, y = lax.scan(wkv7_step, S_prev.astype(f32),
                          (tm(r), tm(decay), tm(k), tm(v), tm(-kk), tm(kk * a)))
    y = jnp.moveaxis(y, 0, 1)
    yn = head_norm(y, LNX_EPS).reshape(B, T, A_WIDTH) * lnx_g + lnx_b
    bonus = (jnp.sum(r * k * r_k, axis=-1, keepdims=True) * v).reshape(B, T, A_WIDTH)
    out = (yn + bonus) * g
    return out.astype(za.dtype), S_final.astype(S_prev.dtype)


def rotary(x, pos):
    half = x.shape[-1] // 2
    inv = ROPE_BASE ** (-jnp.arange(half, dtype=jnp.float32) / half)
    ang = pos[:, None] * inv[None, :]
    cos = jnp.cos(ang)[None, :, None, :]
    sin = jnp.sin(ang)[None, :, None, :]
    x1, x2 = x[..., :half], x[..., half:]
    return jnp.concatenate([x1 * cos - x2 * sin, x1 * sin + x2 * cos], axis=-1)


def retention_chunk(R, qkv, log_gamma):
    q, k, v = qkv
    L = q.shape[1]
    idx = jnp.arange(L, dtype=jnp.float32)
    diff = idx[:, None] - idx[None, :]
    dmask = jnp.where(diff >= 0, jnp.exp(jnp.maximum(diff, 0.0)[None] * log_gamma[:, None, None]), 0.0)
    scores = jnp.einsum('blhd,bmhd->bhlm', q, k) * dmask[None]
    inner = jnp.einsum('bhlm,bmhe->blhe', scores, v)
    cross_decay = jnp.exp((idx + 1.0)[:, None] * log_gamma[None, :])
    cross = jnp.einsum('blhd,bhde->blhe', q, R) * cross_decay[None, :, :, None]
    to_end = jnp.exp((L - 1.0 - idx)[:, None] * log_gamma[None, :])
    R_new = jnp.exp(L * log_gamma)[None, :, None, None] * R + jnp.einsum('blhd,lh,blhe->bhde', k, to_end, v)
    return R_new, inner + cross


def retention_branch(zb, pos, R_prev, gn_g):
    B, T, _ = zb.shape
    f32 = jnp.float32
    q, k, v, gate = jnp.split(zb.astype(f32), B_SPLITS, axis=-1)
    q = rotary(q.reshape(B, T, B_HEADS, B_QK_DIM), pos) * (B_QK_DIM ** -0.5)
    k = rotary(k.reshape(B, T, B_HEADS, B_QK_DIM), pos)
    v = v.reshape(B, T, B_HEADS, B_V_DIM)
    L = min(T, CHUNK)
    nc = T // L
    to_chunks = lambda t: jnp.moveaxis(t.reshape(B, nc, L, B_HEADS, t.shape[-1]), 1, 0)
    log_gamma = jnp.log1p(-jnp.exp2(-5.0 - jnp.arange(B_HEADS, dtype=f32)))
    R_final, o = lax.scan(functools.partial(retention_chunk, log_gamma=log_gamma), R_prev.astype(f32),
                          (to_chunks(q), to_chunks(k), to_chunks(v)))
    o = jnp.moveaxis(o, 0, 1).reshape(B, T, B_HEADS, B_V_DIM)
    o = head_norm(o, GN_EPS).reshape(B, T, B_V_WIDTH) * gn_g
    y = jax.nn.silu(gate) * o
    return y.astype(zb.dtype), R_final.astype(R_prev.dtype)


def trunk_layer(x, pos, shift_prev, S_prev, R_prev, nmp, nmq, nfp, nfq, w_in, mu, w0, w_dup, a0, w_aup,
                w_gup, k_k, k_a, r_k, lnx_g, lnx_b, w_ba, ret_g, w_bb, w_out, w_fi, w_fo):
    h = rms_norm(x, nmp)
    z = h @ w_in
    z_a, z_b, z_gate = jnp.split(z, (A_SHIFT_WIDTH, A_SHIFT_WIDTH + B_IN_WIDTH), axis=-1)
    y_a, S_new = rwkv7_branch(z_a, shift_prev, S_prev, mu, w0, w_dup, a0, w_aup, w_gup, k_k, k_a, r_k, lnx_g, lnx_b)
    y_b, R_new = retention_branch(z_b, pos, R_prev, ret_g)
    gate_a, gate_b = jnp.split(z_gate, 2, axis=-1)
    merged = jax.nn.sigmoid(gate_a) * (y_a @ w_ba) + jax.nn.sigmoid(gate_b) * (y_b @ w_bb)
    x = x + rms_norm(merged @ w_out, nmq)
    h = rms_norm(x, nfp)
    g, u = jnp.split(h @ w_fi, 2, axis=-1)
    x = x + rms_norm((jax.nn.silu(g) * u) @ w_fo, nfq)
    return x, z_a[:, -1], S_new, R_new


def setup_inputs(seed: int = 0) -> dict:
    key = jax.random.key(seed)
    ks = jax.random.split(key, 32)
    f32 = jnp.float32

    def nrm(k, shape, scale):
        return jax.random.normal(k, shape, f32) * scale

    def gain(k, shape, base=1.0):
        return base + 0.05 * jax.random.normal(k, shape, f32)

    return {
        "x_prompt": nrm(ks[0], (BATCH, SEQ, D_MODEL), 1.0),
        "x_sample": nrm(ks[1], (DEC_BATCH, DEC_SEQ, D_MODEL), 1.0),
        "state_wkv": nrm(ks[2], (DEPTH, DEC_BATCH, A_HEADS, A_HEAD_DIM, A_HEAD_DIM), 0.3),
        "state_ret": nrm(ks[3], (DEPTH, DEC_BATCH, B_HEADS, B_QK_DIM, B_V_DIM), 1.0),
        "state_shift": nrm(ks[4], (DEPTH, DEC_BATCH, A_SHIFT_WIDTH), 1.0),
        "norm_mix_pre": gain(ks[5], (DEPTH, D_MODEL)),
        "norm_mix_post": gain(ks[6], (DEPTH, D_MODEL)),
        "norm_ffn_pre": gain(ks[7], (DEPTH, D_MODEL)),
        "norm_ffn_post": gain(ks[8], (DEPTH, D_MODEL)),
        "w_in": nrm(ks[9], (DEPTH, D_MODEL, IN_WIDTH), D_MODEL ** -0.5),
        "shift_mu": jax.random.uniform(ks[10], (DEPTH, A_SHIFT_WIDTH), f32),
        "decay_base": jax.random.uniform(ks[11], (DEPTH, A_WIDTH), f32, -6.0, -1.0),
        "w_decay_up": nrm(ks[12], (DEPTH, DECAY_RANK, A_WIDTH), 0.5 * DECAY_RANK ** -0.5),
        "iclr_base": nrm(ks[13], (DEPTH, A_WIDTH), 0.1),
        "w_iclr_up": nrm(ks[14], (DEPTH, ICLR_RANK, A_WIDTH), 0.5 * ICLR_RANK ** -0.5),
        "w_gate_up": nrm(ks[15], (DEPTH, GATE_RANK, A_WIDTH), GATE_RANK ** -0.5),
        "key_k": gain(ks[16], (DEPTH, A_WIDTH), 0.85),
        "key_a": gain(ks[17], (DEPTH, A_WIDTH)),
        "bonus_rk": nrm(ks[18], (DEPTH, A_HEADS, A_HEAD_DIM), 0.1),
        "lnx_gain": gain(ks[19], (DEPTH, A_WIDTH)),
        "lnx_bias": nrm(ks[20], (DEPTH, A_WIDTH), 0.02),
        "w_branch_a": nrm(ks[21], (DEPTH, A_WIDTH, D_MODEL), A_WIDTH ** -0.5),
        "ret_norm_gain": gain(ks[22], (DEPTH, B_V_WIDTH)),
        "w_branch_b": nrm(ks[23], (DEPTH, B_V_WIDTH, D_MODEL), B_V_WIDTH ** -0.5),
        "w_out": nrm(ks[24], (DEPTH, D_MODEL, D_MODEL), D_MODEL ** -0.5),
        "w_ffn_in": nrm(ks[25], (DEPTH, D_MODEL, 2 * D_FF), D_MODEL ** -0.5),
        "w_ffn_out": nrm(ks[26], (DEPTH, D_FF, D_MODEL), D_FF ** -0.5),
    }


def reference(x_prompt, x_sample, state_wkv, state_ret, state_shift, norm_mix_pre, norm_mix_post,
              norm_ffn_pre, norm_ffn_post, w_in, shift_mu, decay_base, w_decay_up, iclr_base, w_iclr_up,
              w_gate_up, key_k, key_a, bonus_rk, lnx_gain, lnx_bias, w_branch_a, ret_norm_gain, w_branch_b,
              w_out, w_ffn_in, w_ffn_out):
    f32 = jnp.float32
    Bp, Tp, _ = x_prompt.shape
    Bs, Ts, _ = x_sample.shape
    pos_p = jnp.arange(Tp, dtype=f32)
    pos_s = jnp.arange(Ts, dtype=f32) + jnp.float32(PAST_LEN)
    dt = x_prompt.dtype
    zero_shift = jnp.zeros((Bp, A_SHIFT_WIDTH), dt)
    zero_wkv = jnp.zeros((Bp, A_HEADS, A_HEAD_DIM, A_HEAD_DIM), dt)
    zero_ret = jnp.zeros((Bp, B_HEADS, B_QK_DIM, B_V_DIM), dt)
    xp, xs = x_prompt, x_sample
    wkv_p, ret_p, shf_p, wkv_s, ret_s, shf_s = [], [], [], [], [], []
    for l in range(DEPTH):
        weights = (norm_mix_pre[l], norm_mix_post[l], norm_ffn_pre[l], norm_ffn_post[l], w_in[l], shift_mu[l],
                   decay_base[l], w_decay_up[l], iclr_base[l], w_iclr_up[l], w_gate_up[l], key_k[l], key_a[l],
                   bonus_rk[l], lnx_gain[l], lnx_bias[l], w_branch_a[l], ret_norm_gain[l], w_branch_b[l],
                   w_out[l], w_ffn_in[l], w_ffn_out[l])
        xp, sp, Sp, Rp = trunk_layer(xp, pos_p, zero_shift, zero_wkv, zero_ret, *weights)
        xs, ss, Ss, Rs = trunk_layer(xs, pos_s, state_shift[l], state_wkv[l], state_ret[l], *weights)
        wkv_p.append(Sp); ret_p.append(Rp); shf_p.append(sp)
        wkv_s.append(Ss); ret_s.append(Rs); shf_s.append(ss)
    return (xp, xs, jnp.stack(wkv_p), jnp.stack(ret_p), jnp.stack(shf_p),
            jnp.stack(wkv_s), jnp.stack(ret_s), jnp.stack(shf_s))
```

```python
import functools
import math

import jax
import jax.numpy as jnp
from jax import lax
from jax.experimental import pallas as pl
from jax.experimental.pallas import tpu as pltpu

F32 = jnp.float32
BF16 = jnp.bfloat16

PAST_LEN = 4096
RMS_EPS = 1e-6
LNX_EPS = 64e-5
GN_EPS = 1e-5
ROPE_BASE = 10000.0

V7X_LANES = 128
V7X_MXU_DIM = 256
WKV_CHUNK = 64
VMEM_LIMIT_BYTES = 56 * 1024 * 1024


def _dot(a, b):
    return jnp.dot(a, b, preferred_element_type=F32)


def _dot_nt(a, b):
    return lax.dot_general(a, b, (((1,), (1,)), ((), ())), preferred_element_type=F32)


def _dot_tn(a, b):
    return lax.dot_general(a, b, (((0,), (0,)), ((), ())), preferred_element_type=F32)


def _rms(x, g):
    return x * lax.rsqrt(jnp.mean(x * x, axis=-1, keepdims=True) + RMS_EPS) * g


def _tile(n, target, mult=V7X_LANES):
    if n <= target:
        return n
    best = None
    for t in range(mult, target + 1, mult):
        if n % t == 0:
            best = t
    assert best is not None, (n, target, mult)
    return best


def _params(*sem):
    return pltpu.CompilerParams(dimension_semantics=sem, vmem_limit_bytes=VMEM_LIMIT_BYTES)


def _norm_matmul_kernel(x_ref, g_ref, w_ref, o_ref, h_ref):
    @pl.when(pl.program_id(1) == 0)
    def _():
        h_ref[...] = _rms(x_ref[...], g_ref[...]).astype(BF16)

    o_ref[...] = _dot(h_ref[...], w_ref[...]).astype(o_ref.dtype)


def _norm_matmul(x, g, w, out_dtype, tm_target=1024, tn_target=1024):
    m, d = x.shape
    n = w.shape[1]
    tm = _tile(m, tm_target, 8)
    tn = _tile(n, tn_target)
    return pl.pallas_call(
        _norm_matmul_kernel,
        grid=(m // tm, n // tn),
        in_specs=[
            pl.BlockSpec((tm, d), lambda i, j: (i, 0)),
            pl.BlockSpec((1, d), lambda i, j: (0, 0)),
            pl.BlockSpec((d, tn), lambda i, j: (0, j)),
        ],
        out_specs=pl.BlockSpec((tm, tn), lambda i, j: (i, j)),
        out_shape=jax.ShapeDtypeStruct((m, n), out_dtype),
        scratch_shapes=[pltpu.VMEM((tm, d), BF16)],
        compiler_params=_params("arbitrary", "arbitrary"),
        name="norm_matmul",
    )(x, g, w)


def _split_bf16(x):
    hi = x.astype(BF16)
    lo = (x - hi.astype(F32)).astype(BF16)
    return hi, lo


def _wkv_kernel(za_ref, sh0_ref, s0_ref, mu_ref, w0_ref, wdup_ref, a0_ref, waup_ref, wgup_ref,
                kk_ref, ka_ref, rk_ref, lg_ref, lb_ref,
                ya_ref, sho_ref, so_ref,
                carry_ref, s_ref, ar_ref, bk_ref, bkh_ref, v_ref, y_ref, pc_ref,
                *, heads, head_dim, ranks):
    tt = za_ref.shape[1]
    aw = heads * head_dim
    gw = V7X_MXU_DIM
    hpg = gw // head_dim
    groups = aw // gw
    c_len = min(WKV_CHUNK, tt)
    n_chunks = tt // c_len
    dr, ir, gr = ranks
    t = pl.program_id(1)

    @pl.when(t == 0)
    def _init():
        carry_ref[...] = sh0_ref[0]
        s_ref[...] = jnp.zeros_like(s_ref)
        for h in range(heads):
            g, hh = divmod(h, hpg)
            lo = hh * head_dim
            s_ref[g, lo:lo + head_dim, lo:lo + head_dim] = s0_ref[0, h]

    za = za_ref[0]
    row = lax.broadcasted_iota(jnp.int32, za.shape, 0)
    prev = jnp.where(row == 0, carry_ref[...], pltpu.roll(za, 1, 0))
    zs = za + mu_ref[...] * (prev - za)
    last = za[tt - 1:tt, :]
    carry_ref[...] = last
    sho_ref[0] = last

    r = zs[:, 0:aw]
    k = zs[:, aw:2 * aw]
    v = zs[:, 2 * aw:3 * aw]
    o = 3 * aw
    wd = zs[:, o:o + dr]
    ad = zs[:, o + dr:o + dr + ir]
    gd = zs[:, o + dr + ir:o + dr + ir + gr]

    xw = w0_ref[...] + _dot(jnp.tanh(wd).astype(BF16), wdup_ref[...])
    ld = (-math.exp(-0.5)) * jax.nn.sigmoid(xw)
    a_ic = jax.nn.sigmoid(a0_ref[...] + _dot(ad.astype(BF16), waup_ref[...]))
    gate = _dot(jax.nn.sigmoid(gd).astype(BF16), wgup_ref[...])

    gi = lax.broadcasted_iota(jnp.int32, (gw, gw), 0) // head_dim
    gj = lax.broadcasted_iota(jnp.int32, (gw, gw), 1) // head_dim
    same_head = gi == gj
    ones_bd = same_head.astype(BF16)

    def head_sum(x):
        hi, lo = _split_bf16(x)
        parts = []
        for g in range(groups):
            sl = slice(g * gw, (g + 1) * gw)
            parts.append(_dot(hi[:, sl], ones_bd) + _dot(lo[:, sl], ones_bd))
        return jnp.concatenate(parts, axis=1) if groups > 1 else parts[0]

    kk = k * kk_ref[...]
    k2 = k * (1.0 + (a_ic - 1.0) * ka_ref[...])
    kkn = kk / jnp.maximum(jnp.sqrt(head_sum(kk * kk)), 1e-12)
    a_vec = -kkn
    b_vec = kkn * a_ic

    ti = lax.broadcasted_iota(jnp.int32, (tt, tt), 0)
    tj = lax.broadcasted_iota(jnp.int32, (tt, tt), 1)
    same_chunk = (ti // c_len) == (tj // c_len)
    tri = (same_chunk & (tj <= ti)).astype(BF16)
    blk = same_chunk.astype(BF16)
    ld_hi, ld_lo = _split_bf16(ld)
    cum = _dot(tri, ld_hi) + _dot(tri, ld_lo)
    tot = _dot(blk, ld_hi) + _dot(blk, ld_lo)

    p_inc = jnp.exp(cum)
    p_inv = jnp.exp(-cum)
    to_end = jnp.exp(tot - cum)
    a_t = a_vec * (p_inc * jnp.exp(-ld))
    r_t = r * p_inc
    b_t = b_vec * p_inv
    k_t = k2 * p_inv
    b_h = b_vec * to_end
    k_h = k2 * to_end
    p_end = jnp.exp(tot)
    for c in range(n_chunks):
        rs = slice(c * c_len, (c + 1) * c_len)
        ar_ref[c, 0:c_len, :] = a_t[rs].astype(BF16)
        ar_ref[c, c_len:2 * c_len, :] = r_t[rs].astype(BF16)
        bk_ref[c, 0:c_len, :] = b_t[rs].astype(BF16)
        bk_ref[c, c_len:2 * c_len, :] = k_t[rs].astype(BF16)
        bkh_ref[c, 0:c_len, :] = b_h[rs].astype(BF16)
        bkh_ref[c, c_len:2 * c_len, :] = k_h[rs].astype(BF16)
        pc_ref[c] = p_end[c * c_len:c * c_len + 1, :]
    v_ref[...] = v.astype(BF16)

    lane_head = lax.broadcasted_iota(jnp.int32, (1, gw), 1) // head_dim
    ci = lax.broadcasted_iota(jnp.int32, (c_len, gw), 0)
    cj = lax.broadcasted_iota(jnp.int32, (c_len, gw), 1) % head_dim
    assert c_len == head_dim
    strict = cj < ci
    incl = cj <= ci
    eye = (cj == ci).astype(F32)
    n_doubling = int(math.log2(c_len))

    def bd(m):
        zero = jnp.zeros_like(m)
        return jnp.concatenate([jnp.where(lane_head == h, m, zero) for h in range(hpg)], axis=0)

    def chunk_body(c, carry):
        r0 = pl.multiple_of(c * c_len, c_len)
        for g in range(groups):
            sl = slice(g * gw, (g + 1) * gw)
            ar = ar_ref[c, :, sl]
            bk = bk_ref[c, :, sl]
            bkh = bkh_ref[c, :, sl]
            vv = v_ref[pl.ds(r0, c_len), sl]
            s0 = s_ref[g]
            m1 = _dot_nt(ar, s0.astype(BF16))
            bk_bd = jnp.concatenate([bd(bk[0:c_len]), bd(bk[c_len:2 * c_len])], axis=0)
            s4 = _dot_nt(ar, bk_bd)
            a_ab = jnp.where(strict, s4[0:c_len, 0:gw], 0.0)
            a_ak = jnp.where(strict, s4[0:c_len, gw:2 * gw], 0.0)
            w_rb = jnp.where(incl, s4[c_len:2 * c_len, 0:gw], 0.0)
            w_rk = jnp.where(incl, s4[c_len:2 * c_len, gw:2 * gw], 0.0)
            v_bd = bd(vv)
            x = m1[0:c_len] + _dot(a_ak.astype(BF16), v_bd)
            mk = a_ab
            tk = eye + mk
            for step in range(1, n_doubling):
                mb = mk.astype(BF16)
                mk = _dot(mb, bd(mb))
                tk = tk + _dot(mk.astype(BF16), bd(tk.astype(BF16)))
            u = _dot(tk.astype(BF16), bd(x.astype(BF16)))
            ub = u.astype(BF16)
            w_cat = jnp.concatenate([w_rb.astype(BF16), w_rk.astype(BF16)], axis=1)
            uv_bd = jnp.concatenate([bd(ub), v_bd], axis=0)
            y_ref[pl.ds(r0, c_len), sl] = m1[c_len:2 * c_len] + _dot(w_cat, uv_bd)
            uv = jnp.concatenate([ub, vv], axis=0)
            ds = _dot_tn(uv, bkh)
            s_ref[g] = s0 * pc_ref[c][:, sl] + jnp.where(same_head, ds, 0.0)
        return carry

    lax.fori_loop(0, n_chunks, chunk_body, 0)

    y = y_ref[...]
    inv_n = 1.0 / head_dim
    mean = head_sum(y) * inv_n
    d = y - mean
    var = head_sum(d * d) * inv_n
    yn = d * lax.rsqrt(var + LNX_EPS) * lg_ref[...] + lb_ref[...]
    bonus = head_sum(r * k2 * rk_ref[...]) * v
    ya_ref[0] = ((yn + bonus) * gate).astype(ya_ref.dtype)

    @pl.when(t == pl.num_programs(1) - 1)
    def _fin():
        for h in range(heads):
            g, hh = divmod(h, hpg)
            lo = hh * head_dim
            so_ref[0, h] = s_ref[g, lo:lo + head_dim, lo:lo + head_dim]


def _wkv_branch(za, shift0, s0, mu, w0, wdup, a0, waup, wgup, k_k, k_a, r_k, lnx_g, lnx_b):
    b, t, asw = za.shape
    heads, head_dim = s0.shape[1], s0.shape[2]
    aw = heads * head_dim
    ranks = (wdup.shape[0], waup.shape[0], wgup.shape[0])
    tt = _tile(t, 256, WKV_CHUNK)
    c_len = min(WKV_CHUNK, tt)
    n_chunks = tt // c_len
    groups = aw // V7X_MXU_DIM
    row = lambda a: a.reshape(1, -1)
    const = lambda shape: pl.BlockSpec(shape, lambda i, j: (0,) * len(shape))
    kern = functools.partial(_wkv_kernel, heads=heads, head_dim=head_dim, ranks=ranks)
    return pl.pallas_call(
        kern,
        grid=(b, t // tt),
        in_specs=[
            pl.BlockSpec((1, tt, asw), lambda i, j: (i, j, 0)),
            pl.BlockSpec((1, 1, asw), lambda i, j: (i, 0, 0)),
            pl.BlockSpec((1, heads, head_dim, head_dim), lambda i, j: (i, 0, 0, 0)),
            const((1, asw)), const((1, aw)), const(wdup.shape), const((1, aw)), const(waup.shape),
            const(wgup.shape), const((1, aw)), const((1, aw)), const((1, aw)), const((1, aw)),
            const((1, aw)),
        ],
        out_specs=[
            pl.BlockSpec((1, tt, aw), lambda i, j: (i, j, 0)),
            pl.BlockSpec((1, 1, asw), lambda i, j: (i, 0, 0)),
            pl.BlockSpec((1, heads, head_dim, head_dim), lambda i, j: (i, 0, 0, 0)),
        ],
        out_shape=[
            jax.ShapeDtypeStruct((b, t, aw), BF16),
            jax.ShapeDtypeStruct((b, 1, asw), F32),
            jax.ShapeDtypeStruct((b, heads, head_dim, head_dim), F32),
        ],
        scratch_shapes=[
            pltpu.VMEM((1, asw), F32),
            pltpu.VMEM((groups, V7X_MXU_DIM, V7X_MXU_DIM), F32),
            pltpu.VMEM((n_chunks, 2 * c_len, aw), BF16),
            pltpu.VMEM((n_chunks, 2 * c_len, aw), BF16),
            pltpu.VMEM((n_chunks, 2 * c_len, aw), BF16),
            pltpu.VMEM((tt, aw), BF16),
            pltpu.VMEM((tt, aw), F32),
            pltpu.VMEM((n_chunks, 1, aw), F32),
        ],
        compiler_params=_params("arbitrary", "arbitrary"),
        name="wkv7_chunked",
    )(za, shift0.reshape(b, 1, asw), s0, row(mu), row(w0), wdup, row(a0), waup, wgup,
      row(k_k), row(k_a), row(r_k), row(lnx_g), row(lnx_b))


def _ret_kernel(zb_ref, cos_ref, sin_ref, r0_ref, gn_ref, yb_ref, ro_ref, st_ref, *, heads, dk, dv):
    blk = zb_ref.shape[1]
    qkw = heads * dk
    vw = heads * dv
    t = pl.program_id(1)

    @pl.when(t == 0)
    def _init():
        st_ref[...] = r0_ref[0]

    cosv = cos_ref[...]
    sinv = sin_ref[...]
    diff = (lax.broadcasted_iota(jnp.int32, (blk, blk), 0)
            - lax.broadcasted_iota(jnp.int32, (blk, blk), 1)).astype(F32)
    pos = lax.broadcasted_iota(jnp.int32, (blk, dk), 0).astype(F32)
    for h in range(heads):
        log_gamma = math.log1p(-(2.0 ** (-5.0 - h)))
        qh = zb_ref[0, :, h * dk:(h + 1) * dk].astype(F32)
        kh = zb_ref[0, :, qkw + h * dk:qkw + (h + 1) * dk].astype(F32)
        vh = zb_ref[0, :, 2 * qkw + h * dv:2 * qkw + (h + 1) * dv]
        gate = zb_ref[0, :, 2 * qkw + vw + h * dv:2 * qkw + vw + (h + 1) * dv].astype(F32)
        qr = (qh * cosv + pltpu.roll(qh, dk // 2, 1) * sinv) * (dk ** -0.5)
        kr = kh * cosv + pltpu.roll(kh, dk // 2, 1) * sinv
        dmask = jnp.where(diff >= 0, jnp.exp(jnp.maximum(diff, 0.0) * log_gamma), 0.0)
        scores = _dot_nt(qr.astype(BF16), kr.astype(BF16)) * dmask
        inner = _dot(scores.astype(BF16), vh)
        st = st_ref[h]
        cross = _dot((qr * jnp.exp((pos + 1.0) * log_gamma)).astype(BF16), st.astype(BF16))
        ke = kr * jnp.exp((blk - 1.0 - pos) * log_gamma)
        st_ref[h] = math.exp(blk * log_gamma) * st + _dot_tn(ke.astype(BF16), vh)
        o = inner + cross
        mean = jnp.mean(o, axis=-1, keepdims=True)
        d = o - mean
        var = jnp.mean(d * d, axis=-1, keepdims=True)
        on = d * lax.rsqrt(var + GN_EPS) * gn_ref[:, h * dv:(h + 1) * dv]
        yb_ref[0, :, h * dv:(h + 1) * dv] = (jax.nn.silu(gate) * on).astype(yb_ref.dtype)

    @pl.when(t == pl.num_programs(1) - 1)
    def _fin():
        ro_ref[0] = st_ref[...]


def _ret_branch(zb, cos_t, sin_t, r0, gn_g):
    b, t, bin_w = zb.shape
    heads, dk, dv = r0.shape[1], r0.shape[2], r0.shape[3]
    vw = heads * dv
    blk = _tile(t, 256, 64)
    kern = functools.partial(_ret_kernel, heads=heads, dk=dk, dv=dv)
    return pl.pallas_call(
        kern,
        grid=(b, t // blk),
        in_specs=[
            pl.BlockSpec((1, blk, bin_w), lambda i, j: (i, j, 0)),
            pl.BlockSpec((blk, dk), lambda i, j: (j, 0)),
            pl.BlockSpec((blk, dk), lambda i, j: (j, 0)),
            pl.BlockSpec((1, heads, dk, dv), lambda i, j: (i, 0, 0, 0)),
            pl.BlockSpec((1, vw), lambda i, j: (0, 0)),
        ],
        out_specs=[
            pl.BlockSpec((1, blk, vw), lambda i, j: (i, j, 0)),
            pl.BlockSpec((1, heads, dk, dv), lambda i, j: (i, 0, 0, 0)),
        ],
        out_shape=[
            jax.ShapeDtypeStruct((b, t, vw), BF16),
            jax.ShapeDtypeStruct((b, heads, dk, dv), F32),
        ],
        scratch_shapes=[pltpu.VMEM((heads, dk, dv), F32)],
        compiler_params=_params("arbitrary", "arbitrary"),
        name="retention_chunked",
    )(zb, cos_t, sin_t, r0, gn_g.reshape(1, -1))


def _merge_kernel(ya_ref, yb_ref, zg_ref, x_ref, wba_ref, wbb_ref, wo_ref, g_ref, o_ref):
    d = x_ref.shape[1]
    ga = jax.nn.sigmoid(zg_ref[:, 0:d].astype(F32))
    gb = jax.nn.sigmoid(zg_ref[:, d:2 * d].astype(F32))
    merged = ga * _dot(ya_ref[...], wba_ref[...]) + gb * _dot(yb_ref[...], wbb_ref[...])
    mo = _dot(merged.astype(BF16), wo_ref[...])
    o_ref[...] = x_ref[...] + _rms(mo, g_ref[...])


def _merge(ya, yb, zg, x, wba, wbb, wo, g):
    m, d = x.shape
    tm = _tile(m, 256, 8)
    tok = lambda w: pl.BlockSpec((tm, w), lambda i: (i, 0))
    res = lambda a: pl.BlockSpec(a.shape, lambda i: (0, 0), pipeline_mode=pl.Buffered(1))
    return pl.pallas_call(
        _merge_kernel,
        grid=(m // tm,),
        in_specs=[tok(ya.shape[1]), tok(yb.shape[1]), tok(zg.shape[1]), tok(d),
                  res(wba), res(wbb), res(wo), pl.BlockSpec((1, d), lambda i: (0, 0))],
        out_specs=tok(d),
        out_shape=jax.ShapeDtypeStruct((m, d), F32),
        compiler_params=_params("arbitrary"),
        name="merge_out",
    )(ya, yb, zg, x, wba, wbb, wo, g)


def _ffn_kernel(x_ref, gpre_ref, wg_ref, wu_ref, wo_ref, gpost_ref, o_ref, h_ref, acc_ref):
    j = pl.program_id(1)

    @pl.when(j == 0)
    def _():
        h_ref[...] = _rms(x_ref[...], gpre_ref[...]).astype(BF16)
        acc_ref[...] = jnp.zeros_like(acc_ref)

    h = h_ref[...]
    act = jax.nn.silu(_dot(h, wg_ref[...])) * _dot(h, wu_ref[...])
    acc_ref[...] += _dot(act.astype(BF16), wo_ref[...])

    @pl.when(j == pl.num_programs(1) - 1)
    def _():
        o_ref[...] = x_ref[...] + _rms(acc_ref[...], gpost_ref[...])


def _ffn(x, gpre, w_fi, w_fo, gpost):
    m, d = x.shape
    f = w_fo.shape[0]
    tm = _tile(m, 512, 8)
    tf = _tile(f, 512)
    nf = f // tf
    return pl.pallas_call(
        _ffn_kernel,
        grid=(m // tm, nf),
        in_specs=[
            pl.BlockSpec((tm, d), lambda i, j: (i, 0)),
            pl.BlockSpec((1, d), lambda i, j: (0, 0)),
            pl.BlockSpec((d, tf), lambda i, j: (0, j)),
            pl.BlockSpec((d, tf), lambda i, j: (0, j + nf)),
            pl.BlockSpec((tf, d), lambda i, j: (j, 0)),
            pl.BlockSpec((1, d), lambda i, j: (0, 0)),
        ],
        out_specs=pl.BlockSpec((tm, d), lambda i, j: (i, 0)),
        out_shape=jax.ShapeDtypeStruct((m, d), F32),
        scratch_shapes=[pltpu.VMEM((tm, d), BF16), pltpu.VMEM((tm, d), F32)],
        compiler_params=_params("arbitrary", "arbitrary"),
        name="swiglu_ffn",
    )(x, gpre, w_fi, w_fi, w_fo, gpost)


def _rope_tables(pos, dk):
    half = dk // 2
    inv = ROPE_BASE ** (-jnp.arange(half, dtype=F32) / half)
    ang = pos[:, None] * inv[None, :]
    cos = jnp.cos(ang)
    sin = jnp.sin(ang)
    return jnp.concatenate([cos, cos], axis=-1), jnp.concatenate([-sin, sin], axis=-1)


def _layer(x, rope, shift0, s0, r0, w):
    b, t, d = x.shape
    x2 = x.reshape(b * t, d)
    row = lambda a: a.reshape(1, -1)
    za = _norm_matmul(x2, row(w["nmp"]), w["w_a"], F32, tm_target=512, tn_target=1664)
    zb = _norm_matmul(x2, row(w["nmp"]), w["w_b"], BF16)
    zg = _norm_matmul(x2, row(w["nmp"]), w["w_g"], BF16)
    ya, shift, s_new = _wkv_branch(za.reshape(b, t, -1), shift0, s0, w["mu"], w["w0"], w["w_dup"],
                                   w["a0"], w["w_aup"], w["w_gup"], w["k_k"], w["k_a"], w["r_k"],
                                   w["lnx_g"], w["lnx_b"])
    yb, r_new = _ret_branch(zb.reshape(b, t, -1), rope[0], rope[1], r0, w["ret_g"])
    x1 = _merge(ya.reshape(b * t, -1), yb.reshape(b * t, -1), zg, x2, w["w_ba"], w["w_bb"],
                w["w_out"], row(w["nmq"]))
    xo = _ffn(x1, row(w["nfp"]), w["w_fi"], w["w_fo"], row(w["nfq"]))
    return xo.reshape(b, t, d), shift.reshape(b, -1), s_new, r_new


def kernel(x_prompt, x_sample, state_wkv, state_ret, state_shift, norm_mix_pre, norm_mix_post, norm_ffn_pre, norm_ffn_post, w_in, shift_mu, decay_base, w_decay_up, iclr_base, w_iclr_up, w_gate_up, key_k, key_a, bonus_rk, lnx_gain, lnx_bias, w_branch_a, ret_norm_gain, w_branch_b, w_out, w_ffn_in, w_ffn_out):
    depth = w_in.shape[0]
    bp, tp, _ = x_prompt.shape
    bs, ts, _ = x_sample.shape
    asw = shift_mu.shape[1]
    heads_a, hd_a = bonus_rk.shape[1], bonus_rk.shape[2]
    heads_b, dk, dv = state_ret.shape[2], state_ret.shape[3], state_ret.shape[4]
    b_in = 2 * heads_b * dk + 2 * heads_b * dv
    dt = x_prompt.dtype

    rope_p = _rope_tables(jnp.arange(tp, dtype=F32), dk)
    rope_s = _rope_tables(jnp.arange(ts, dtype=F32) + jnp.float32(PAST_LEN), dk)
    zero_shift = jnp.zeros((bp, asw), dt)
    zero_wkv = jnp.zeros((bp, heads_a, hd_a, hd_a), dt)
    zero_ret = jnp.zeros((bp, heads_b, dk, dv), dt)

    xp, xs = x_prompt, x_sample
    outs = [[] for _ in range(6)]
    for l in range(depth):
        w = dict(
            nmp=norm_mix_pre[l], nmq=norm_mix_post[l], nfp=norm_ffn_pre[l], nfq=norm_ffn_post[l],
            w_a=w_in[l, :, :asw].astype(BF16),
            w_b=w_in[l, :, asw:asw + b_in].astype(BF16),
            w_g=w_in[l, :, asw + b_in:].astype(BF16),
            mu=shift_mu[l], w0=decay_base[l], w_dup=w_decay_up[l].astype(BF16), a0=iclr_base[l],
            w_aup=w_iclr_up[l].astype(BF16), w_gup=w_gate_up[l].astype(BF16), k_k=key_k[l],
            k_a=key_a[l], r_k=bonus_rk[l], lnx_g=lnx_gain[l], lnx_b=lnx_bias[l],
            w_ba=w_branch_a[l].astype(BF16), ret_g=ret_norm_gain[l],
            w_bb=w_branch_b[l].astype(BF16), w_out=w_out[l].astype(BF16),
            w_fi=w_ffn_in[l].astype(BF16), w_fo=w_ffn_out[l].astype(BF16),
        )
        xp, sp, wkp, rtp = _layer(xp, rope_p, zero_shift, zero_wkv, zero_ret, w)
        xs, ss, wks, rts = _layer(xs, rope_s, state_shift[l], state_wkv[l], state_ret[l], w)
        for lst, val in zip(outs, (wkp, rtp, sp, wks, rts, ss)):
            lst.append(val)
    return (xp, xs) + tuple(jnp.stack(o) for o in outs)
```

```python
import functools
import math

import jax
import jax.numpy as jnp
from jax import lax
from jax.experimental import pallas as pl
from jax.experimental.pallas import tpu as pltpu

F32 = jnp.float32
BF16 = jnp.bfloat16

PAST_LEN = 4096
RMS_EPS = 1e-6
LNX_EPS = 64e-5
GN_EPS = 1e-5
ROPE_BASE = 10000.0

V7X_LANES = 128
V7X_MXU_DIM = 256
WKV_CHUNK = 64
VMEM_LIMIT_BYTES = 56 * 1024 * 1024


def _dot(a, b):
    return jnp.dot(a, b, preferred_element_type=F32)


def _dot_nt(a, b):
    return lax.dot_general(a, b, (((1,), (1,)), ((), ())), preferred_element_type=F32)


def _dot_tn(a, b):
    return lax.dot_general(a, b, (((0,), (0,)), ((), ())), preferred_element_type=F32)


def _rms(x, g):
    return x * lax.rsqrt(jnp.mean(x * x, axis=-1, keepdims=True) + RMS_EPS) * g


def _tile(n, target, mult=V7X_LANES):
    if n <= target:
        return n
    best = None
    for t in range(mult, target + 1, mult):
        if n % t == 0:
            best = t
    assert best is not None, (n, target, mult)
    return best


def _params(*sem):
    return pltpu.CompilerParams(dimension_semantics=sem, vmem_limit_bytes=VMEM_LIMIT_BYTES)


def _norm_matmul_kernel(x_ref, g_ref, w_ref, o_ref, h_ref):
    @pl.when(pl.program_id(1) == 0)
    def _():
        h_ref[...] = _rms(x_ref[...], g_ref[...]).astype(BF16)

    o_ref[...] = _dot(h_ref[...], w_ref[...]).astype(o_ref.dtype)


def _norm_matmul(x, g, w, out_dtype, tm_target=1024, tn_target=1024):
    m, d = x.shape
    n = w.shape[1]
    tm = _tile(m, tm_target, 8)
    tn = _tile(n, tn_target)
    return pl.pallas_call(
        _norm_matmul_kernel,
        grid=(m // tm, n // tn),
        in_specs=[
            pl.BlockSpec((tm, d), lambda i, j: (i, 0)),
            pl.BlockSpec((1, d), lambda i, j: (0, 0)),
            pl.BlockSpec((d, tn), lambda i, j: (0, j)),
        ],
        out_specs=pl.BlockSpec((tm, tn), lambda i, j: (i, j)),
        out_shape=jax.ShapeDtypeStruct((m, n), out_dtype),
        scratch_shapes=[pltpu.VMEM((tm, d), BF16)],
        compiler_params=_params("arbitrary", "arbitrary"),
        name="norm_matmul",
    )(x, g, w)


def _split_bf16(x):
    hi = x.astype(BF16)
    lo = (x - hi.astype(F32)).astype(BF16)
    return hi, lo


def _wkv_kernel(za_ref, sh0_ref, s0_ref, mu_ref, w0_ref, wdup_ref, a0_ref, waup_ref, wgup_ref,
                kk_ref, ka_ref, rk_ref, lg_ref, lb_ref,
                ya_ref, sho_ref, so_ref,
                carry_ref, s_ref, ar_ref, bk_ref, bkh_ref, v_ref, y_ref, pc_ref, t_ref, av_ref, wcat_ref,
                *, heads, head_dim, ranks):
    tt = za_ref.shape[1]
    aw = heads * head_dim
    gw = V7X_MXU_DIM
    hpg = gw // head_dim
    groups = aw // gw
    c_len = min(WKV_CHUNK, tt)
    n_chunks = tt // c_len
    dr, ir, gr = ranks
    t = pl.program_id(1)

    @pl.when(t == 0)
    def _init():
        carry_ref[...] = sh0_ref[0]
        s_ref[...] = jnp.zeros_like(s_ref)
        for h in range(heads):
            g, hh = divmod(h, hpg)
            lo = hh * head_dim
            s_ref[g, lo:lo + head_dim, lo:lo + head_dim] = s0_ref[0, h]

    za = za_ref[0]
    row = lax.broadcasted_iota(jnp.int32, za.shape, 0)
    prev = jnp.where(row == 0, carry_ref[...], pltpu.roll(za, 1, 0))
    zs = za + mu_ref[...] * (prev - za)
    last = za[tt - 1:tt, :]
    carry_ref[...] = last
    sho_ref[0] = last

    r = zs[:, 0:aw]
    k = zs[:, aw:2 * aw]
    v = zs[:, 2 * aw:3 * aw]
    o = 3 * aw
    wd = zs[:, o:o + dr]
    ad = zs[:, o + dr:o + dr + ir]
    gd = zs[:, o + dr + ir:o + dr + ir + gr]

    xw = w0_ref[...] + _dot(jnp.tanh(wd).astype(BF16), wdup_ref[...])
    ld = (-math.exp(-0.5)) * jax.nn.sigmoid(xw)
    a_ic = jax.nn.sigmoid(a0_ref[...] + _dot(ad.astype(BF16), waup_ref[...]))
    gate = _dot(jax.nn.sigmoid(gd).astype(BF16), wgup_ref[...])

    gi = lax.broadcasted_iota(jnp.int32, (gw, gw), 0) // head_dim
    gj = lax.broadcasted_iota(jnp.int32, (gw, gw), 1) // head_dim
    same_head = gi == gj
    ones_bd = same_head.astype(BF16)

    def head_sum(x):
        hi, lo = _split_bf16(x)
        parts = []
        for g in range(groups):
            sl = slice(g * gw, (g + 1) * gw)
            parts.append(_dot(hi[:, sl], ones_bd) + _dot(lo[:, sl], ones_bd))
        return jnp.concatenate(parts, axis=1) if groups > 1 else parts[0]

    kk = k * kk_ref[...]
    k2 = k * (1.0 + (a_ic - 1.0) * ka_ref[...])
    kkn = kk / jnp.maximum(jnp.sqrt(head_sum(kk * kk)), 1e-12)
    a_vec = -kkn
    b_vec = kkn * a_ic

    ti = lax.broadcasted_iota(jnp.int32, (tt, tt), 0)
    tj = lax.broadcasted_iota(jnp.int32, (tt, tt), 1)
    same_chunk = (ti // c_len) == (tj // c_len)
    tri = (same_chunk & (tj <= ti)).astype(BF16)
    blk = same_chunk.astype(BF16)
    ld_hi, ld_lo = _split_bf16(ld)
    cum = _dot(tri, ld_hi) + _dot(tri, ld_lo)
    tot = _dot(blk, ld_hi) + _dot(blk, ld_lo)

    p_inc = jnp.exp(cum)
    p_inv = jnp.exp(-cum)
    to_end = jnp.exp(tot - cum)
    a_t = a_vec * (p_inc * jnp.exp(-ld))
    r_t = r * p_inc
    b_t = b_vec * p_inv
    k_t = k2 * p_inv
    b_h = b_vec * to_end
    k_h = k2 * to_end
    p_end = jnp.exp(tot)
    for c in range(n_chunks):
        rs = slice(c * c_len, (c + 1) * c_len)
        ar_ref[c, 0:c_len, :] = a_t[rs].astype(BF16)
        ar_ref[c, c_len:2 * c_len, :] = r_t[rs].astype(BF16)
        bk_ref[c, 0:c_len, :] = b_t[rs].astype(BF16)
        bk_ref[c, c_len:2 * c_len, :] = k_t[rs].astype(BF16)
        bkh_ref[c, 0:c_len, :] = b_h[rs].astype(BF16)
        bkh_ref[c, c_len:2 * c_len, :] = k_h[rs].astype(BF16)
        pc_ref[c] = p_end[c * c_len:c * c_len + 1, :]
    v_ref[...] = v.astype(BF16)

    lane_head = lax.broadcasted_iota(jnp.int32, (1, gw), 1) // head_dim
    ci = lax.broadcasted_iota(jnp.int32, (c_len, gw), 0)
    cj = lax.broadcasted_iota(jnp.int32, (c_len, gw), 1) % head_dim
    assert c_len == head_dim
    strict = cj < ci
    incl = cj <= ci
    eye = (cj == ci).astype(F32)
    n_doubling = int(math.log2(c_len))

    def bd(m):
        zero = jnp.zeros_like(m)
        return jnp.concatenate([jnp.where(lane_head == h, m, zero) for h in range(hpg)], axis=0)

    gs = range(groups)
    sls = [slice(g * gw, (g + 1) * gw) for g in gs]

    def prep_body(c, carry):
        r0 = pl.multiple_of(c * c_len, c_len)
        ar = [ar_ref[c, :, sl] for sl in sls]
        bk = [bk_ref[c, :, sl] for sl in sls]
        vv = [v_ref[pl.ds(r0, c_len), sl] for sl in sls]
        s4 = [_dot_nt(ar[g], jnp.concatenate([bd(bk[g][0:c_len]), bd(bk[g][c_len:2 * c_len])],
                                              axis=0)) for g in gs]
        a_ab = [jnp.where(strict, s4[g][0:c_len, 0:gw], 0.0) for g in gs]
        a_ak = [jnp.where(strict, s4[g][0:c_len, gw:2 * gw], 0.0).astype(BF16) for g in gs]
        for g in gs:
            wcat_ref[c, g, :, 0:gw] = jnp.where(incl, s4[g][c_len:2 * c_len, 0:gw], 0.0).astype(BF16)
            wcat_ref[c, g, :, gw:2 * gw] = jnp.where(incl, s4[g][c_len:2 * c_len, gw:2 * gw],
                                                     0.0).astype(BF16)
        av = [_dot(a_ak[g], bd(vv[g])) for g in gs]
        for g in gs:
            av_ref[c, g] = av[g]
        mk = a_ab
        tk = [eye + mk[g] for g in gs]
        for step in range(1, n_doubling):
            mb = [mk[g].astype(BF16) for g in gs]
            mk = [_dot(mb[g], bd(mb[g])) for g in gs]
            mt = [_dot(mk[g].astype(BF16), bd(tk[g].astype(BF16))) for g in gs]
            tk = [tk[g] + mt[g] for g in gs]
        for g in gs:
            t_ref[c, g] = tk[g].astype(BF16)
        return carry

    lax.fori_loop(0, n_chunks, prep_body, 0)

    def chunk_body(c, carry):
        r0 = pl.multiple_of(c * c_len, c_len)
        ar = [ar_ref[c, :, sl] for sl in sls]
        vv = [v_ref[pl.ds(r0, c_len), sl] for sl in sls]
        s0 = [s_ref[g] for g in gs]
        m1 = [_dot_nt(ar[g], s0[g].astype(BF16)) for g in gs]
        xb = [(m1[g][0:c_len] + av_ref[c, g]).astype(BF16) for g in gs]
        ub = [_dot(t_ref[c, g], bd(xb[g])).astype(BF16) for g in gs]
        ds = [_dot_tn(jnp.concatenate([ub[g], vv[g]], axis=0), bkh_ref[c, :, sls[g]]) for g in gs]
        for g in gs:
            s_ref[g] = s0[g] * pc_ref[c][:, sls[g]] + jnp.where(same_head, ds[g], 0.0)
        yy = [_dot(wcat_ref[c, g], jnp.concatenate([bd(ub[g]), bd(vv[g])], axis=0)) for g in gs]
        for g in gs:
            y_ref[pl.ds(r0, c_len), sls[g]] = m1[g][c_len:2 * c_len] + yy[g]
        return carry

    lax.fori_loop(0, n_chunks, chunk_body, 0)

    y = y_ref[...]
    inv_n = 1.0 / head_dim
    mean = head_sum(y) * inv_n
    d = y - mean
    var = head_sum(d * d) * inv_n
    yn = d * lax.rsqrt(var + LNX_EPS) * lg_ref[...] + lb_ref[...]
    bonus = head_sum(r * k2 * rk_ref[...]) * v
    ya_ref[0] = ((yn + bonus) * gate).astype(ya_ref.dtype)

    @pl.when(t == pl.num_programs(1) - 1)
    def _fin():
        for h in range(heads):
            g, hh = divmod(h, hpg)
            lo = hh * head_dim
            so_ref[0, h] = s_ref[g, lo:lo + head_dim, lo:lo + head_dim]


def _wkv_branch(za, shift0, s0, mu, w0, wdup, a0, waup, wgup, k_k, k_a, r_k, lnx_g, lnx_b):
    b, t, asw = za.shape
    heads, head_dim = s0.shape[1], s0.shape[2]
    aw = heads * head_dim
    ranks = (wdup.shape[0], waup.shape[0], wgup.shape[0])
    tt = _tile(t, 256, WKV_CHUNK)
    c_len = min(WKV_CHUNK, tt)
    n_chunks = tt // c_len
    groups = aw // V7X_MXU_DIM
    row = lambda a: a.reshape(1, -1)
    const = lambda shape: pl.BlockSpec(shape, lambda i, j: (0,) * len(shape))
    kern = functools.partial(_wkv_kernel, heads=heads, head_dim=head_dim, ranks=ranks)
    return pl.pallas_call(
        kern,
        grid=(b, t // tt),
        in_specs=[
            pl.BlockSpec((1, tt, asw), lambda i, j: (i, j, 0)),
            pl.BlockSpec((1, 1, asw), lambda i, j: (i, 0, 0)),
            pl.BlockSpec((1, heads, head_dim, head_dim), lambda i, j: (i, 0, 0, 0)),
            const((1, asw)), const((1, aw)), const(wdup.shape), const((1, aw)), const(waup.shape),
            const(wgup.shape), const((1, aw)), const((1, aw)), const((1, aw)), const((1, aw)),
            const((1, aw)),
        ],
        out_specs=[
            pl.BlockSpec((1, tt, aw), lambda i, j: (i, j, 0)),
            pl.BlockSpec((1, 1, asw), lambda i, j: (i, 0, 0)),
            pl.BlockSpec((1, heads, head_dim, head_dim), lambda i, j: (i, 0, 0, 0)),
        ],
        out_shape=[
            jax.ShapeDtypeStruct((b, t, aw), BF16),
            jax.ShapeDtypeStruct((b, 1, asw), F32),
            jax.ShapeDtypeStruct((b, heads, head_dim, head_dim), F32),
        ],
        scratch_shapes=[
            pltpu.VMEM((1, asw), F32),
            pltpu.VMEM((groups, V7X_MXU_DIM, V7X_MXU_DIM), F32),
            pltpu.VMEM((n_chunks, 2 * c_len, aw), BF16),
            pltpu.VMEM((n_chunks, 2 * c_len, aw), BF16),
            pltpu.VMEM((n_chunks, 2 * c_len, aw), BF16),
            pltpu.VMEM((tt, aw), BF16),
            pltpu.VMEM((tt, aw), F32),
            pltpu.VMEM((n_chunks, 1, aw), F32),
            pltpu.VMEM((n_chunks, groups, c_len, V7X_MXU_DIM), BF16),
            pltpu.VMEM((n_chunks, groups, c_len, V7X_MXU_DIM), F32),
            pltpu.VMEM((n_chunks, groups, c_len, 2 * V7X_MXU_DIM), BF16),
        ],
        compiler_params=_params("arbitrary", "arbitrary"),
        name="wkv7_chunked",
    )(za, shift0.reshape(b, 1, asw), s0, row(mu), row(w0), wdup, row(a0), waup, wgup,
      row(k_k), row(k_a), row(r_k), row(lnx_g), row(lnx_b))


def _ret_kernel(zb_ref, cos_ref, sin_ref, r0_ref, gn_ref, yb_ref, ro_ref, st_ref, *, heads, dk, dv):
    blk = zb_ref.shape[1]
    qkw = heads * dk
    vw = heads * dv
    t = pl.program_id(1)

    @pl.when(t == 0)
    def _init():
        st_ref[...] = r0_ref[0]

    cosv = cos_ref[...]
    sinv = sin_ref[...]
    diff = (lax.broadcasted_iota(jnp.int32, (blk, blk), 0)
            - lax.broadcasted_iota(jnp.int32, (blk, blk), 1)).astype(F32)
    pos = lax.broadcasted_iota(jnp.int32, (blk, dk), 0).astype(F32)
    for h in range(heads):
        log_gamma = math.log1p(-(2.0 ** (-5.0 - h)))
        qh = zb_ref[0, :, h * dk:(h + 1) * dk].astype(F32)
        kh = zb_ref[0, :, qkw + h * dk:qkw + (h + 1) * dk].astype(F32)
        vh = zb_ref[0, :, 2 * qkw + h * dv:2 * qkw + (h + 1) * dv]
        gate = zb_ref[0, :, 2 * qkw + vw + h * dv:2 * qkw + vw + (h + 1) * dv].astype(F32)
        qr = (qh * cosv + pltpu.roll(qh, dk // 2, 1) * sinv) * (dk ** -0.5)
        kr = kh * cosv + pltpu.roll(kh, dk // 2, 1) * sinv
        dmask = jnp.where(diff >= 0, jnp.exp(jnp.maximum(diff, 0.0) * log_gamma), 0.0)
        scores = _dot_nt(qr.astype(BF16), kr.astype(BF16)) * dmask
        inner = _dot(scores.astype(BF16), vh)
        st = st_ref[h]
        cross = _dot((qr * jnp.exp((pos + 1.0) * log_gamma)).astype(BF16), st.astype(BF16))
        ke = kr * jnp.exp((blk - 1.0 - pos) * log_gamma)
        st_ref[h] = math.exp(blk * log_gamma) * st + _dot_tn(ke.astype(BF16), vh)
        o = inner + cross
        mean = jnp.mean(o, axis=-1, keepdims=True)
        d = o - mean
        var = jnp.mean(d * d, axis=-1, keepdims=True)
        on = d * lax.rsqrt(var + GN_EPS) * gn_ref[:, h * dv:(h + 1) * dv]
        yb_ref[0, :, h * dv:(h + 1) * dv] = (jax.nn.silu(gate) * on).astype(yb_ref.dtype)

    @pl.when(t == pl.num_programs(1) - 1)
    def _fin():
        ro_ref[0] = st_ref[...]


def _ret_branch(zb, cos_t, sin_t, r0, gn_g):
    b, t, bin_w = zb.shape
    heads, dk, dv = r0.shape[1], r0.shape[2], r0.shape[3]
    vw = heads * dv
    blk = _tile(t, 256, 64)
    kern = functools.partial(_ret_kernel, heads=heads, dk=dk, dv=dv)
    return pl.pallas_call(
        kern,
        grid=(b, t // blk),
        in_specs=[
            pl.BlockSpec((1, blk, bin_w), lambda i, j: (i, j, 0)),
            pl.BlockSpec((blk, dk), lambda i, j: (j, 0)),
            pl.BlockSpec((blk, dk), lambda i, j: (j, 0)),
            pl.BlockSpec((1, heads, dk, dv), lambda i, j: (i, 0, 0, 0)),
            pl.BlockSpec((1, vw), lambda i, j: (0, 0)),
        ],
        out_specs=[
            pl.BlockSpec((1, blk, vw), lambda i, j: (i, j, 0)),
            pl.BlockSpec((1, heads, dk, dv), lambda i, j: (i, 0, 0, 0)),
        ],
        out_shape=[
            jax.ShapeDtypeStruct((b, t, vw), BF16),
            jax.ShapeDtypeStruct((b, heads, dk, dv), F32),
        ],
        scratch_shapes=[pltpu.VMEM((heads, dk, dv), F32)],
        compiler_params=_params("arbitrary", "arbitrary"),
        name="retention_chunked",
    )(zb, cos_t, sin_t, r0, gn_g.reshape(1, -1))


def _merge_kernel(ya_ref, yb_ref, zg_ref, x_ref, wba_ref, wbb_ref, wo_ref, g_ref, o_ref):
    d = x_ref.shape[1]
    ga = jax.nn.sigmoid(zg_ref[:, 0:d].astype(F32))
    gb = jax.nn.sigmoid(zg_ref[:, d:2 * d].astype(F32))
    merged = ga * _dot(ya_ref[...], wba_ref[...]) + gb * _dot(yb_ref[...], wbb_ref[...])
    mo = _dot(merged.astype(BF16), wo_ref[...])
    o_ref[...] = x_ref[...] + _rms(mo, g_ref[...])


def _merge(ya, yb, zg, x, wba, wbb, wo, g):
    m, d = x.shape
    tm = _tile(m, 256, 8)
    tok = lambda w: pl.BlockSpec((tm, w), lambda i: (i, 0))
    res = lambda a: pl.BlockSpec(a.shape, lambda i: (0, 0), pipeline_mode=pl.Buffered(1))
    return pl.pallas_call(
        _merge_kernel,
        grid=(m // tm,),
        in_specs=[tok(ya.shape[1]), tok(yb.shape[1]), tok(zg.shape[1]), tok(d),
                  res(wba), res(wbb), res(wo), pl.BlockSpec((1, d), lambda i: (0, 0))],
        out_specs=tok(d),
        out_shape=jax.ShapeDtypeStruct((m, d), F32),
        compiler_params=_params("arbitrary"),
        name="merge_out",
    )(ya, yb, zg, x, wba, wbb, wo, g)


def _ffn_kernel(x_ref, gpre_ref, wg_ref, wu_ref, wo_ref, gpost_ref, o_ref, h_ref, acc_ref):
    j = pl.program_id(1)

    @pl.when(j == 0)
    def _():
        h_ref[...] = _rms(x_ref[...], gpre_ref[...]).astype(BF16)
        acc_ref[...] = jnp.zeros_like(acc_ref)

    h = h_ref[...]
    act = jax.nn.silu(_dot(h, wg_ref[...])) * _dot(h, wu_ref[...])
    acc_ref[...] += _dot(act.astype(BF16), wo_ref[...])

    @pl.when(j == pl.num_programs(1) - 1)
    def _():
        o_ref[...] = x_ref[...] + _rms(acc_ref[...], gpost_ref[...])


def _ffn(x, gpre, w_fi, w_fo, gpost):
    m, d = x.shape
    f = w_fo.shape[0]
    tm = _tile(m, 512, 8)
    tf = _tile(f, 512)
    nf = f // tf
    return pl.pallas_call(
        _ffn_kernel,
        grid=(m // tm, nf),
        in_specs=[
            pl.BlockSpec((tm, d), lambda i, j: (i, 0)),
            pl.BlockSpec((1, d), lambda i, j: (0, 0)),
            pl.BlockSpec((d, tf), lambda i, j: (0, j)),
            pl.BlockSpec((d, tf), lambda i, j: (0, j + nf)),
            pl.BlockSpec((tf, d), lambda i, j: (j, 0)),
            pl.BlockSpec((1, d), lambda i, j: (0, 0)),
        ],
        out_specs=pl.BlockSpec((tm, d), lambda i, j: (i, 0)),
        out_shape=jax.ShapeDtypeStruct((m, d), F32),
        scratch_shapes=[pltpu.VMEM((tm, d), BF16), pltpu.VMEM((tm, d), F32)],
        compiler_params=_params("arbitrary", "arbitrary"),
        name="swiglu_ffn",
    )(x, gpre, w_fi, w_fi, w_fo, gpost)


def _rope_tables(pos, dk):
    half = dk // 2
    inv = ROPE_BASE ** (-jnp.arange(half, dtype=F32) / half)
    ang = pos[:, None] * inv[None, :]
    cos = jnp.cos(ang)
    sin = jnp.sin(ang)
    return jnp.concatenate([cos, cos], axis=-1), jnp.concatenate([-sin, sin], axis=-1)


def _layer(x, rope, shift0, s0, r0, w):
    b, t, d = x.shape
    x2 = x.reshape(b * t, d)
    row = lambda a: a.reshape(1, -1)
    za = _norm_matmul(x2, row(w["nmp"]), w["w_a"], F32, tm_target=512, tn_target=1664)
    zb = _norm_matmul(x2, row(w["nmp"]), w["w_b"], BF16)
    zg = _norm_matmul(x2, row(w["nmp"]), w["w_g"], BF16)
    ya, shift, s_new = _wkv_branch(za.reshape(b, t, -1), shift0, s0, w["mu"], w["w0"], w["w_dup"],
                                   w["a0"], w["w_aup"], w["w_gup"], w["k_k"], w["k_a"], w["r_k"],
                                   w["lnx_g"], w["lnx_b"])
    yb, r_new = _ret_branch(zb.reshape(b, t, -1), rope[0], rope[1], r0, w["ret_g"])
    x1 = _merge(ya.reshape(b * t, -1), yb.reshape(b * t, -1), zg, x2, w["w_ba"], w["w_bb"],
                w["w_out"], row(w["nmq"]))
    xo = _ffn(x1, row(w["nfp"]), w["w_fi"], w["w_fo"], row(w["nfq"]))
    return xo.reshape(b, t, d), shift.reshape(b, -1), s_new, r_new


def kernel(x_prompt, x_sample, state_wkv, state_ret, state_shift, norm_mix_pre, norm_mix_post, norm_ffn_pre, norm_ffn_post, w_in, shift_mu, decay_base, w_decay_up, iclr_base, w_iclr_up, w_gate_up, key_k, key_a, bonus_rk, lnx_gain, lnx_bias, w_branch_a, ret_norm_gain, w_branch_b, w_out, w_ffn_in, w_ffn_out):
    depth = w_in.shape[0]
    bp, tp, _ = x_prompt.shape
    bs, ts, _ = x_sample.shape
    asw = shift_mu.shape[1]
    heads_a, hd_a = bonus_rk.shape[1], bonus_rk.shape[2]
    heads_b, dk, dv = state_ret.shape[2], state_ret.shape[3], state_ret.shape[4]
    b_in = 2 * heads_b * dk + 2 * heads_b * dv
    dt = x_prompt.dtype

    rope_p = _rope_tables(jnp.arange(tp, dtype=F32), dk)
    rope_s = _rope_tables(jnp.arange(ts, dtype=F32) + jnp.float32(PAST_LEN), dk)
    zero_shift = jnp.zeros((bp, asw), dt)
    zero_wkv = jnp.zeros((bp, heads_a, hd_a, hd_a), dt)
    zero_ret = jnp.zeros((bp, heads_b, dk, dv), dt)

    xp, xs = x_prompt, x_sample
    outs = [[] for _ in range(6)]
    for l in range(depth):
        w = dict(
            nmp=norm_mix_pre[l], nmq=norm_mix_post[l], nfp=norm_ffn_pre[l], nfq=norm_ffn_post[l],
            w_a=w_in[l, :, :asw].astype(BF16),
            w_b=w_in[l, :, asw:asw + b_in].astype(BF16),
            w_g=w_in[l, :, asw + b_in:].astype(BF16),
            mu=shift_mu[l], w0=decay_base[l], w_dup=w_decay_up[l].astype(BF16), a0=iclr_base[l],
            w_aup=w_iclr_up[l].astype(BF16), w_gup=w_gate_up[l].astype(BF16), k_k=key_k[l],
            k_a=key_a[l], r_k=bonus_rk[l], lnx_g=lnx_gain[l], lnx_b=lnx_bias[l],
            w_ba=w_branch_a[l].astype(BF16), ret_g=ret_norm_gain[l],
            w_bb=w_branch_b[l].astype(BF16), w_out=w_out[l].astype(BF16),
            w_fi=w_ffn_in[l].astype(BF16), w_fo=w_ffn_out[l].astype(BF16),
        )
        xp, sp, wkp, rtp = _layer(xp, rope_p, zero_shift, zero_wkv, zero_ret, w)
        xs, ss, wks, rts = _layer(xs, rope_s, state_shift[l], state_wkv[l], state_ret[l], w)
        for lst, val in zip(outs, (wkp, rtp, sp, wks, rts, ss)):
            lst.append(val)
    return (xp, xs) + tuple(jnp.stack(o) for o in outs)
```

```python
import functools
import math

import jax
import jax.numpy as jnp
from jax import lax
from jax.experimental import pallas as pl
from jax.experimental.pallas import tpu as pltpu

F32 = jnp.float32
BF16 = jnp.bfloat16

PAST_LEN = 4096
RMS_EPS = 1e-6
LNX_EPS = 64e-5
GN_EPS = 1e-5
ROPE_BASE = 10000.0

V7X_LANES = 128
V7X_MXU_DIM = 256
WKV_CHUNK = 64
VMEM_LIMIT_BYTES = 56 * 1024 * 1024


def _dot(a, b):
    return jnp.dot(a, b, preferred_element_type=F32)


def _dot_nt(a, b):
    return lax.dot_general(a, b, (((1,), (1,)), ((), ())), preferred_element_type=F32)


def _dot_tn(a, b):
    return lax.dot_general(a, b, (((0,), (0,)), ((), ())), preferred_element_type=F32)


def _rms(x, g):
    return x * lax.rsqrt(jnp.mean(x * x, axis=-1, keepdims=True) + RMS_EPS) * g


def _tile(n, target, mult=V7X_LANES):
    if n <= target:
        return n
    best = None
    for t in range(mult, target + 1, mult):
        if n % t == 0:
            best = t
    assert best is not None, (n, target, mult)
    return best


def _params(*sem):
    return pltpu.CompilerParams(dimension_semantics=sem, vmem_limit_bytes=VMEM_LIMIT_BYTES)


def _norm_matmul_kernel(x_ref, g_ref, w_ref, o_ref, *, tn):
    h = _rms(x_ref[...], g_ref[...]).astype(BF16)
    for j in range(w_ref.shape[1] // tn):
        cols = slice(j * tn, (j + 1) * tn)
        o_ref[:, cols] = _dot(h, w_ref[:, cols]).astype(o_ref.dtype)


def _norm_matmul(x, g, w, out_dtype, tm_target=512, tn_target=512):
    m, d = x.shape
    n = w.shape[1]
    tm = _tile(m, tm_target, 8)
    tn = _tile(n, tn_target)
    return pl.pallas_call(
        functools.partial(_norm_matmul_kernel, tn=tn),
        grid=(m // tm,),
        in_specs=[
            pl.BlockSpec((tm, d), lambda i: (i, 0)),
            pl.BlockSpec((1, d), lambda i: (0, 0)),
            pl.BlockSpec((d, n), lambda i: (0, 0), pipeline_mode=pl.Buffered(1)),
        ],
        out_specs=pl.BlockSpec((tm, n), lambda i: (i, 0)),
        out_shape=jax.ShapeDtypeStruct((m, n), out_dtype),
        compiler_params=_params("arbitrary"),
        name="norm_matmul",
    )(x, g, w)


def _split_bf16(x):
    hi = x.astype(BF16)
    lo = (x - hi.astype(F32)).astype(BF16)
    return hi, lo


def _wkv_kernel(za_ref, sh0_ref, s0_ref, mu_ref, w0_ref, wdup_ref, a0_ref, waup_ref, wgup_ref,
                kk_ref, ka_ref, rk_ref, lg_ref, lb_ref,
                ya_ref, sho_ref, so_ref,
                carry_ref, s_ref, ar_ref, bk_ref, bkh_ref, v_ref, y_ref, pc_ref, t_ref, av_ref, wcat_ref,
                *, heads, head_dim, ranks):
    tt = za_ref.shape[1]
    aw = heads * head_dim
    gw = V7X_MXU_DIM
    hpg = gw // head_dim
    groups = aw // gw
    c_len = min(WKV_CHUNK, tt)
    n_chunks = tt // c_len
    dr, ir, gr = ranks
    t = pl.program_id(1)

    @pl.when(t == 0)
    def _init():
        carry_ref[...] = sh0_ref[0]
        s_ref[...] = jnp.zeros_like(s_ref)
        for h in range(heads):
            g, hh = divmod(h, hpg)
            lo = hh * head_dim
            s_ref[g, lo:lo + head_dim, lo:lo + head_dim] = s0_ref[0, h]

    za = za_ref[0]
    rolled = pltpu.roll(za, 1, 0)
    row8 = lax.broadcasted_iota(jnp.int32, (8, za.shape[1]), 0)
    prev = jnp.concatenate([jnp.where(row8 == 0, carry_ref[...], rolled[0:8]), rolled[8:]], axis=0)
    zs = za + mu_ref[...] * (prev - za)
    last = za[tt - 1:tt, :]
    carry_ref[...] = last
    sho_ref[0] = last

    r = zs[:, 0:aw]
    k = zs[:, aw:2 * aw]
    v = zs[:, 2 * aw:3 * aw]
    o = 3 * aw
    wd = zs[:, o:o + dr]
    ad = zs[:, o + dr:o + dr + ir]
    gd = zs[:, o + dr + ir:o + dr + ir + gr]

    xw = w0_ref[...] + _dot(jnp.tanh(wd).astype(BF16), wdup_ref[...])
    ld = (-math.exp(-0.5)) * jax.nn.sigmoid(xw)
    a_ic = jax.nn.sigmoid(a0_ref[...] + _dot(ad.astype(BF16), waup_ref[...]))
    gate = _dot(jax.nn.sigmoid(gd).astype(BF16), wgup_ref[...])

    gi = lax.broadcasted_iota(jnp.int32, (gw, gw), 0) // head_dim
    gj = lax.broadcasted_iota(jnp.int32, (gw, gw), 1) // head_dim
    same_head = gi == gj
    ones_bd = same_head.astype(BF16)

    def head_sum(x):
        xb = x.astype(BF16)
        parts = [_dot(xb[:, g * gw:(g + 1) * gw], ones_bd) for g in range(groups)]
        return jnp.concatenate(parts, axis=1) if groups > 1 else parts[0]

    kk = k * kk_ref[...]
    k2 = k * (1.0 + (a_ic - 1.0) * ka_ref[...])
    kkn = kk * jnp.minimum(lax.rsqrt(head_sum(kk * kk)), 1e12)
    a_vec = -kkn
    b_vec = kkn * a_ic

    ti = lax.broadcasted_iota(jnp.int32, (tt, tt), 0)
    tj = lax.broadcasted_iota(jnp.int32, (tt, tt), 1)
    tri = (((ti // c_len) == (tj // c_len)) & (tj <= ti)).astype(BF16)
    ld_hi, ld_lo = _split_bf16(ld)
    cum = _dot(tri, ld_hi) + _dot(tri, ld_lo)

    p_inc = jnp.exp(cum)
    p_inv = jnp.exp(-cum)
    a_t = a_vec * jnp.exp(cum - ld)
    r_t = r * p_inc
    b_t = b_vec * p_inv
    k_t = k2 * p_inv
    for c in range(n_chunks):
        rs = slice(c * c_len, (c + 1) * c_len)
        p_end = p_inc[(c + 1) * c_len - 1:(c + 1) * c_len, :]
        ar_ref[c, 0:c_len, :] = a_t[rs].astype(BF16)
        ar_ref[c, c_len:2 * c_len, :] = r_t[rs].astype(BF16)
        bk_ref[c, 0:c_len, :] = b_t[rs].astype(BF16)
        bk_ref[c, c_len:2 * c_len, :] = k_t[rs].astype(BF16)
        bkh_ref[c, 0:c_len, :] = (b_t[rs] * p_end).astype(BF16)
        bkh_ref[c, c_len:2 * c_len, :] = (k_t[rs] * p_end).astype(BF16)
        pc_ref[c] = p_end
    v_ref[...] = v.astype(BF16)

    lane_head = lax.broadcasted_iota(jnp.int32, (1, gw), 1) // head_dim
    ci = lax.broadcasted_iota(jnp.int32, (c_len, gw), 0)
    cj = lax.broadcasted_iota(jnp.int32, (c_len, gw), 1) % head_dim
    assert c_len == head_dim
    strict = cj < ci
    incl = cj <= ci
    eye = (cj == ci).astype(F32)
    n_doubling = int(math.log2(c_len))

    def bd(m):
        zero = jnp.zeros_like(m)
        return jnp.concatenate([jnp.where(lane_head == h, m, zero) for h in range(hpg)], axis=0)

    gs = range(groups)
    sls = [slice(g * gw, (g + 1) * gw) for g in gs]

    def prep_body(c, carry):
        r0 = pl.multiple_of(c * c_len, c_len)
        ar = [ar_ref[c, :, sl] for sl in sls]
        bk = [bk_ref[c, :, sl] for sl in sls]
        vv = [v_ref[pl.ds(r0, c_len), sl] for sl in sls]
        s4 = [_dot_nt(ar[g], jnp.concatenate([bd(bk[g][0:c_len]), bd(bk[g][c_len:2 * c_len])],
                                              axis=0)) for g in gs]
        a_ab = [jnp.where(strict, s4[g][0:c_len, 0:gw], 0.0) for g in gs]
        a_ak = [jnp.where(strict, s4[g][0:c_len, gw:2 * gw], 0.0).astype(BF16) for g in gs]
        for g in gs:
            wcat_ref[c, g, :, 0:gw] = jnp.where(incl, s4[g][c_len:2 * c_len, 0:gw], 0.0).astype(BF16)
            wcat_ref[c, g, :, gw:2 * gw] = jnp.where(incl, s4[g][c_len:2 * c_len, gw:2 * gw],
                                                     0.0).astype(BF16)
        av = [_dot(a_ak[g], bd(vv[g])) for g in gs]
        for g in gs:
            av_ref[c, g] = av[g]
        tk = [eye + a_ab[g] for g in gs]
        mb = [a_ab[g].astype(BF16) for g in gs]
        mk = [_dot(mb[g], bd(mb[g])) for g in gs]
        for step in range(1, n_doubling - 1):
            mb = [mk[g].astype(BF16) for g in gs]
            rr = [_dot(mb[g], jnp.concatenate([bd(mb[g]), bd(tk[g].astype(BF16))], axis=1))
                  for g in gs]
            mk = [rr[g][:, 0:gw] for g in gs]
            tk = [tk[g] + rr[g][:, gw:2 * gw] for g in gs]
        tk = [tk[g] + _dot(mk[g].astype(BF16), bd(tk[g].astype(BF16))) for g in gs]
        for g in gs:
            t_ref[c, g] = tk[g].astype(BF16)
        return carry

    lax.fori_loop(0, n_chunks, prep_body, 0)

    def chunk_body(c, carry):
        r0 = pl.multiple_of(c * c_len, c_len)
        ar = [ar_ref[c, :, sl] for sl in sls]
        vv = [v_ref[pl.ds(r0, c_len), sl] for sl in sls]
        s0 = [s_ref[g] for g in gs]
        m1 = [_dot_nt(ar[g], s0[g].astype(BF16)) for g in gs]
        xb = [(m1[g][0:c_len] + av_ref[c, g]).astype(BF16) for g in gs]
        ub = [_dot(t_ref[c, g], bd(xb[g])).astype(BF16) for g in gs]
        ds = [_dot_tn(jnp.concatenate([ub[g], vv[g]], axis=0), bkh_ref[c, :, sls[g]]) for g in gs]
        for g in gs:
            s_ref[g] = s0[g] * pc_ref[c][:, sls[g]] + jnp.where(same_head, ds[g], 0.0)
        yy = [_dot(wcat_ref[c, g], jnp.concatenate([bd(ub[g]), bd(vv[g])], axis=0)) for g in gs]
        for g in gs:
            y_ref[pl.ds(r0, c_len), sls[g]] = m1[g][c_len:2 * c_len] + yy[g]
        return carry

    lax.fori_loop(0, n_chunks, chunk_body, 0)

    y = y_ref[...]
    inv_n = 1.0 / head_dim
    mean = head_sum(y) * inv_n
    d = y - mean
    var = head_sum(d * d) * inv_n
    yn = d * lax.rsqrt(var + LNX_EPS) * lg_ref[...] + lb_ref[...]
    bonus = head_sum(r * k2 * rk_ref[...]) * v
    ya_ref[0] = ((yn + bonus) * gate).astype(ya_ref.dtype)

    @pl.when(t == pl.num_programs(1) - 1)
    def _fin():
        for h in range(heads):
            g, hh = divmod(h, hpg)
            lo = hh * head_dim
            so_ref[0, h] = s_ref[g, lo:lo + head_dim, lo:lo + head_dim]


def _wkv_branch(za, shift0, s0, mu, w0, wdup, a0, waup, wgup, k_k, k_a, r_k, lnx_g, lnx_b):
    b, t, asw = za.shape
    heads, head_dim = s0.shape[1], s0.shape[2]
    aw = heads * head_dim
    ranks = (wdup.shape[0], waup.shape[0], wgup.shape[0])
    tt = _tile(t, 256, WKV_CHUNK)
    c_len = min(WKV_CHUNK, tt)
    n_chunks = tt // c_len
    groups = aw // V7X_MXU_DIM
    row = lambda a: a.reshape(1, -1)
    const = lambda shape: pl.BlockSpec(shape, lambda i, j: (0,) * len(shape))
    kern = functools.partial(_wkv_kernel, heads=heads, head_dim=head_dim, ranks=ranks)
    return pl.pallas_call(
        kern,
        grid=(b, t // tt),
        in_specs=[
            pl.BlockSpec((1, tt, asw), lambda i, j: (i, j, 0)),
            pl.BlockSpec((1, 1, asw), lambda i, j: (i, 0, 0)),
            pl.BlockSpec((1, heads, head_dim, head_dim), lambda i, j: (i, 0, 0, 0)),
            const((1, asw)), const((1, aw)), const(wdup.shape), const((1, aw)), const(waup.shape),
            const(wgup.shape), const((1, aw)), const((1, aw)), const((1, aw)), const((1, aw)),
            const((1, aw)),
        ],
        out_specs=[
            pl.BlockSpec((1, tt, aw), lambda i, j: (i, j, 0)),
            pl.BlockSpec((1, 1, asw), lambda i, j: (i, 0, 0)),
            pl.BlockSpec((1, heads, head_dim, head_dim), lambda i, j: (i, 0, 0, 0)),
        ],
        out_shape=[
            jax.ShapeDtypeStruct((b, t, aw), BF16),
            jax.ShapeDtypeStruct((b, 1, asw), F32),
            jax.ShapeDtypeStruct((b, heads, head_dim, head_dim), F32),
        ],
        scratch_shapes=[
            pltpu.VMEM((1, asw), F32),
            pltpu.VMEM((groups, V7X_MXU_DIM, V7X_MXU_DIM), F32),
            pltpu.VMEM((n_chunks, 2 * c_len, aw), BF16),
            pltpu.VMEM((n_chunks, 2 * c_len, aw), BF16),
            pltpu.VMEM((n_chunks, 2 * c_len, aw), BF16),
            pltpu.VMEM((tt, aw), BF16),
            pltpu.VMEM((tt, aw), F32),
            pltpu.VMEM((n_chunks, 1, aw), F32),
            pltpu.VMEM((n_chunks, groups, c_len, V7X_MXU_DIM), BF16),
            pltpu.VMEM((n_chunks, groups, c_len, V7X_MXU_DIM), F32),
            pltpu.VMEM((n_chunks, groups, c_len, 2 * V7X_MXU_DIM), BF16),
        ],
        compiler_params=_params("arbitrary", "arbitrary"),
        name="wkv7_chunked",
    )(za, shift0.reshape(b, 1, asw), s0, row(mu), row(w0), wdup, row(a0), waup, wgup,
      row(k_k), row(k_a), row(r_k), row(lnx_g), row(lnx_b))


def _ret_kernel(zb_ref, cos_ref, sin_ref, r0_ref, gn_ref, yb_ref, ro_ref, st_ref, dm_ref, qd_ref,
                kd_ref, *, heads, dk, dv):
    blk = zb_ref.shape[1]
    qkw = heads * dk
    vw = heads * dv
    t = pl.program_id(1)
    log_gammas = [math.log1p(-(2.0 ** (-5.0 - h))) for h in range(heads)]

    @pl.when((pl.program_id(0) == 0) & (t == 0))
    def _tables():
        diff = (lax.broadcasted_iota(jnp.int32, (blk, blk), 0)
                - lax.broadcasted_iota(jnp.int32, (blk, blk), 1)).astype(F32)
        pos = lax.broadcasted_iota(jnp.int32, (blk, dk), 0).astype(F32)
        for h in range(heads):
            lg = log_gammas[h]
            dm_ref[h] = jnp.where(diff >= 0, jnp.exp(jnp.maximum(diff, 0.0) * lg), 0.0)
            qd_ref[h] = jnp.exp((pos + 1.0) * lg) * (dk ** -0.5)
            kd_ref[h] = jnp.exp((blk - 1.0 - pos) * lg)

    @pl.when(t == 0)
    def _init():
        st_ref[...] = r0_ref[0]

    cosv = cos_ref[...]
    sinv = sin_ref[...]
    for h in range(heads):
        qh = zb_ref[0, :, h * dk:(h + 1) * dk].astype(F32)
        kh = zb_ref[0, :, qkw + h * dk:qkw + (h + 1) * dk].astype(F32)
        vh = zb_ref[0, :, 2 * qkw + h * dv:2 * qkw + (h + 1) * dv]
        gate = zb_ref[0, :, 2 * qkw + vw + h * dv:2 * qkw + vw + (h + 1) * dv].astype(F32)
        qrot = qh * cosv + pltpu.roll(qh, dk // 2, 1) * sinv
        kr = kh * cosv + pltpu.roll(kh, dk // 2, 1) * sinv
        scores = _dot_nt((qrot * (dk ** -0.5)).astype(BF16), kr.astype(BF16)) * dm_ref[h]
        inner = _dot(scores.astype(BF16), vh)
        st = st_ref[h]
        cross = _dot((qrot * qd_ref[h]).astype(BF16), st.astype(BF16))
        ke = kr * kd_ref[h]
        st_ref[h] = math.exp(blk * log_gammas[h]) * st + _dot_tn(ke.astype(BF16), vh)
        o = inner + cross
        mean = jnp.mean(o, axis=-1, keepdims=True)
        d = o - mean
        var = jnp.mean(d * d, axis=-1, keepdims=True)
        on = d * lax.rsqrt(var + GN_EPS) * gn_ref[:, h * dv:(h + 1) * dv]
        yb_ref[0, :, h * dv:(h + 1) * dv] = (jax.nn.silu(gate) * on).astype(yb_ref.dtype)

    @pl.when(t == pl.num_programs(1) - 1)
    def _fin():
        ro_ref[0] = st_ref[...]


def _ret_branch(zb, cos_t, sin_t, r0, gn_g):
    b, t, bin_w = zb.shape
    heads, dk, dv = r0.shape[1], r0.shape[2], r0.shape[3]
    vw = heads * dv
    blk = _tile(t, 256, 64)
    kern = functools.partial(_ret_kernel, heads=heads, dk=dk, dv=dv)
    return pl.pallas_call(
        kern,
        grid=(b, t // blk),
        in_specs=[
            pl.BlockSpec((1, blk, bin_w), lambda i, j: (i, j, 0)),
            pl.BlockSpec((blk, dk), lambda i, j: (j, 0)),
            pl.BlockSpec((blk, dk), lambda i, j: (j, 0)),
            pl.BlockSpec((1, heads, dk, dv), lambda i, j: (i, 0, 0, 0)),
            pl.BlockSpec((1, vw), lambda i, j: (0, 0)),
        ],
        out_specs=[
            pl.BlockSpec((1, blk, vw), lambda i, j: (i, j, 0)),
            pl.BlockSpec((1, heads, dk, dv), lambda i, j: (i, 0, 0, 0)),
        ],
        out_shape=[
            jax.ShapeDtypeStruct((b, t, vw), BF16),
            jax.ShapeDtypeStruct((b, heads, dk, dv), F32),
        ],
        scratch_shapes=[pltpu.VMEM((heads, dk, dv), F32), pltpu.VMEM((heads, blk, blk), F32),
                        pltpu.VMEM((heads, blk, dk), F32), pltpu.VMEM((heads, blk, dk), F32)],
        compiler_params=_params("arbitrary", "arbitrary"),
        name="retention_chunked",
    )(zb, cos_t, sin_t, r0, gn_g.reshape(1, -1))


def _merge_kernel(ya_ref, yb_ref, zg_ref, x_ref, wba_ref, wbb_ref, wo_ref, g_ref, o_ref):
    d = x_ref.shape[1]
    ga = jax.nn.sigmoid(zg_ref[:, 0:d].astype(F32))
    gb = jax.nn.sigmoid(zg_ref[:, d:2 * d].astype(F32))
    merged = ga * _dot(ya_ref[...], wba_ref[...]) + gb * _dot(yb_ref[...], wbb_ref[...])
    mo = _dot(merged.astype(BF16), wo_ref[...])
    o_ref[...] = x_ref[...] + _rms(mo, g_ref[...])


def _merge(ya, yb, zg, x, wba, wbb, wo, g):
    m, d = x.shape
    tm = _tile(m, 256, 8)
    tok = lambda w: pl.BlockSpec((tm, w), lambda i: (i, 0))
    res = lambda a: pl.BlockSpec(a.shape, lambda i: (0, 0), pipeline_mode=pl.Buffered(1))
    return pl.pallas_call(
        _merge_kernel,
        grid=(m // tm,),
        in_specs=[tok(ya.shape[1]), tok(yb.shape[1]), tok(zg.shape[1]), tok(d),
                  res(wba), res(wbb), res(wo), pl.BlockSpec((1, d), lambda i: (0, 0))],
        out_specs=tok(d),
        out_shape=jax.ShapeDtypeStruct((m, d), F32),
        compiler_params=_params("arbitrary"),
        name="merge_out",
    )(ya, yb, zg, x, wba, wbb, wo, g)


def _ffn_kernel(x_ref, gpre_ref, wg_ref, wu_ref, wo_ref, gpost_ref, o_ref, h_ref, acc_ref):
    j = pl.program_id(1)

    @pl.when(j == 0)
    def _():
        h_ref[...] = _rms(x_ref[...], gpre_ref[...]).astype(BF16)
        acc_ref[...] = jnp.zeros_like(acc_ref)

    h = h_ref[...]
    act = jax.nn.silu(_dot(h, wg_ref[...])) * _dot(h, wu_ref[...])
    acc_ref[...] += _dot(act.astype(BF16), wo_ref[...])

    @pl.when(j == pl.num_programs(1) - 1)
    def _():
        o_ref[...] = x_ref[...] + _rms(acc_ref[...], gpost_ref[...])


def _ffn(x, gpre, w_fi, w_fo, gpost):
    m, d = x.shape
    f = w_fo.shape[0]
    tm = _tile(m, 512, 8)
    tf = _tile(f, 512)
    nf = f // tf
    return pl.pallas_call(
        _ffn_kernel,
        grid=(m // tm, nf),
        in_specs=[
            pl.BlockSpec((tm, d), lambda i, j: (i, 0)),
            pl.BlockSpec((1, d), lambda i, j: (0, 0)),
            pl.BlockSpec((d, tf), lambda i, j: (0, j)),
            pl.BlockSpec((d, tf), lambda i, j: (0, j + nf)),
            pl.BlockSpec((tf, d), lambda i, j: (j, 0)),
            pl.BlockSpec((1, d), lambda i, j: (0, 0)),
        ],
        out_specs=pl.BlockSpec((tm, d), lambda i, j: (i, 0)),
        out_shape=jax.ShapeDtypeStruct((m, d), F32),
        scratch_shapes=[pltpu.VMEM((tm, d), BF16), pltpu.VMEM((tm, d), F32)],
        compiler_params=_params("arbitrary", "arbitrary"),
        name="swiglu_ffn",
    )(x, gpre, w_fi, w_fi, w_fo, gpost)


def _rope_tables(pos, dk):
    half = dk // 2
    inv = ROPE_BASE ** (-jnp.arange(half, dtype=F32) / half)
    ang = pos[:, None] * inv[None, :]
    cos = jnp.cos(ang)
    sin = jnp.sin(ang)
    return jnp.concatenate([cos, cos], axis=-1), jnp.concatenate([-sin, sin], axis=-1)


def _layer(x, rope, shift0, s0, r0, w):
    b, t, d = x.shape
    x2 = x.reshape(b * t, d)
    row = lambda a: a.reshape(1, -1)
    za = _norm_matmul(x2, row(w["nmp"]), w["w_a"], F32)
    zb = _norm_matmul(x2, row(w["nmp"]), w["w_b"], BF16)
    zg = _norm_matmul(x2, row(w["nmp"]), w["w_g"], BF16)
    ya, shift, s_new = _wkv_branch(za.reshape(b, t, -1), shift0, s0, w["mu"], w["w0"], w["w_dup"],
                                   w["a0"], w["w_aup"], w["w_gup"], w["k_k"], w["k_a"], w["r_k"],
                                   w["lnx_g"], w["lnx_b"])
    yb, r_new = _ret_branch(zb.reshape(b, t, -1), rope[0], rope[1], r0, w["ret_g"])
    x1 = _merge(ya.reshape(b * t, -1), yb.reshape(b * t, -1), zg, x2, w["w_ba"], w["w_bb"],
                w["w_out"], row(w["nmq"]))
    xo = _ffn(x1, row(w["nfp"]), w["w_fi"], w["w_fo"], row(w["nfq"]))
    return xo.reshape(b, t, d), shift.reshape(b, -1), s_new, r_new


def kernel(x_prompt, x_sample, state_wkv, state_ret, state_shift, norm_mix_pre, norm_mix_post, norm_ffn_pre, norm_ffn_post, w_in, shift_mu, decay_base, w_decay_up, iclr_base, w_iclr_up, w_gate_up, key_k, key_a, bonus_rk, lnx_gain, lnx_bias, w_branch_a, ret_norm_gain, w_branch_b, w_out, w_ffn_in, w_ffn_out):
    depth = w_in.shape[0]
    bp, tp, _ = x_prompt.shape
    bs, ts, _ = x_sample.shape
    asw = shift_mu.shape[1]
    heads_a, hd_a = bonus_rk.shape[1], bonus_rk.shape[2]
    heads_b, dk, dv = state_ret.shape[2], state_ret.shape[3], state_ret.shape[4]
    b_in = 2 * heads_b * dk + 2 * heads_b * dv
    dt = x_prompt.dtype

    rope_p = _rope_tables(jnp.arange(tp, dtype=F32), dk)
    rope_s = _rope_tables(jnp.arange(ts, dtype=F32) + jnp.float32(PAST_LEN), dk)
    zero_shift = jnp.zeros((bp, asw), dt)
    zero_wkv = jnp.zeros((bp, heads_a, hd_a, hd_a), dt)
    zero_ret = jnp.zeros((bp, heads_b, dk, dv), dt)

    xp, xs = x_prompt, x_sample
    outs = [[] for _ in range(6)]
    for l in range(depth):
        w = dict(
            nmp=norm_mix_pre[l], nmq=norm_mix_post[l], nfp=norm_ffn_pre[l], nfq=norm_ffn_post[l],
            w_a=w_in[l, :, :asw].astype(BF16),
            w_b=w_in[l, :, asw:asw + b_in].astype(BF16),
            w_g=w_in[l, :, asw + b_in:].astype(BF16),
            mu=shift_mu[l], w0=decay_base[l], w_dup=w_decay_up[l].astype(BF16), a0=iclr_base[l],
            w_aup=w_iclr_up[l].astype(BF16), w_gup=w_gate_up[l].astype(BF16), k_k=key_k[l],
            k_a=key_a[l], r_k=bonus_rk[l], lnx_g=lnx_gain[l], lnx_b=lnx_bias[l],
            w_ba=w_branch_a[l].astype(BF16), ret_g=ret_norm_gain[l],
            w_bb=w_branch_b[l].astype(BF16), w_out=w_out[l].astype(BF16),
            w_fi=w_ffn_in[l].astype(BF16), w_fo=w_ffn_out[l].astype(BF16),
        )
        xp, sp, wkp, rtp = _layer(xp, rope_p, zero_shift, zero_wkv, zero_ret, w)
        xs, ss, wks, rts = _layer(xs, rope_s, state_shift[l], state_wkv[l], state_ret[l], w)
        for lst, val in zip(outs, (wkp, rtp, sp, wks, rts, ss)):
            lst.append(val)
    return (xp, xs) + tuple(jnp.stack(o) for o in outs)
```

```python
import functools
import math

import jax
import jax.numpy as jnp
from jax import lax
from jax.experimental import pallas as pl
from jax.experimental.pallas import tpu as pltpu

F32 = jnp.float32
BF16 = jnp.bfloat16

PAST_LEN = 4096
RMS_EPS = 1e-6
LNX_EPS = 64e-5
GN_EPS = 1e-5
ROPE_BASE = 10000.0

V7X_LANES = 128
V7X_MXU_DIM = 256
WKV_GROUP_LANES = 128
WKV_CHUNK = 64
VMEM_LIMIT_BYTES = 56 * 1024 * 1024


def _dot(a, b):
    return jnp.dot(a, b, preferred_element_type=F32)


def _dot_nt(a, b):
    return lax.dot_general(a, b, (((1,), (1,)), ((), ())), preferred_element_type=F32)


def _dot_tn(a, b):
    return lax.dot_general(a, b, (((0,), (0,)), ((), ())), preferred_element_type=F32)


def _rms(x, g):
    return x * lax.rsqrt(jnp.mean(x * x, axis=-1, keepdims=True) + RMS_EPS) * g


def _tile(n, target, mult=V7X_LANES):
    if n <= target:
        return n
    best = None
    for t in range(mult, target + 1, mult):
        if n % t == 0:
            best = t
    assert best is not None, (n, target, mult)
    return best


def _params(*sem):
    return pltpu.CompilerParams(dimension_semantics=sem, vmem_limit_bytes=VMEM_LIMIT_BYTES)


def _norm_matmul_kernel(x_ref, g_ref, w_ref, o_ref, *, tn):
    h = _rms(x_ref[...], g_ref[...]).astype(BF16)
    for j in range(w_ref.shape[1] // tn):
        cols = slice(j * tn, (j + 1) * tn)
        o_ref[:, cols] = _dot(h, w_ref[:, cols]).astype(o_ref.dtype)


def _norm_matmul(x, g, w, out_dtype, tm_target=512, tn_target=512):
    m, d = x.shape
    n = w.shape[1]
    tm = _tile(m, tm_target, 8)
    tn = _tile(n, tn_target)
    return pl.pallas_call(
        functools.partial(_norm_matmul_kernel, tn=tn),
        grid=(m // tm,),
        in_specs=[
            pl.BlockSpec((tm, d), lambda i: (i, 0)),
            pl.BlockSpec((1, d), lambda i: (0, 0)),
            pl.BlockSpec((d, n), lambda i: (0, 0), pipeline_mode=pl.Buffered(1)),
        ],
        out_specs=pl.BlockSpec((tm, n), lambda i: (i, 0)),
        out_shape=jax.ShapeDtypeStruct((m, n), out_dtype),
        compiler_params=_params("arbitrary"),
        name="norm_matmul",
    )(x, g, w)


def _split_bf16(x):
    hi = x.astype(BF16)
    lo = (x - hi.astype(F32)).astype(BF16)
    return hi, lo


def _wkv_kernel(za_ref, sh0_ref, s0_ref, mu_ref, w0_ref, wdup_ref, a0_ref, waup_ref, wgup_ref,
                kk_ref, ka_ref, rk_ref, lg_ref, lb_ref,
                ya_ref, sho_ref, so_ref,
                carry_ref, s_ref, ar_ref, bk_ref, bkh_ref, v_ref, y_ref, pc_ref, t_ref, av_ref, wcat_ref,
                *, heads, head_dim, ranks):
    tt = za_ref.shape[1]
    aw = heads * head_dim
    gw = WKV_GROUP_LANES
    hpg = gw // head_dim
    groups = aw // gw
    c_len = min(WKV_CHUNK, tt)
    n_chunks = tt // c_len
    dr, ir, gr = ranks
    t = pl.program_id(1)

    @pl.when(t == 0)
    def _init():
        carry_ref[...] = sh0_ref[0]
        s_ref[...] = jnp.zeros_like(s_ref)
        for h in range(heads):
            g, hh = divmod(h, hpg)
            lo = hh * head_dim
            s_ref[g, lo:lo + head_dim, lo:lo + head_dim] = s0_ref[0, h]

    za = za_ref[0]
    rolled = pltpu.roll(za, 1, 0)
    row8 = lax.broadcasted_iota(jnp.int32, (8, za.shape[1]), 0)
    prev = jnp.concatenate([jnp.where(row8 == 0, carry_ref[...], rolled[0:8]), rolled[8:]], axis=0)
    zs = za + mu_ref[...] * (prev - za)
    last = za[tt - 1:tt, :]
    carry_ref[...] = last
    sho_ref[0] = last

    r = zs[:, 0:aw]
    k = zs[:, aw:2 * aw]
    v = zs[:, 2 * aw:3 * aw]
    o = 3 * aw
    wd = zs[:, o:o + dr]
    ad = zs[:, o + dr:o + dr + ir]
    gd = zs[:, o + dr + ir:o + dr + ir + gr]

    xw = w0_ref[...] + _dot(jnp.tanh(wd).astype(BF16), wdup_ref[...])
    ld = (-math.exp(-0.5)) * jax.nn.sigmoid(xw)
    a_ic = jax.nn.sigmoid(a0_ref[...] + _dot(ad.astype(BF16), waup_ref[...]))
    gate = _dot(jax.nn.sigmoid(gd).astype(BF16), wgup_ref[...])

    gi = lax.broadcasted_iota(jnp.int32, (gw, gw), 0) // head_dim
    gj = lax.broadcasted_iota(jnp.int32, (gw, gw), 1) // head_dim
    same_head = gi == gj
    ones_bd = same_head.astype(BF16)

    def head_sum(x):
        xb = x.astype(BF16)
        parts = [_dot(xb[:, g * gw:(g + 1) * gw], ones_bd) for g in range(groups)]
        return jnp.concatenate(parts, axis=1) if groups > 1 else parts[0]

    kk = k * kk_ref[...]
    k2 = k * (1.0 + (a_ic - 1.0) * ka_ref[...])
    kkn = kk * jnp.minimum(lax.rsqrt(head_sum(kk * kk)), 1e12)
    a_vec = -kkn
    b_vec = kkn * a_ic

    ti = lax.broadcasted_iota(jnp.int32, (tt, tt), 0)
    tj = lax.broadcasted_iota(jnp.int32, (tt, tt), 1)
    tri = (((ti // c_len) == (tj // c_len)) & (tj <= ti)).astype(BF16)
    ld_hi, ld_lo = _split_bf16(ld)
    cum = _dot(tri, ld_hi) + _dot(tri, ld_lo)

    p_inc = jnp.exp(cum)
    p_inv = jnp.exp(-cum)
    a_t = a_vec * jnp.exp(cum - ld)
    r_t = r * p_inc
    b_t = b_vec * p_inv
    k_t = k2 * p_inv
    for c in range(n_chunks):
        rs = slice(c * c_len, (c + 1) * c_len)
        p_end = p_inc[(c + 1) * c_len - 1:(c + 1) * c_len, :]
        ar_ref[c, 0:c_len, :] = a_t[rs].astype(BF16)
        ar_ref[c, c_len:2 * c_len, :] = r_t[rs].astype(BF16)
        bk_ref[c, 0:c_len, :] = b_t[rs].astype(BF16)
        bk_ref[c, c_len:2 * c_len, :] = k_t[rs].astype(BF16)
        bkh_ref[c, 0:c_len, :] = (b_t[rs] * p_end).astype(BF16)
        bkh_ref[c, c_len:2 * c_len, :] = (k_t[rs] * p_end).astype(BF16)
        pc_ref[c] = p_end
    v_ref[...] = v.astype(BF16)

    lane_head = lax.broadcasted_iota(jnp.int32, (1, gw), 1) // head_dim
    ci = lax.broadcasted_iota(jnp.int32, (c_len, gw), 0)
    cj = lax.broadcasted_iota(jnp.int32, (c_len, gw), 1) % head_dim
    assert c_len == head_dim
    strict = cj < ci
    incl = cj <= ci
    eye = (cj == ci).astype(F32)
    n_doubling = int(math.log2(c_len))

    def bd(m):
        zero = jnp.zeros_like(m)
        return jnp.concatenate([jnp.where(lane_head == h, m, zero) for h in range(hpg)], axis=0)

    gs = range(groups)
    sls = [slice(g * gw, (g + 1) * gw) for g in gs]

    def prep_body(c, carry):
        r0 = pl.multiple_of(c * c_len, c_len)
        ar = [ar_ref[c, :, sl] for sl in sls]
        bk = [bk_ref[c, :, sl] for sl in sls]
        vv = [v_ref[pl.ds(r0, c_len), sl] for sl in sls]
        s4 = [_dot_nt(ar[g], jnp.concatenate([bd(bk[g][0:c_len]), bd(bk[g][c_len:2 * c_len])],
                                              axis=0)) for g in gs]
        a_ab = [jnp.where(strict, s4[g][0:c_len, 0:gw], 0.0) for g in gs]
        a_ak = [jnp.where(strict, s4[g][0:c_len, gw:2 * gw], 0.0).astype(BF16) for g in gs]
        for g in gs:
            wcat_ref[c, g, :, 0:gw] = jnp.where(incl, s4[g][c_len:2 * c_len, 0:gw], 0.0).astype(BF16)
            wcat_ref[c, g, :, gw:2 * gw] = jnp.where(incl, s4[g][c_len:2 * c_len, gw:2 * gw],
                                                     0.0).astype(BF16)
        av = [_dot(a_ak[g], bd(vv[g])) for g in gs]
        for g in gs:
            av_ref[c, g] = av[g]
        tk = [eye + a_ab[g] for g in gs]
        mb = [a_ab[g].astype(BF16) for g in gs]
        mk = [_dot(mb[g], bd(mb[g])) for g in gs]
        for step in range(1, n_doubling - 1):
            mb = [mk[g].astype(BF16) for g in gs]
            rr = [_dot(mb[g], jnp.concatenate([bd(mb[g]), bd(tk[g].astype(BF16))], axis=1))
                  for g in gs]
            mk = [rr[g][:, 0:gw] for g in gs]
            tk = [tk[g] + rr[g][:, gw:2 * gw] for g in gs]
        tk = [tk[g] + _dot(mk[g].astype(BF16), bd(tk[g].astype(BF16))) for g in gs]
        for g in gs:
            t_ref[c, g] = tk[g].astype(BF16)
        return carry

    lax.fori_loop(0, n_chunks, prep_body, 0)

    def chunk_body(c, carry):
        r0 = pl.multiple_of(c * c_len, c_len)
        ar = [ar_ref[c, :, sl] for sl in sls]
        vv = [v_ref[pl.ds(r0, c_len), sl] for sl in sls]
        s0 = [s_ref[g] for g in gs]
        m1 = [_dot_nt(ar[g], s0[g].astype(BF16)) for g in gs]
        xb = [(m1[g][0:c_len] + av_ref[c, g]).astype(BF16) for g in gs]
        ub = [_dot(t_ref[c, g], bd(xb[g])).astype(BF16) for g in gs]
        ds = [_dot_tn(jnp.concatenate([ub[g], vv[g]], axis=0), bkh_ref[c, :, sls[g]]) for g in gs]
        for g in gs:
            s_ref[g] = s0[g] * pc_ref[c][:, sls[g]] + jnp.where(same_head, ds[g], 0.0)
        yy = [_dot(wcat_ref[c, g], jnp.concatenate([bd(ub[g]), bd(vv[g])], axis=0)) for g in gs]
        for g in gs:
            y_ref[pl.ds(r0, c_len), sls[g]] = m1[g][c_len:2 * c_len] + yy[g]
        return carry

    lax.fori_loop(0, n_chunks, chunk_body, 0)

    y = y_ref[...]
    inv_n = 1.0 / head_dim
    mean = head_sum(y) * inv_n
    d = y - mean
    var = head_sum(d * d) * inv_n
    yn = d * lax.rsqrt(var + LNX_EPS) * lg_ref[...] + lb_ref[...]
    bonus = head_sum(r * k2 * rk_ref[...]) * v
    ya_ref[0] = ((yn + bonus) * gate).astype(ya_ref.dtype)

    @pl.when(t == pl.num_programs(1) - 1)
    def _fin():
        for h in range(heads):
            g, hh = divmod(h, hpg)
            lo = hh * head_dim
            so_ref[0, h] = s_ref[g, lo:lo + head_dim, lo:lo + head_dim]


def _wkv_branch(za, shift0, s0, mu, w0, wdup, a0, waup, wgup, k_k, k_a, r_k, lnx_g, lnx_b):
    b, t, asw = za.shape
    heads, head_dim = s0.shape[1], s0.shape[2]
    aw = heads * head_dim
    ranks = (wdup.shape[0], waup.shape[0], wgup.shape[0])
    tt = _tile(t, 256, WKV_CHUNK)
    c_len = min(WKV_CHUNK, tt)
    n_chunks = tt // c_len
    groups = aw // WKV_GROUP_LANES
    row = lambda a: a.reshape(1, -1)
    const = lambda shape: pl.BlockSpec(shape, lambda i, j: (0,) * len(shape))
    kern = functools.partial(_wkv_kernel, heads=heads, head_dim=head_dim, ranks=ranks)
    return pl.pallas_call(
        kern,
        grid=(b, t // tt),
        in_specs=[
            pl.BlockSpec((1, tt, asw), lambda i, j: (i, j, 0)),
            pl.BlockSpec((1, 1, asw), lambda i, j: (i, 0, 0)),
            pl.BlockSpec((1, heads, head_dim, head_dim), lambda i, j: (i, 0, 0, 0)),
            const((1, asw)), const((1, aw)), const(wdup.shape), const((1, aw)), const(waup.shape),
            const(wgup.shape), const((1, aw)), const((1, aw)), const((1, aw)), const((1, aw)),
            const((1, aw)),
        ],
        out_specs=[
            pl.BlockSpec((1, tt, aw), lambda i, j: (i, j, 0)),
            pl.BlockSpec((1, 1, asw), lambda i, j: (i, 0, 0)),
            pl.BlockSpec((1, heads, head_dim, head_dim), lambda i, j: (i, 0, 0, 0)),
        ],
        out_shape=[
            jax.ShapeDtypeStruct((b, t, aw), BF16),
            jax.ShapeDtypeStruct((b, 1, asw), F32),
            jax.ShapeDtypeStruct((b, heads, head_dim, head_dim), F32),
        ],
        scratch_shapes=[
            pltpu.VMEM((1, asw), F32),
            pltpu.VMEM((groups, WKV_GROUP_LANES, WKV_GROUP_LANES), F32),
            pltpu.VMEM((n_chunks, 2 * c_len, aw), BF16),
            pltpu.VMEM((n_chunks, 2 * c_len, aw), BF16),
            pltpu.VMEM((n_chunks, 2 * c_len, aw), BF16),
            pltpu.VMEM((tt, aw), BF16),
            pltpu.VMEM((tt, aw), F32),
            pltpu.VMEM((n_chunks, 1, aw), F32),
            pltpu.VMEM((n_chunks, groups, c_len, WKV_GROUP_LANES), BF16),
            pltpu.VMEM((n_chunks, groups, c_len, WKV_GROUP_LANES), F32),
            pltpu.VMEM((n_chunks, groups, c_len, 2 * WKV_GROUP_LANES), BF16),
        ],
        compiler_params=_params("arbitrary", "arbitrary"),
        name="wkv7_chunked",
    )(za, shift0.reshape(b, 1, asw), s0, row(mu), row(w0), wdup, row(a0), waup, wgup,
      row(k_k), row(k_a), row(r_k), row(lnx_g), row(lnx_b))


def _ret_kernel(zb_ref, cos_ref, sin_ref, r0_ref, gn_ref, yb_ref, ro_ref, st_ref, dm_ref, qd_ref,
                kd_ref, *, heads, dk, dv):
    blk = zb_ref.shape[1]
    qkw = heads * dk
    vw = heads * dv
    t = pl.program_id(1)
    log_gammas = [math.log1p(-(2.0 ** (-5.0 - h))) for h in range(heads)]

    @pl.when((pl.program_id(0) == 0) & (t == 0))
    def _tables():
        diff = (lax.broadcasted_iota(jnp.int32, (blk, blk), 0)
                - lax.broadcasted_iota(jnp.int32, (blk, blk), 1)).astype(F32)
        pos = lax.broadcasted_iota(jnp.int32, (blk, dk), 0).astype(F32)
        for h in range(heads):
            lg = log_gammas[h]
            dm_ref[h] = jnp.where(diff >= 0, jnp.exp(jnp.maximum(diff, 0.0) * lg), 0.0)
            qd_ref[h] = jnp.exp((pos + 1.0) * lg) * (dk ** -0.5)
            kd_ref[h] = jnp.exp((blk - 1.0 - pos) * lg)

    @pl.when(t == 0)
    def _init():
        st_ref[...] = r0_ref[0]

    cosv = cos_ref[...]
    sinv = sin_ref[...]
    for h in range(heads):
        qh = zb_ref[0, :, h * dk:(h + 1) * dk].astype(F32)
        kh = zb_ref[0, :, qkw + h * dk:qkw + (h + 1) * dk].astype(F32)
        vh = zb_ref[0, :, 2 * qkw + h * dv:2 * qkw + (h + 1) * dv]
        gate = zb_ref[0, :, 2 * qkw + vw + h * dv:2 * qkw + vw + (h + 1) * dv].astype(F32)
        qrot = qh * cosv + pltpu.roll(qh, dk // 2, 1) * sinv
        kr = kh * cosv + pltpu.roll(kh, dk // 2, 1) * sinv
        scores = _dot_nt((qrot * (dk ** -0.5)).astype(BF16), kr.astype(BF16)) * dm_ref[h]
        inner = _dot(scores.astype(BF16), vh)
        st = st_ref[h]
        cross = _dot((qrot * qd_ref[h]).astype(BF16), st.astype(BF16))
        ke = kr * kd_ref[h]
        st_ref[h] = math.exp(blk * log_gammas[h]) * st + _dot_tn(ke.astype(BF16), vh)
        o = inner + cross
        mean = jnp.mean(o, axis=-1, keepdims=True)
        d = o - mean
        var = jnp.mean(d * d, axis=-1, keepdims=True)
        on = d * lax.rsqrt(var + GN_EPS) * gn_ref[:, h * dv:(h + 1) * dv]
        yb_ref[0, :, h * dv:(h + 1) * dv] = (jax.nn.silu(gate) * on).astype(yb_ref.dtype)

    @pl.when(t == pl.num_programs(1) - 1)
    def _fin():
        ro_ref[0] = st_ref[...]


def _ret_branch(zb, cos_t, sin_t, r0, gn_g):
    b, t, bin_w = zb.shape
    heads, dk, dv = r0.shape[1], r0.shape[2], r0.shape[3]
    vw = heads * dv
    blk = _tile(t, 256, 64)
    kern = functools.partial(_ret_kernel, heads=heads, dk=dk, dv=dv)
    return pl.pallas_call(
        kern,
        grid=(b, t // blk),
        in_specs=[
            pl.BlockSpec((1, blk, bin_w), lambda i, j: (i, j, 0)),
            pl.BlockSpec((blk, dk), lambda i, j: (j, 0)),
            pl.BlockSpec((blk, dk), lambda i, j: (j, 0)),
            pl.BlockSpec((1, heads, dk, dv), lambda i, j: (i, 0, 0, 0)),
            pl.BlockSpec((1, vw), lambda i, j: (0, 0)),
        ],
        out_specs=[
            pl.BlockSpec((1, blk, vw), lambda i, j: (i, j, 0)),
            pl.BlockSpec((1, heads, dk, dv), lambda i, j: (i, 0, 0, 0)),
        ],
        out_shape=[
            jax.ShapeDtypeStruct((b, t, vw), BF16),
            jax.ShapeDtypeStruct((b, heads, dk, dv), F32),
        ],
        scratch_shapes=[pltpu.VMEM((heads, dk, dv), F32), pltpu.VMEM((heads, blk, blk), F32),
                        pltpu.VMEM((heads, blk, dk), F32), pltpu.VMEM((heads, blk, dk), F32)],
        compiler_params=_params("arbitrary", "arbitrary"),
        name="retention_chunked",
    )(zb, cos_t, sin_t, r0, gn_g.reshape(1, -1))


def _merge_kernel(ya_ref, yb_ref, zg_ref, x_ref, wba_ref, wbb_ref, wo_ref, g_ref, o_ref):
    d = x_ref.shape[1]
    ga = jax.nn.sigmoid(zg_ref[:, 0:d].astype(F32))
    gb = jax.nn.sigmoid(zg_ref[:, d:2 * d].astype(F32))
    merged = ga * _dot(ya_ref[...], wba_ref[...]) + gb * _dot(yb_ref[...], wbb_ref[...])
    mo = _dot(merged.astype(BF16), wo_ref[...])
    o_ref[...] = x_ref[...] + _rms(mo, g_ref[...])


def _merge(ya, yb, zg, x, wba, wbb, wo, g):
    m, d = x.shape
    tm = _tile(m, 256, 8)
    tok = lambda w: pl.BlockSpec((tm, w), lambda i: (i, 0))
    res = lambda a: pl.BlockSpec(a.shape, lambda i: (0, 0), pipeline_mode=pl.Buffered(1))
    return pl.pallas_call(
        _merge_kernel,
        grid=(m // tm,),
        in_specs=[tok(ya.shape[1]), tok(yb.shape[1]), tok(zg.shape[1]), tok(d),
                  res(wba), res(wbb), res(wo), pl.BlockSpec((1, d), lambda i: (0, 0))],
        out_specs=tok(d),
        out_shape=jax.ShapeDtypeStruct((m, d), F32),
        compiler_params=_params("arbitrary"),
        name="merge_out",
    )(ya, yb, zg, x, wba, wbb, wo, g)


def _ffn_kernel(x_ref, gpre_ref, wg_ref, wu_ref, wo_ref, gpost_ref, o_ref, h_ref, acc_ref):
    j = pl.program_id(1)

    @pl.when(j == 0)
    def _():
        h_ref[...] = _rms(x_ref[...], gpre_ref[...]).astype(BF16)
        acc_ref[...] = jnp.zeros_like(acc_ref)

    h = h_ref[...]
    act = jax.nn.silu(_dot(h, wg_ref[...])) * _dot(h, wu_ref[...])
    acc_ref[...] += _dot(act.astype(BF16), wo_ref[...])

    @pl.when(j == pl.num_programs(1) - 1)
    def _():
        o_ref[...] = x_ref[...] + _rms(acc_ref[...], gpost_ref[...])


def _ffn(x, gpre, w_fi, w_fo, gpost):
    m, d = x.shape
    f = w_fo.shape[0]
    tm = _tile(m, 512, 8)
    tf = _tile(f, 512)
    nf = f // tf
    return pl.pallas_call(
        _ffn_kernel,
        grid=(m // tm, nf),
        in_specs=[
            pl.BlockSpec((tm, d), lambda i, j: (i, 0)),
            pl.BlockSpec((1, d), lambda i, j: (0, 0)),
            pl.BlockSpec((d, tf), lambda i, j: (0, j)),
            pl.BlockSpec((d, tf), lambda i, j: (0, j + nf)),
            pl.BlockSpec((tf, d), lambda i, j: (j, 0)),
            pl.BlockSpec((1, d), lambda i, j: (0, 0)),
        ],
        out_specs=pl.BlockSpec((tm, d), lambda i, j: (i, 0)),
        out_shape=jax.ShapeDtypeStruct((m, d), F32),
        scratch_shapes=[pltpu.VMEM((tm, d), BF16), pltpu.VMEM((tm, d), F32)],
        compiler_params=_params("arbitrary", "arbitrary"),
        name="swiglu_ffn",
    )(x, gpre, w_fi, w_fi, w_fo, gpost)


def _rope_tables(pos, dk):
    half = dk // 2
    inv = ROPE_BASE ** (-jnp.arange(half, dtype=F32) / half)
    ang = pos[:, None] * inv[None, :]
    cos = jnp.cos(ang)
    sin = jnp.sin(ang)
    return jnp.concatenate([cos, cos], axis=-1), jnp.concatenate([-sin, sin], axis=-1)


def _layer(x, rope, shift0, s0, r0, w):
    b, t, d = x.shape
    x2 = x.reshape(b * t, d)
    row = lambda a: a.reshape(1, -1)
    za = _norm_matmul(x2, row(w["nmp"]), w["w_a"], F32)
    zb = _norm_matmul(x2, row(w["nmp"]), w["w_b"], BF16)
    zg = _norm_matmul(x2, row(w["nmp"]), w["w_g"], BF16)
    ya, shift, s_new = _wkv_branch(za.reshape(b, t, -1), shift0, s0, w["mu"], w["w0"], w["w_dup"],
                                   w["a0"], w["w_aup"], w["w_gup"], w["k_k"], w["k_a"], w["r_k"],
                                   w["lnx_g"], w["lnx_b"])
    yb, r_new = _ret_branch(zb.reshape(b, t, -1), rope[0], rope[1], r0, w["ret_g"])
    x1 = _merge(ya.reshape(b * t, -1), yb.reshape(b * t, -1), zg, x2, w["w_ba"], w["w_bb"],
                w["w_out"], row(w["nmq"]))
    xo = _ffn(x1, row(w["nfp"]), w["w_fi"], w["w_fo"], row(w["nfq"]))
    return xo.reshape(b, t, d), shift.reshape(b, -1), s_new, r_new


def kernel(x_prompt, x_sample, state_wkv, state_ret, state_shift, norm_mix_pre, norm_mix_post, norm_ffn_pre, norm_ffn_post, w_in, shift_mu, decay_base, w_decay_up, iclr_base, w_iclr_up, w_gate_up, key_k, key_a, bonus_rk, lnx_gain, lnx_bias, w_branch_a, ret_norm_gain, w_branch_b, w_out, w_ffn_in, w_ffn_out):
    depth = w_in.shape[0]
    bp, tp, _ = x_prompt.shape
    bs, ts, _ = x_sample.shape
    asw = shift_mu.shape[1]
    heads_a, hd_a = bonus_rk.shape[1], bonus_rk.shape[2]
    heads_b, dk, dv = state_ret.shape[2], state_ret.shape[3], state_ret.shape[4]
    b_in = 2 * heads_b * dk + 2 * heads_b * dv
    dt = x_prompt.dtype

    rope_p = _rope_tables(jnp.arange(tp, dtype=F32), dk)
    rope_s = _rope_tables(jnp.arange(ts, dtype=F32) + jnp.float32(PAST_LEN), dk)
    zero_shift = jnp.zeros((bp, asw), dt)
    zero_wkv = jnp.zeros((bp, heads_a, hd_a, hd_a), dt)
    zero_ret = jnp.zeros((bp, heads_b, dk, dv), dt)

    xp, xs = x_prompt, x_sample
    outs = [[] for _ in range(6)]
    for l in range(depth):
        w = dict(
            nmp=norm_mix_pre[l], nmq=norm_mix_post[l], nfp=norm_ffn_pre[l], nfq=norm_ffn_post[l],
            w_a=w_in[l, :, :asw].astype(BF16),
            w_b=w_in[l, :, asw:asw + b_in].astype(BF16),
            w_g=w_in[l, :, asw + b_in:].astype(BF16),
            mu=shift_mu[l], w0=decay_base[l], w_dup=w_decay_up[l].astype(BF16), a0=iclr_base[l],
            w_aup=w_iclr_up[l].astype(BF16), w_gup=w_gate_up[l].astype(BF16), k_k=key_k[l],
            k_a=key_a[l], r_k=bonus_rk[l], lnx_g=lnx_gain[l], lnx_b=lnx_bias[l],
            w_ba=w_branch_a[l].astype(BF16), ret_g=ret_norm_gain[l],
            w_bb=w_branch_b[l].astype(BF16), w_out=w_out[l].astype(BF16),
            w_fi=w_ffn_in[l].astype(BF16), w_fo=w_ffn_out[l].astype(BF16),
        )
        xp, sp, wkp, rtp = _layer(xp, rope_p, zero_shift, zero_wkv, zero_ret, w)
        xs, ss, wks, rts = _layer(xs, rope_s, state_shift[l], state_wkv[l], state_ret[l], w)
        for lst, val in zip(outs, (wkp, rtp, sp, wks, rts, ss)):
            lst.append(val)
    return (xp, xs) + tuple(jnp.stack(o) for o in outs)
```

```python
import functools
import math

import jax
import jax.numpy as jnp
from jax import lax
from jax.experimental import pallas as pl
from jax.experimental.pallas import tpu as pltpu

F32 = jnp.float32
BF16 = jnp.bfloat16

PAST_LEN = 4096
RMS_EPS = 1e-6
LNX_EPS = 64e-5
GN_EPS = 1e-5
ROPE_BASE = 10000.0

V7X_LANES = 128
V7X_MXU_DIM = 256
WKV_GROUP_LANES = 128
WKV_CHUNK = 64
V7X_VMEM_BYTES = 64 * 1024 * 1024
VMEM_LIMIT_BYTES = V7X_VMEM_BYTES * 7 // 8
VMEM_LIMIT_RESIDENT_BYTES = V7X_VMEM_BYTES * 15 // 16


def _dot(a, b):
    return jnp.dot(a, b, preferred_element_type=F32)


def _dot_nt(a, b):
    return lax.dot_general(a, b, (((1,), (1,)), ((), ())), preferred_element_type=F32)


def _dot_tn(a, b):
    return lax.dot_general(a, b, (((0,), (0,)), ((), ())), preferred_element_type=F32)


def _rms(x, g):
    return x * lax.rsqrt(jnp.mean(x * x, axis=-1, keepdims=True) + RMS_EPS) * g


def _tile(n, target, mult=V7X_LANES):
    if n <= target:
        return n
    best = None
    for t in range(mult, target + 1, mult):
        if n % t == 0:
            best = t
    assert best is not None, (n, target, mult)
    return best


def _params(*sem, vmem_limit=VMEM_LIMIT_BYTES):
    return pltpu.CompilerParams(dimension_semantics=sem, vmem_limit_bytes=vmem_limit)


def _norm_matmul_kernel(x_ref, g_ref, w_ref, o_ref, *, tn):
    h = _rms(x_ref[...], g_ref[...]).astype(BF16)
    for j in range(w_ref.shape[1] // tn):
        cols = slice(j * tn, (j + 1) * tn)
        o_ref[:, cols] = _dot(h, w_ref[:, cols]).astype(o_ref.dtype)


def _norm_matmul(x, g, w, out_dtype, tm_target=512, tn_target=512):
    m, d = x.shape
    n = w.shape[1]
    tm = _tile(m, tm_target, 8)
    tn = _tile(n, tn_target)
    return pl.pallas_call(
        functools.partial(_norm_matmul_kernel, tn=tn),
        grid=(m // tm,),
        in_specs=[
            pl.BlockSpec((tm, d), lambda i: (i, 0)),
            pl.BlockSpec((1, d), lambda i: (0, 0)),
            pl.BlockSpec((d, n), lambda i: (0, 0), pipeline_mode=pl.Buffered(1)),
        ],
        out_specs=pl.BlockSpec((tm, n), lambda i: (i, 0)),
        out_shape=jax.ShapeDtypeStruct((m, n), out_dtype),
        compiler_params=_params("arbitrary"),
        name="norm_matmul",
    )(x, g, w)


def _split_bf16(x):
    hi = x.astype(BF16)
    lo = (x - hi.astype(F32)).astype(BF16)
    return hi, lo


def _wkv_kernel(za_ref, sh0_ref, s0_ref, mu_ref, w0_ref, wdup_ref, a0_ref, waup_ref, wgup_ref,
                kk_ref, ka_ref, rk_ref, lg_ref, lb_ref,
                ya_ref, sho_ref, so_ref,
                carry_ref, s_ref, ar_ref, bk_ref, bkh_ref, v_ref, y_ref, pc_ref, t_ref, av_ref, wcat_ref,
                *, heads, head_dim, ranks):
    tt = za_ref.shape[1]
    aw = heads * head_dim
    gw = WKV_GROUP_LANES
    hpg = gw // head_dim
    groups = aw // gw
    c_len = min(WKV_CHUNK, tt)
    n_chunks = tt // c_len
    dr, ir, gr = ranks
    t = pl.program_id(1)

    @pl.when(t == 0)
    def _init():
        carry_ref[...] = sh0_ref[0]
        s_ref[...] = jnp.zeros_like(s_ref)
        for h in range(heads):
            g, hh = divmod(h, hpg)
            lo = hh * head_dim
            s_ref[g, lo:lo + head_dim, lo:lo + head_dim] = s0_ref[0, h]

    za = za_ref[0]
    rolled = pltpu.roll(za, 1, 0)
    row8 = lax.broadcasted_iota(jnp.int32, (8, za.shape[1]), 0)
    prev = jnp.concatenate([jnp.where(row8 == 0, carry_ref[...], rolled[0:8]), rolled[8:]], axis=0)
    zs = za + mu_ref[...] * (prev - za)
    last = za[tt - 1:tt, :]
    carry_ref[...] = last
    sho_ref[0] = last

    r = zs[:, 0:aw]
    k = zs[:, aw:2 * aw]
    v = zs[:, 2 * aw:3 * aw]
    o = 3 * aw
    wd = zs[:, o:o + dr]
    ad = zs[:, o + dr:o + dr + ir]
    gd = zs[:, o + dr + ir:o + dr + ir + gr]

    xw = w0_ref[...] + _dot(jnp.tanh(wd).astype(BF16), wdup_ref[...])
    ld = (-math.exp(-0.5)) * jax.nn.sigmoid(xw)
    a_ic = jax.nn.sigmoid(a0_ref[...] + _dot(ad.astype(BF16), waup_ref[...]))
    gate = _dot(jax.nn.sigmoid(gd).astype(BF16), wgup_ref[...])

    gi = lax.broadcasted_iota(jnp.int32, (gw, gw), 0) // head_dim
    gj = lax.broadcasted_iota(jnp.int32, (gw, gw), 1) // head_dim
    same_head = gi == gj
    ones_bd = same_head.astype(BF16)

    def head_sum(x):
        xb = x.astype(BF16)
        parts = [_dot(xb[:, g * gw:(g + 1) * gw], ones_bd) for g in range(groups)]
        return jnp.concatenate(parts, axis=1) if groups > 1 else parts[0]

    kk = k * kk_ref[...]
    k2 = k * (1.0 + (a_ic - 1.0) * ka_ref[...])
    kkn = kk * jnp.minimum(lax.rsqrt(head_sum(kk * kk)), 1e12)
    a_vec = -kkn
    b_vec = kkn * a_ic

    ti = lax.broadcasted_iota(jnp.int32, (tt, tt), 0)
    tj = lax.broadcasted_iota(jnp.int32, (tt, tt), 1)
    tri = (((ti // c_len) == (tj // c_len)) & (tj <= ti)).astype(BF16)
    ld_hi, ld_lo = _split_bf16(ld)
    cum = _dot(tri, ld_hi) + _dot(tri, ld_lo)

    p_inc = jnp.exp(cum)
    p_inv = jnp.exp(-cum)
    a_t = a_vec * jnp.exp(cum - ld)
    r_t = r * p_inc
    b_t = b_vec * p_inv
    k_t = k2 * p_inv
    for c in range(n_chunks):
        rs = slice(c * c_len, (c + 1) * c_len)
        p_end = p_inc[(c + 1) * c_len - 1:(c + 1) * c_len, :]
        ar_ref[c, 0:c_len, :] = a_t[rs].astype(BF16)
        ar_ref[c, c_len:2 * c_len, :] = r_t[rs].astype(BF16)
        bk_ref[c, 0:c_len, :] = b_t[rs].astype(BF16)
        bk_ref[c, c_len:2 * c_len, :] = k_t[rs].astype(BF16)
        bkh_ref[c, 0:c_len, :] = (b_t[rs] * p_end).astype(BF16)
        bkh_ref[c, c_len:2 * c_len, :] = (k_t[rs] * p_end).astype(BF16)
        pc_ref[c] = p_end
    v_ref[...] = v.astype(BF16)

    lane_head = lax.broadcasted_iota(jnp.int32, (1, gw), 1) // head_dim
    ci = lax.broadcasted_iota(jnp.int32, (c_len, gw), 0)
    cj = lax.broadcasted_iota(jnp.int32, (c_len, gw), 1) % head_dim
    assert c_len == head_dim
    strict = cj < ci
    incl = cj <= ci
    eye = (cj == ci).astype(F32)
    n_doubling = int(math.log2(c_len))

    def bd(m):
        zero = jnp.zeros_like(m)
        return jnp.concatenate([jnp.where(lane_head == h, m, zero) for h in range(hpg)], axis=0)

    gs = range(groups)
    sls = [slice(g * gw, (g + 1) * gw) for g in gs]

    def prep_body(c, carry):
        r0 = pl.multiple_of(c * c_len, c_len)
        ar = [ar_ref[c, :, sl] for sl in sls]
        bk = [bk_ref[c, :, sl] for sl in sls]
        vv = [v_ref[pl.ds(r0, c_len), sl] for sl in sls]
        s4 = [_dot_nt(ar[g], jnp.concatenate([bd(bk[g][0:c_len]), bd(bk[g][c_len:2 * c_len])],
                                              axis=0)) for g in gs]
        a_ab = [jnp.where(strict, s4[g][0:c_len, 0:gw], 0.0) for g in gs]
        a_ak = [jnp.where(strict, s4[g][0:c_len, gw:2 * gw], 0.0).astype(BF16) for g in gs]
        for g in gs:
            wcat_ref[c, g, :, 0:gw] = jnp.where(incl, s4[g][c_len:2 * c_len, 0:gw], 0.0).astype(BF16)
            wcat_ref[c, g, :, gw:2 * gw] = jnp.where(incl, s4[g][c_len:2 * c_len, gw:2 * gw],
                                                     0.0).astype(BF16)
        av = [_dot(a_ak[g], bd(vv[g])) for g in gs]
        for g in gs:
            av_ref[c, g] = av[g]
        tk = [eye + a_ab[g] for g in gs]
        mb = [a_ab[g].astype(BF16) for g in gs]
        mk = [_dot(mb[g], bd(mb[g])) for g in gs]
        for step in range(1, n_doubling - 1):
            mb = [mk[g].astype(BF16) for g in gs]
            rr = [_dot(mb[g], jnp.concatenate([bd(mb[g]), bd(tk[g].astype(BF16))], axis=1))
                  for g in gs]
            mk = [rr[g][:, 0:gw] for g in gs]
            tk = [tk[g] + rr[g][:, gw:2 * gw] for g in gs]
        tk = [tk[g] + _dot(mk[g].astype(BF16), bd(tk[g].astype(BF16))) for g in gs]
        for g in gs:
            t_ref[c, g] = tk[g].astype(BF16)
        return carry

    lax.fori_loop(0, n_chunks, prep_body, 0)

    def chunk_body(c, carry):
        r0 = pl.multiple_of(c * c_len, c_len)
        ar = [ar_ref[c, :, sl] for sl in sls]
        vv = [v_ref[pl.ds(r0, c_len), sl] for sl in sls]
        s0 = [s_ref[g] for g in gs]
        m1 = [_dot_nt(ar[g], s0[g].astype(BF16)) for g in gs]
        xb = [(m1[g][0:c_len] + av_ref[c, g]).astype(BF16) for g in gs]
        ub = [_dot(t_ref[c, g], bd(xb[g])).astype(BF16) for g in gs]
        ds = [_dot_tn(jnp.concatenate([ub[g], vv[g]], axis=0), bkh_ref[c, :, sls[g]]) for g in gs]
        for g in gs:
            s_ref[g] = s0[g] * pc_ref[c][:, sls[g]] + jnp.where(same_head, ds[g], 0.0)
        yy = [_dot(wcat_ref[c, g], jnp.concatenate([bd(ub[g]), bd(vv[g])], axis=0)) for g in gs]
        for g in gs:
            y_ref[pl.ds(r0, c_len), sls[g]] = m1[g][c_len:2 * c_len] + yy[g]
        return carry

    lax.fori_loop(0, n_chunks, chunk_body, 0)

    y = y_ref[...]
    inv_n = 1.0 / head_dim
    mean = head_sum(y) * inv_n
    d = y - mean
    var = head_sum(d * d) * inv_n
    yn = d * lax.rsqrt(var + LNX_EPS) * lg_ref[...] + lb_ref[...]
    bonus = head_sum(r * k2 * rk_ref[...]) * v
    ya_ref[0] = ((yn + bonus) * gate).astype(ya_ref.dtype)

    @pl.when(t == pl.num_programs(1) - 1)
    def _fin():
        for h in range(heads):
            g, hh = divmod(h, hpg)
            lo = hh * head_dim
            so_ref[0, h] = s_ref[g, lo:lo + head_dim, lo:lo + head_dim]


def _wkv_branch(za, shift0, s0, mu, w0, wdup, a0, waup, wgup, k_k, k_a, r_k, lnx_g, lnx_b):
    b, t, asw = za.shape
    heads, head_dim = s0.shape[1], s0.shape[2]
    aw = heads * head_dim
    ranks = (wdup.shape[0], waup.shape[0], wgup.shape[0])
    tt = _tile(t, 256, WKV_CHUNK)
    c_len = min(WKV_CHUNK, tt)
    n_chunks = tt // c_len
    groups = aw // WKV_GROUP_LANES
    row = lambda a: a.reshape(1, -1)
    const = lambda shape: pl.BlockSpec(shape, lambda i, j: (0,) * len(shape))
    kern = functools.partial(_wkv_kernel, heads=heads, head_dim=head_dim, ranks=ranks)
    return pl.pallas_call(
        kern,
        grid=(b, t // tt),
        in_specs=[
            pl.BlockSpec((1, tt, asw), lambda i, j: (i, j, 0)),
            pl.BlockSpec((1, 1, asw), lambda i, j: (i, 0, 0)),
            pl.BlockSpec((1, heads, head_dim, head_dim), lambda i, j: (i, 0, 0, 0)),
            const((1, asw)), const((1, aw)), const(wdup.shape), const((1, aw)), const(waup.shape),
            const(wgup.shape), const((1, aw)), const((1, aw)), const((1, aw)), const((1, aw)),
            const((1, aw)),
        ],
        out_specs=[
            pl.BlockSpec((1, tt, aw), lambda i, j: (i, j, 0)),
            pl.BlockSpec((1, 1, asw), lambda i, j: (i, 0, 0)),
            pl.BlockSpec((1, heads, head_dim, head_dim), lambda i, j: (i, 0, 0, 0)),
        ],
        out_shape=[
            jax.ShapeDtypeStruct((b, t, aw), BF16),
            jax.ShapeDtypeStruct((b, 1, asw), F32),
            jax.ShapeDtypeStruct((b, heads, head_dim, head_dim), F32),
        ],
        scratch_shapes=[
            pltpu.VMEM((1, asw), F32),
            pltpu.VMEM((groups, WKV_GROUP_LANES, WKV_GROUP_LANES), F32),
            pltpu.VMEM((n_chunks, 2 * c_len, aw), BF16),
            pltpu.VMEM((n_chunks, 2 * c_len, aw), BF16),
            pltpu.VMEM((n_chunks, 2 * c_len, aw), BF16),
            pltpu.VMEM((tt, aw), BF16),
            pltpu.VMEM((tt, aw), F32),
            pltpu.VMEM((n_chunks, 1, aw), F32),
            pltpu.VMEM((n_chunks, groups, c_len, WKV_GROUP_LANES), BF16),
            pltpu.VMEM((n_chunks, groups, c_len, WKV_GROUP_LANES), F32),
            pltpu.VMEM((n_chunks, groups, c_len, 2 * WKV_GROUP_LANES), BF16),
        ],
        compiler_params=_params("arbitrary", "arbitrary"),
        name="wkv7_chunked",
    )(za, shift0.reshape(b, 1, asw), s0, row(mu), row(w0), wdup, row(a0), waup, wgup,
      row(k_k), row(k_a), row(r_k), row(lnx_g), row(lnx_b))


def _ret_proj_kernel(x_ref, g_ref, w_ref, cos_ref, sin_ref, r0_ref, gn_ref, yb_ref, ro_ref,
                     st_ref, dm_ref, qd_ref, kd_ref, *, heads, dk, dv, blk, steps_per_seq):
    tm = x_ref.shape[0]
    hw = 2 * dk + 2 * dv
    i = pl.program_id(0)
    log_gammas = [math.log1p(-(2.0 ** (-5.0 - h))) for h in range(heads)]

    @pl.when(i == 0)
    def _tables():
        diff = (lax.broadcasted_iota(jnp.int32, (blk, blk), 0)
                - lax.broadcasted_iota(jnp.int32, (blk, blk), 1)).astype(F32)
        pos = lax.broadcasted_iota(jnp.int32, (blk, dk), 0).astype(F32)
        for h in range(heads):
            lg = log_gammas[h]
            dm_ref[h] = jnp.where(diff >= 0, jnp.exp(jnp.maximum(diff, 0.0) * lg), 0.0)
            qd_ref[h] = jnp.exp((pos + 1.0) * lg) * (dk ** -0.5)
            kd_ref[h] = jnp.exp((blk - 1.0 - pos) * lg)

    @pl.when(i % steps_per_seq == 0)
    def _init():
        st_ref[...] = r0_ref[0]

    hn = _rms(x_ref[...], g_ref[...]).astype(BF16)
    for h in range(heads):
        zh = _dot(hn, w_ref[:, h * hw:(h + 1) * hw])
        for s in range(tm // blk):
            rows = slice(s * blk, (s + 1) * blk)
            cosv = cos_ref[rows, :]
            sinv = sin_ref[rows, :]
            qh = zh[rows, 0:dk]
            kh = zh[rows, dk:2 * dk]
            vh = zh[rows, 2 * dk:2 * dk + dv].astype(BF16)
            gate = zh[rows, 2 * dk + dv:hw]
            qrot = qh * cosv + pltpu.roll(qh, dk // 2, 1) * sinv
            kr = kh * cosv + pltpu.roll(kh, dk // 2, 1) * sinv
            scores = _dot_nt((qrot * (dk ** -0.5)).astype(BF16), kr.astype(BF16)) * dm_ref[h]
            inner = _dot(scores.astype(BF16), vh)
            st = st_ref[h]
            cross = _dot((qrot * qd_ref[h]).astype(BF16), st.astype(BF16))
            ke = kr * kd_ref[h]
            st_ref[h] = math.exp(blk * log_gammas[h]) * st + _dot_tn(ke.astype(BF16), vh)
            o = inner + cross
            mean = jnp.mean(o, axis=-1, keepdims=True)
            d = o - mean
            var = jnp.mean(d * d, axis=-1, keepdims=True)
            on = d * lax.rsqrt(var + GN_EPS) * gn_ref[:, h * dv:(h + 1) * dv]
            yb_ref[rows, h * dv:(h + 1) * dv] = (jax.nn.silu(gate) * on).astype(yb_ref.dtype)

    @pl.when(i % steps_per_seq == steps_per_seq - 1)
    def _fin():
        ro_ref[0] = st_ref[...]


def _ret_proj(x, g, w_heads, cos_t, sin_t, r0, gn_g, seq_len):
    m, d = x.shape
    heads, dk, dv = r0.shape[1], r0.shape[2], r0.shape[3]
    vw = heads * dv
    tm = _tile(seq_len, 512, 64)
    blk = _tile(tm, 256, 64)
    sps = seq_len // tm
    kern = functools.partial(_ret_proj_kernel, heads=heads, dk=dk, dv=dv, blk=blk, steps_per_seq=sps)
    return pl.pallas_call(
        kern,
        grid=(m // tm,),
        in_specs=[
            pl.BlockSpec((tm, d), lambda i: (i, 0)),
            pl.BlockSpec((1, d), lambda i: (0, 0)),
            pl.BlockSpec(w_heads.shape, lambda i: (0, 0), pipeline_mode=pl.Buffered(1)),
            pl.BlockSpec((tm, dk), lambda i: (i % sps, 0)),
            pl.BlockSpec((tm, dk), lambda i: (i % sps, 0)),
            pl.BlockSpec((1, heads, dk, dv), lambda i: (i // sps, 0, 0, 0)),
            pl.BlockSpec((1, vw), lambda i: (0, 0)),
        ],
        out_specs=[
            pl.BlockSpec((tm, vw), lambda i: (i, 0)),
            pl.BlockSpec((1, heads, dk, dv), lambda i: (i // sps, 0, 0, 0)),
        ],
        out_shape=[
            jax.ShapeDtypeStruct((m, vw), BF16),
            jax.ShapeDtypeStruct((m // seq_len, heads, dk, dv), F32),
        ],
        scratch_shapes=[pltpu.VMEM((heads, dk, dv), F32), pltpu.VMEM((heads, blk, blk), F32),
                        pltpu.VMEM((heads, blk, dk), F32), pltpu.VMEM((heads, blk, dk), F32)],
        compiler_params=_params("arbitrary", vmem_limit=VMEM_LIMIT_RESIDENT_BYTES),
        name="retention_proj",
    )(x, g, w_heads, cos_t, sin_t, r0, gn_g.reshape(1, -1))


def _merge_kernel(ya_ref, yb_ref, zg_ref, x_ref, wba_ref, wbb_ref, wo_ref, g_ref, o_ref):
    d = x_ref.shape[1]
    ga = jax.nn.sigmoid(zg_ref[:, 0:d].astype(F32))
    gb = jax.nn.sigmoid(zg_ref[:, d:2 * d].astype(F32))
    merged = ga * _dot(ya_ref[...], wba_ref[...]) + gb * _dot(yb_ref[...], wbb_ref[...])
    mo = _dot(merged.astype(BF16), wo_ref[...])
    o_ref[...] = x_ref[...] + _rms(mo, g_ref[...])


def _merge(ya, yb, zg, x, wba, wbb, wo, g):
    m, d = x.shape
    tm = _tile(m, 256, 8)
    tok = lambda w: pl.BlockSpec((tm, w), lambda i: (i, 0))
    res = lambda a: pl.BlockSpec(a.shape, lambda i: (0, 0), pipeline_mode=pl.Buffered(1))
    return pl.pallas_call(
        _merge_kernel,
        grid=(m // tm,),
        in_specs=[tok(ya.shape[1]), tok(yb.shape[1]), tok(zg.shape[1]), tok(d),
                  res(wba), res(wbb), res(wo), pl.BlockSpec((1, d), lambda i: (0, 0))],
        out_specs=tok(d),
        out_shape=jax.ShapeDtypeStruct((m, d), F32),
        compiler_params=_params("arbitrary"),
        name="merge_out",
    )(ya, yb, zg, x, wba, wbb, wo, g)


def _ffn_kernel(x_ref, gpre_ref, wg_ref, wu_ref, wo_ref, gpost_ref, o_ref, h_ref, acc_ref):
    j = pl.program_id(1)

    @pl.when(j == 0)
    def _():
        h_ref[...] = _rms(x_ref[...], gpre_ref[...]).astype(BF16)
        acc_ref[...] = jnp.zeros_like(acc_ref)

    h = h_ref[...]
    act = jax.nn.silu(_dot(h, wg_ref[...])) * _dot(h, wu_ref[...])
    acc_ref[...] += _dot(act.astype(BF16), wo_ref[...])

    @pl.when(j == pl.num_programs(1) - 1)
    def _():
        o_ref[...] = x_ref[...] + _rms(acc_ref[...], gpost_ref[...])


def _ffn(x, gpre, w_fi, w_fo, gpost):
    m, d = x.shape
    f = w_fo.shape[0]
    tm = _tile(m, 512, 8)
    tf = _tile(f, 512)
    nf = f // tf
    return pl.pallas_call(
        _ffn_kernel,
        grid=(m // tm, nf),
        in_specs=[
            pl.BlockSpec((tm, d), lambda i, j: (i, 0)),
            pl.BlockSpec((1, d), lambda i, j: (0, 0)),
            pl.BlockSpec((d, tf), lambda i, j: (0, j)),
            pl.BlockSpec((d, tf), lambda i, j: (0, j + nf)),
            pl.BlockSpec((tf, d), lambda i, j: (j, 0)),
            pl.BlockSpec((1, d), lambda i, j: (0, 0)),
        ],
        out_specs=pl.BlockSpec((tm, d), lambda i, j: (i, 0)),
        out_shape=jax.ShapeDtypeStruct((m, d), F32),
        scratch_shapes=[pltpu.VMEM((tm, d), BF16), pltpu.VMEM((tm, d), F32)],
        compiler_params=_params("arbitrary", "arbitrary"),
        name="swiglu_ffn",
    )(x, gpre, w_fi, w_fi, w_fo, gpost)


def _rope_tables(pos, dk):
    half = dk // 2
    inv = ROPE_BASE ** (-jnp.arange(half, dtype=F32) / half)
    ang = pos[:, None] * inv[None, :]
    cos = jnp.cos(ang)
    sin = jnp.sin(ang)
    return jnp.concatenate([cos, cos], axis=-1), jnp.concatenate([-sin, sin], axis=-1)


def _head_major(w_b, heads, dk, dv):
    d = w_b.shape[0]
    qkw, vw = heads * dk, heads * dv
    parts = [w_b[:, 0:qkw].reshape(d, heads, dk), w_b[:, qkw:2 * qkw].reshape(d, heads, dk),
             w_b[:, 2 * qkw:2 * qkw + vw].reshape(d, heads, dv),
             w_b[:, 2 * qkw + vw:].reshape(d, heads, dv)]
    return jnp.concatenate(parts, axis=2).reshape(d, heads * (2 * dk + 2 * dv))


def _layer(x, rope, shift0, s0, r0, w):
    b, t, d = x.shape
    x2 = x.reshape(b * t, d)
    row = lambda a: a.reshape(1, -1)
    za = _norm_matmul(x2, row(w["nmp"]), w["w_a"], F32)
    zg = _norm_matmul(x2, row(w["nmp"]), w["w_g"], BF16)
    ya, shift, s_new = _wkv_branch(za.reshape(b, t, -1), shift0, s0, w["mu"], w["w0"], w["w_dup"],
                                   w["a0"], w["w_aup"], w["w_gup"], w["k_k"], w["k_a"], w["r_k"],
                                   w["lnx_g"], w["lnx_b"])
    yb, r_new = _ret_proj(x2, row(w["nmp"]), w["w_b"], rope[0], rope[1], r0, w["ret_g"], t)
    x1 = _merge(ya.reshape(b * t, -1), yb, zg, x2, w["w_ba"], w["w_bb"],
                w["w_out"], row(w["nmq"]))
    xo = _ffn(x1, row(w["nfp"]), w["w_fi"], w["w_fo"], row(w["nfq"]))
    return xo.reshape(b, t, d), shift.reshape(b, -1), s_new, r_new


def kernel(x_prompt, x_sample, state_wkv, state_ret, state_shift, norm_mix_pre, norm_mix_post, norm_ffn_pre, norm_ffn_post, w_in, shift_mu, decay_base, w_decay_up, iclr_base, w_iclr_up, w_gate_up, key_k, key_a, bonus_rk, lnx_gain, lnx_bias, w_branch_a, ret_norm_gain, w_branch_b, w_out, w_ffn_in, w_ffn_out):
    depth = w_in.shape[0]
    bp, tp, _ = x_prompt.shape
    bs, ts, _ = x_sample.shape
    asw = shift_mu.shape[1]
    heads_a, hd_a = bonus_rk.shape[1], bonus_rk.shape[2]
    heads_b, dk, dv = state_ret.shape[2], state_ret.shape[3], state_ret.shape[4]
    b_in = 2 * heads_b * dk + 2 * heads_b * dv
    dt = x_prompt.dtype

    rope_p = _rope_tables(jnp.arange(tp, dtype=F32), dk)
    rope_s = _rope_tables(jnp.arange(ts, dtype=F32) + jnp.float32(PAST_LEN), dk)
    zero_shift = jnp.zeros((bp, asw), dt)
    zero_wkv = jnp.zeros((bp, heads_a, hd_a, hd_a), dt)
    zero_ret = jnp.zeros((bp, heads_b, dk, dv), dt)

    xp, xs = x_prompt, x_sample
    outs = [[] for _ in range(6)]
    for l in range(depth):
        w = dict(
            nmp=norm_mix_pre[l], nmq=norm_mix_post[l], nfp=norm_ffn_pre[l], nfq=norm_ffn_post[l],
            w_a=w_in[l, :, :asw].astype(BF16),
            w_b=_head_major(w_in[l, :, asw:asw + b_in], heads_b, dk, dv).astype(BF16),
            w_g=w_in[l, :, asw + b_in:].astype(BF16),
            mu=shift_mu[l], w0=decay_base[l], w_dup=w_decay_up[l].astype(BF16), a0=iclr_base[l],
            w_aup=w_iclr_up[l].astype(BF16), w_gup=w_gate_up[l].astype(BF16), k_k=key_k[l],
            k_a=key_a[l], r_k=bonus_rk[l], lnx_g=lnx_gain[l], lnx_b=lnx_bias[l],
            w_ba=w_branch_a[l].astype(BF16), ret_g=ret_norm_gain[l],
            w_bb=w_branch_b[l].astype(BF16), w_out=w_out[l].astype(BF16),
            w_fi=w_ffn_in[l].astype(BF16), w_fo=w_ffn_out[l].astype(BF16),
        )
        xp, sp, wkp, rtp = _layer(xp, rope_p, zero_shift, zero_wkv, zero_ret, w)
        xs, ss, wks, rts = _layer(xs, rope_s, state_shift[l], state_wkv[l], state_ret[l], w)
        for lst, val in zip(outs, (wkp, rtp, sp, wks, rts, ss)):
            lst.append(val)
    return (xp, xs) + tuple(jnp.stack(o) for o in outs)
```

```python
import functools
import math

import jax
import jax.numpy as jnp
from jax import lax
from jax.experimental import pallas as pl
from jax.experimental.pallas import tpu as pltpu

F32 = jnp.float32
BF16 = jnp.bfloat16

PAST_LEN = 4096
RMS_EPS = 1e-6
LNX_EPS = 64e-5
GN_EPS = 1e-5
ROPE_BASE = 10000.0

V7X_LANES = 128
V7X_MXU_DIM = 256
WKV_GROUP_LANES = 128
WKV_BLOCK_LANES = 256
WKV_CHUNK = 64
V7X_VMEM_BYTES = 64 * 1024 * 1024
VMEM_LIMIT_BYTES = V7X_VMEM_BYTES * 7 // 8
VMEM_LIMIT_RESIDENT_BYTES = V7X_VMEM_BYTES * 15 // 16


def _dot(a, b):
    return jnp.dot(a, b, preferred_element_type=F32)


def _dot_nt(a, b):
    return lax.dot_general(a, b, (((1,), (1,)), ((), ())), preferred_element_type=F32)


def _dot_tn(a, b):
    return lax.dot_general(a, b, (((0,), (0,)), ((), ())), preferred_element_type=F32)


def _rms(x, g):
    return x * lax.rsqrt(jnp.mean(x * x, axis=-1, keepdims=True) + RMS_EPS) * g


def _tile(n, target, mult=V7X_LANES):
    if n <= target:
        return n
    best = None
    for t in range(mult, target + 1, mult):
        if n % t == 0:
            best = t
    assert best is not None, (n, target, mult)
    return best


def _params(*sem, vmem_limit=VMEM_LIMIT_BYTES):
    return pltpu.CompilerParams(dimension_semantics=sem, vmem_limit_bytes=vmem_limit)


def _norm_matmul_kernel(x_ref, g_ref, w_ref, o_ref, *, tn):
    h = _rms(x_ref[...], g_ref[...]).astype(BF16)
    for j in range(w_ref.shape[1] // tn):
        cols = slice(j * tn, (j + 1) * tn)
        o_ref[:, cols] = _dot(h, w_ref[:, cols]).astype(o_ref.dtype)


def _norm_matmul(x, g, w, out_dtype, tm_target=512, tn_target=512):
    m, d = x.shape
    n = w.shape[1]
    tm = _tile(m, tm_target, 8)
    tn = _tile(n, tn_target)
    return pl.pallas_call(
        functools.partial(_norm_matmul_kernel, tn=tn),
        grid=(m // tm,),
        in_specs=[
            pl.BlockSpec((tm, d), lambda i: (i, 0)),
            pl.BlockSpec((1, d), lambda i: (0, 0)),
            pl.BlockSpec((d, n), lambda i: (0, 0), pipeline_mode=pl.Buffered(1)),
        ],
        out_specs=pl.BlockSpec((tm, n), lambda i: (i, 0)),
        out_shape=jax.ShapeDtypeStruct((m, n), out_dtype),
        compiler_params=_params("arbitrary"),
        name="norm_matmul",
    )(x, g, w)


def _split_bf16(x):
    hi = x.astype(BF16)
    lo = (x - hi.astype(F32)).astype(BF16)
    return hi, lo


def _wkv_kernel(x_ref, gn_ref, wa_ref, sh0_ref, s0_ref, mu_ref, w0_ref, wdup_ref, a0_ref, waup_ref,
                wgup_ref, kk_ref, ka_ref, rk_ref, lg_ref, lb_ref,
                ya_ref, sho_ref, so_ref,
                carry_ref, s_ref, ar_ref, bk_ref, bkh_ref, v_ref, y_ref, pc_ref, t_ref, av_ref, wcat_ref,
                v32_ref, rk2_ref, gate_ref,
                *, heads, head_dim, ranks):
    tt = x_ref.shape[1]
    aw = heads * head_dim
    gw = WKV_GROUP_LANES
    hpg = gw // head_dim
    groups = aw // gw
    c_len = min(WKV_CHUNK, tt)
    n_chunks = tt // c_len
    dr, ir, gr = ranks
    t = pl.program_id(1)

    @pl.when(t == 0)
    def _init():
        carry_ref[...] = sh0_ref[0]
        s_ref[...] = jnp.zeros_like(s_ref)
        for h in range(heads):
            g, hh = divmod(h, hpg)
            lo = hh * head_dim
            s_ref[g, lo:lo + head_dim, lo:lo + head_dim] = s0_ref[0, h]

    hn = _rms(x_ref[0], gn_ref[...]).astype(BF16)
    lw = dr + ir + gr

    def project_shifted(cols):
        z = _dot(hn, wa_ref[:, cols])
        rolled = pltpu.roll(z, 1, 0)
        row8 = lax.broadcasted_iota(jnp.int32, (8, z.shape[1]), 0)
        prev = jnp.concatenate([jnp.where(row8 == 0, carry_ref[:, cols], rolled[0:8]), rolled[8:]],
                               axis=0)
        last = z[tt - 1:tt, :]
        carry_ref[:, cols] = last
        sho_ref[0, :, cols] = last
        return z + mu_ref[:, cols] * (prev - z)

    zl = project_shifted(slice(0, lw))
    wd = zl[:, 0:dr]
    ad = zl[:, dr:dr + ir]
    gd = zl[:, dr + ir:lw]
    xw = w0_ref[...] + _dot(jnp.tanh(wd).astype(BF16), wdup_ref[...])
    ld = (-math.exp(-0.5)) * jax.nn.sigmoid(xw)
    a_ic = jax.nn.sigmoid(a0_ref[...] + _dot(ad.astype(BF16), waup_ref[...]))
    gate_ref[...] = _dot(jax.nn.sigmoid(gd).astype(BF16), wgup_ref[...])

    gi = lax.broadcasted_iota(jnp.int32, (gw, gw), 0) // head_dim
    gj = lax.broadcasted_iota(jnp.int32, (gw, gw), 1) // head_dim
    same_head = gi == gj
    ones_bd = same_head.astype(BF16)

    sw = min(aw, WKV_BLOCK_LANES)
    si = lax.broadcasted_iota(jnp.int32, (sw, sw), 0) // head_dim
    sj = lax.broadcasted_iota(jnp.int32, (sw, sw), 1) // head_dim
    ones_sw = (si == sj).astype(BF16)

    def head_sum(x):
        xb = x.astype(BF16)
        parts = [_dot(xb[:, q * sw:(q + 1) * sw], ones_sw) for q in range(aw // sw)]
        return jnp.concatenate(parts, axis=1) if len(parts) > 1 else parts[0]

    ti = lax.broadcasted_iota(jnp.int32, (tt, tt), 0)
    tj = lax.broadcasted_iota(jnp.int32, (tt, tt), 1)
    tri = (((ti // c_len) == (tj // c_len)) & (tj <= ti)).astype(BF16)
    ld_hi, ld_lo = _split_bf16(ld)
    cum_all = _dot(tri, ld_hi) + _dot(tri, ld_lo)

    for q in range(aw // sw):
        gl = slice(q * sw, (q + 1) * sw)
        z3 = project_shifted(slice(lw + 3 * q * sw, lw + 3 * (q + 1) * sw))
        r = z3[:, 0:sw]
        k = z3[:, sw:2 * sw]
        v = z3[:, 2 * sw:3 * sw]
        a_g = a_ic[:, gl]
        ld_g = ld[:, gl]
        kk = k * kk_ref[:, gl]
        k2 = k * (1.0 + (a_g - 1.0) * ka_ref[:, gl])
        kkn = kk * jnp.minimum(lax.rsqrt(_dot((kk * kk).astype(BF16), ones_sw)), 1e12)
        cum = cum_all[:, gl]
        p_inc = jnp.exp(cum)
        p_inv = jnp.exp(-cum)
        a_t = -(kkn * jnp.exp(cum - ld_g))
        r_t = r * p_inc
        b_t = (kkn * a_g) * p_inv
        k_t = k2 * p_inv
        for c in range(n_chunks):
            rs = slice(c * c_len, (c + 1) * c_len)
            p_end = p_inc[(c + 1) * c_len - 1:(c + 1) * c_len, :]
            ar_ref[c, 0:c_len, gl] = a_t[rs].astype(BF16)
            ar_ref[c, c_len:2 * c_len, gl] = r_t[rs].astype(BF16)
            bk_ref[c, 0:c_len, gl] = b_t[rs].astype(BF16)
            bk_ref[c, c_len:2 * c_len, gl] = k_t[rs].astype(BF16)
            bkh_ref[c, 0:c_len, gl] = (b_t[rs] * p_end).astype(BF16)
            bkh_ref[c, c_len:2 * c_len, gl] = (k_t[rs] * p_end).astype(BF16)
            pc_ref[c, :, gl] = p_end
        v_ref[:, gl] = v.astype(BF16)
        v32_ref[:, gl] = v
        rk2_ref[:, gl] = r * k2 * rk_ref[:, gl]

    lane_head = lax.broadcasted_iota(jnp.int32, (1, gw), 1) // head_dim
    ci = lax.broadcasted_iota(jnp.int32, (c_len, gw), 0)
    cj = lax.broadcasted_iota(jnp.int32, (c_len, gw), 1) % head_dim
    assert c_len == head_dim
    strict = cj < ci
    incl = cj <= ci
    eye = (cj == ci).astype(F32)
    n_doubling = int(math.log2(c_len))

    def bd(m):
        zero = jnp.zeros_like(m)
        return jnp.concatenate([jnp.where(lane_head == h, m, zero) for h in range(hpg)], axis=0)

    gs = range(groups)
    sls = [slice(g * gw, (g + 1) * gw) for g in gs]

    def prep_body(c, carry):
        r0 = pl.multiple_of(c * c_len, c_len)
        ar = [ar_ref[c, :, sl] for sl in sls]
        bk = [bk_ref[c, :, sl] for sl in sls]
        vv = [v_ref[pl.ds(r0, c_len), sl] for sl in sls]
        s4 = [_dot_nt(ar[g], jnp.concatenate([bd(bk[g][0:c_len]), bd(bk[g][c_len:2 * c_len])],
                                              axis=0)) for g in gs]
        a_ab = [jnp.where(strict, s4[g][0:c_len, 0:gw], 0.0) for g in gs]
        a_ak = [jnp.where(strict, s4[g][0:c_len, gw:2 * gw], 0.0).astype(BF16) for g in gs]
        for g in gs:
            wcat_ref[c, g, :, 0:gw] = jnp.where(incl, s4[g][c_len:2 * c_len, 0:gw], 0.0).astype(BF16)
            wcat_ref[c, g, :, gw:2 * gw] = jnp.where(incl, s4[g][c_len:2 * c_len, gw:2 * gw],
                                                     0.0).astype(BF16)
        av = [_dot(a_ak[g], bd(vv[g])) for g in gs]
        for g in gs:
            av_ref[c, g] = av[g]
        tk = [eye + a_ab[g] for g in gs]
        mb = [a_ab[g].astype(BF16) for g in gs]
        mk = [_dot(mb[g], bd(mb[g])) for g in gs]
        for step in range(1, n_doubling - 1):
            mb = [mk[g].astype(BF16) for g in gs]
            rr = [_dot(mb[g], jnp.concatenate([bd(mb[g]), bd(tk[g].astype(BF16))], axis=1))
                  for g in gs]
            mk = [rr[g][:, 0:gw] for g in gs]
            tk = [tk[g] + rr[g][:, gw:2 * gw] for g in gs]
        tk = [tk[g] + _dot(mk[g].astype(BF16), bd(tk[g].astype(BF16))) for g in gs]
        for g in gs:
            t_ref[c, g] = tk[g].astype(BF16)
        return carry

    lax.fori_loop(0, n_chunks, prep_body, 0)

    def chunk_body(c, carry):
        r0 = pl.multiple_of(c * c_len, c_len)
        ar = [ar_ref[c, :, sl] for sl in sls]
        vv = [v_ref[pl.ds(r0, c_len), sl] for sl in sls]
        s0 = [s_ref[g] for g in gs]
        m1 = [_dot_nt(ar[g], s0[g].astype(BF16)) for g in gs]
        xb = [(m1[g][0:c_len] + av_ref[c, g]).astype(BF16) for g in gs]
        ub = [_dot(t_ref[c, g], bd(xb[g])).astype(BF16) for g in gs]
        ds = [_dot_tn(jnp.concatenate([ub[g], vv[g]], axis=0), bkh_ref[c, :, sls[g]]) for g in gs]
        for g in gs:
            s_ref[g] = s0[g] * pc_ref[c][:, sls[g]] + jnp.where(same_head, ds[g], 0.0)
        yy = [_dot(wcat_ref[c, g], jnp.concatenate([bd(ub[g]), bd(vv[g])], axis=0)) for g in gs]
        for g in gs:
            y_ref[pl.ds(r0, c_len), sls[g]] = m1[g][c_len:2 * c_len] + yy[g]
        return carry

    lax.fori_loop(0, n_chunks, chunk_body, 0)

    y = y_ref[...]
    inv_n = 1.0 / head_dim
    mean = head_sum(y) * inv_n
    d = y - mean
    var = head_sum(d * d) * inv_n
    yn = d * lax.rsqrt(var + LNX_EPS) * lg_ref[...] + lb_ref[...]
    bonus = head_sum(rk2_ref[...]) * v32_ref[...]
    ya_ref[0] = ((yn + bonus) * gate_ref[...]).astype(ya_ref.dtype)

    @pl.when(t == pl.num_programs(1) - 1)
    def _fin():
        for h in range(heads):
            g, hh = divmod(h, hpg)
            lo = hh * head_dim
            so_ref[0, h] = s_ref[g, lo:lo + head_dim, lo:lo + head_dim]


def _group_major(a, aw):
    lead = a.shape[:-1]
    bw = min(aw, WKV_BLOCK_LANES)
    g = aw // bw
    rkv = a[..., :3 * aw].reshape(*lead, 3, g, bw)
    rkv = jnp.swapaxes(rkv, -3, -2).reshape(*lead, 3 * aw)
    return jnp.concatenate([a[..., 3 * aw:], rkv], axis=-1)


def _group_major_inverse(a, aw):
    lead = a.shape[:-1]
    bw = min(aw, WKV_BLOCK_LANES)
    g = aw // bw
    lw = a.shape[-1] - 3 * aw
    rkv = a[..., lw:].reshape(*lead, g, 3, bw)
    rkv = jnp.swapaxes(rkv, -3, -2).reshape(*lead, 3 * aw)
    return jnp.concatenate([rkv, a[..., :lw]], axis=-1)


def _wkv_branch(x, gn, wa, shift0, s0, mu, w0, wdup, a0, waup, wgup, k_k, k_a, r_k, lnx_g, lnx_b):
    b, t, d = x.shape
    asw = wa.shape[1]
    heads, head_dim = s0.shape[1], s0.shape[2]
    aw = heads * head_dim
    ranks = (wdup.shape[0], waup.shape[0], wgup.shape[0])
    tt = _tile(t, 256, WKV_CHUNK)
    c_len = min(WKV_CHUNK, tt)
    n_chunks = tt // c_len
    groups = aw // WKV_GROUP_LANES
    row = lambda a: a.reshape(1, -1)
    const = lambda shape: pl.BlockSpec(shape, lambda i, j: (0,) * len(shape))
    kern = functools.partial(_wkv_kernel, heads=heads, head_dim=head_dim, ranks=ranks)
    return pl.pallas_call(
        kern,
        grid=(b, t // tt),
        in_specs=[
            pl.BlockSpec((1, tt, d), lambda i, j: (i, j, 0)),
            const((1, d)),
            pl.BlockSpec(wa.shape, lambda i, j: (0, 0), pipeline_mode=pl.Buffered(1)),
            pl.BlockSpec((1, 1, asw), lambda i, j: (i, 0, 0)),
            pl.BlockSpec((1, heads, head_dim, head_dim), lambda i, j: (i, 0, 0, 0)),
            const((1, asw)), const((1, aw)), const(wdup.shape), const((1, aw)), const(waup.shape),
            const(wgup.shape), const((1, aw)), const((1, aw)), const((1, aw)), const((1, aw)),
            const((1, aw)),
        ],
        out_specs=[
            pl.BlockSpec((1, tt, aw), lambda i, j: (i, j, 0)),
            pl.BlockSpec((1, 1, asw), lambda i, j: (i, 0, 0)),
            pl.BlockSpec((1, heads, head_dim, head_dim), lambda i, j: (i, 0, 0, 0)),
        ],
        out_shape=[
            jax.ShapeDtypeStruct((b, t, aw), BF16),
            jax.ShapeDtypeStruct((b, 1, asw), F32),
            jax.ShapeDtypeStruct((b, heads, head_dim, head_dim), F32),
        ],
        scratch_shapes=[
            pltpu.VMEM((1, asw), F32),
            pltpu.VMEM((groups, WKV_GROUP_LANES, WKV_GROUP_LANES), F32),
            pltpu.VMEM((n_chunks, 2 * c_len, aw), BF16),
            pltpu.VMEM((n_chunks, 2 * c_len, aw), BF16),
            pltpu.VMEM((n_chunks, 2 * c_len, aw), BF16),
            pltpu.VMEM((tt, aw), BF16),
            pltpu.VMEM((tt, aw), F32),
            pltpu.VMEM((n_chunks, 1, aw), F32),
            pltpu.VMEM((n_chunks, groups, c_len, WKV_GROUP_LANES), BF16),
            pltpu.VMEM((n_chunks, groups, c_len, WKV_GROUP_LANES), F32),
            pltpu.VMEM((n_chunks, groups, c_len, 2 * WKV_GROUP_LANES), BF16),
            pltpu.VMEM((tt, aw), F32),
            pltpu.VMEM((tt, aw), F32),
            pltpu.VMEM((tt, aw), F32),
        ],
        compiler_params=_params("arbitrary", "arbitrary"),
        name="wkv7_chunked",
    )(x, gn, wa, shift0.reshape(b, 1, asw), s0, row(mu), row(w0), wdup, row(a0), waup, wgup,
      row(k_k), row(k_a), row(r_k), row(lnx_g), row(lnx_b))


def _ret_proj_kernel(x_ref, g_ref, w_ref, cos_ref, sin_ref, r0_ref, gn_ref, yb_ref, ro_ref,
                     st_ref, dm_ref, qd_ref, kd_ref, *, heads, dk, dv, blk, steps_per_seq):
    tm = x_ref.shape[0]
    hw = 2 * dk + 2 * dv
    i = pl.program_id(0)
    log_gammas = [math.log1p(-(2.0 ** (-5.0 - h))) for h in range(heads)]

    @pl.when(i == 0)
    def _tables():
        diff = (lax.broadcasted_iota(jnp.int32, (blk, blk), 0)
                - lax.broadcasted_iota(jnp.int32, (blk, blk), 1)).astype(F32)
        pos = lax.broadcasted_iota(jnp.int32, (blk, dk), 0).astype(F32)
        for h in range(heads):
            lg = log_gammas[h]
            dm_ref[h] = jnp.where(diff >= 0, jnp.exp(jnp.maximum(diff, 0.0) * lg), 0.0)
            qd_ref[h] = jnp.exp((pos + 1.0) * lg) * (dk ** -0.5)
            kd_ref[h] = jnp.exp((blk - 1.0 - pos) * lg)

    @pl.when(i % steps_per_seq == 0)
    def _init():
        st_ref[...] = r0_ref[0]

    hn = _rms(x_ref[...], g_ref[...]).astype(BF16)
    for h in range(heads):
        zh = _dot(hn, w_ref[:, h * hw:(h + 1) * hw])
        for s in range(tm // blk):
            rows = slice(s * blk, (s + 1) * blk)
            cosv = cos_ref[rows, :]
            sinv = sin_ref[rows, :]
            qh = zh[rows, 0:dk]
            kh = zh[rows, dk:2 * dk]
            vh = zh[rows, 2 * dk:2 * dk + dv].astype(BF16)
            gate = zh[rows, 2 * dk + dv:hw]
            qrot = qh * cosv + pltpu.roll(qh, dk // 2, 1) * sinv
            kr = kh * cosv + pltpu.roll(kh, dk // 2, 1) * sinv
            scores = _dot_nt((qrot * (dk ** -0.5)).astype(BF16), kr.astype(BF16)) * dm_ref[h]
            inner = _dot(scores.astype(BF16), vh)
            st = st_ref[h]
            cross = _dot((qrot * qd_ref[h]).astype(BF16), st.astype(BF16))
            ke = kr * kd_ref[h]
            st_ref[h] = math.exp(blk * log_gammas[h]) * st + _dot_tn(ke.astype(BF16), vh)
            o = inner + cross
            mean = jnp.mean(o, axis=-1, keepdims=True)
            d = o - mean
            var = jnp.mean(d * d, axis=-1, keepdims=True)
            on = d * lax.rsqrt(var + GN_EPS) * gn_ref[:, h * dv:(h + 1) * dv]
            yb_ref[rows, h * dv:(h + 1) * dv] = (jax.nn.silu(gate) * on).astype(yb_ref.dtype)

    @pl.when(i % steps_per_seq == steps_per_seq - 1)
    def _fin():
        ro_ref[0] = st_ref[...]


def _ret_proj(x, g, w_heads, cos_t, sin_t, r0, gn_g, seq_len):
    m, d = x.shape
    heads, dk, dv = r0.shape[1], r0.shape[2], r0.shape[3]
    vw = heads * dv
    tm = _tile(seq_len, 512, 64)
    blk = _tile(tm, 256, 64)
    sps = seq_len // tm
    kern = functools.partial(_ret_proj_kernel, heads=heads, dk=dk, dv=dv, blk=blk, steps_per_seq=sps)
    return pl.pallas_call(
        kern,
        grid=(m // tm,),
        in_specs=[
            pl.BlockSpec((tm, d), lambda i: (i, 0)),
            pl.BlockSpec((1, d), lambda i: (0, 0)),
            pl.BlockSpec(w_heads.shape, lambda i: (0, 0), pipeline_mode=pl.Buffered(1)),
            pl.BlockSpec((tm, dk), lambda i: (i % sps, 0)),
            pl.BlockSpec((tm, dk), lambda i: (i % sps, 0)),
            pl.BlockSpec((1, heads, dk, dv), lambda i: (i // sps, 0, 0, 0)),
            pl.BlockSpec((1, vw), lambda i: (0, 0)),
        ],
        out_specs=[
            pl.BlockSpec((tm, vw), lambda i: (i, 0)),
            pl.BlockSpec((1, heads, dk, dv), lambda i: (i // sps, 0, 0, 0)),
        ],
        out_shape=[
            jax.ShapeDtypeStruct((m, vw), BF16),
            jax.ShapeDtypeStruct((m // seq_len, heads, dk, dv), F32),
        ],
        scratch_shapes=[pltpu.VMEM((heads, dk, dv), F32), pltpu.VMEM((heads, blk, blk), F32),
                        pltpu.VMEM((heads, blk, dk), F32), pltpu.VMEM((heads, blk, dk), F32)],
        compiler_params=_params("arbitrary", vmem_limit=VMEM_LIMIT_RESIDENT_BYTES),
        name="retention_proj",
    )(x, g, w_heads, cos_t, sin_t, r0, gn_g.reshape(1, -1))


def _merge_kernel(ya_ref, yb_ref, zg_ref, x_ref, wba_ref, wbb_ref, wo_ref, g_ref, o_ref):
    d = x_ref.shape[1]
    ga = jax.nn.sigmoid(zg_ref[:, 0:d].astype(F32))
    gb = jax.nn.sigmoid(zg_ref[:, d:2 * d].astype(F32))
    merged = ga * _dot(ya_ref[...], wba_ref[...]) + gb * _dot(yb_ref[...], wbb_ref[...])
    mo = _dot(merged.astype(BF16), wo_ref[...])
    o_ref[...] = x_ref[...] + _rms(mo, g_ref[...])


def _merge(ya, yb, zg, x, wba, wbb, wo, g):
    m, d = x.shape
    tm = _tile(m, 256, 8)
    tok = lambda w: pl.BlockSpec((tm, w), lambda i: (i, 0))
    res = lambda a: pl.BlockSpec(a.shape, lambda i: (0, 0), pipeline_mode=pl.Buffered(1))
    return pl.pallas_call(
        _merge_kernel,
        grid=(m // tm,),
        in_specs=[tok(ya.shape[1]), tok(yb.shape[1]), tok(zg.shape[1]), tok(d),
                  res(wba), res(wbb), res(wo), pl.BlockSpec((1, d), lambda i: (0, 0))],
        out_specs=tok(d),
        out_shape=jax.ShapeDtypeStruct((m, d), F32),
        compiler_params=_params("arbitrary"),
        name="merge_out",
    )(ya, yb, zg, x, wba, wbb, wo, g)


def _ffn_kernel(x_ref, gpre_ref, wg_ref, wu_ref, wo_ref, gpost_ref, o_ref, h_ref):
    j = pl.program_id(1)

    @pl.when(j == 0)
    def _():
        h_ref[...] = _rms(x_ref[...], gpre_ref[...]).astype(BF16)
        o_ref[...] = jnp.zeros_like(o_ref)

    h = h_ref[...]
    act = jax.nn.silu(_dot(h, wg_ref[...])) * _dot(h, wu_ref[...])
    o_ref[...] += _dot(act.astype(BF16), wo_ref[...])

    @pl.when(j == pl.num_programs(1) - 1)
    def _():
        o_ref[...] = x_ref[...] + _rms(o_ref[...], gpost_ref[...])


def _ffn(x, gpre, w_fi, w_fo, gpost):
    m, d = x.shape
    f = w_fo.shape[0]
    tm = _tile(m, 512, 8)
    tf = _tile(f, 512)
    nf = f // tf
    return pl.pallas_call(
        _ffn_kernel,
        grid=(m // tm, nf),
        in_specs=[
            pl.BlockSpec((tm, d), lambda i, j: (i, 0)),
            pl.BlockSpec((1, d), lambda i, j: (0, 0)),
            pl.BlockSpec((d, tf), lambda i, j: (0, j)),
            pl.BlockSpec((d, tf), lambda i, j: (0, j + nf)),
            pl.BlockSpec((tf, d), lambda i, j: (j, 0)),
            pl.BlockSpec((1, d), lambda i, j: (0, 0)),
        ],
        out_specs=pl.BlockSpec((tm, d), lambda i, j: (i, 0)),
        out_shape=jax.ShapeDtypeStruct((m, d), F32),
        scratch_shapes=[pltpu.VMEM((tm, d), BF16)],
        compiler_params=_params("arbitrary", "arbitrary"),
        name="swiglu_ffn",
    )(x, gpre, w_fi, w_fi, w_fo, gpost)


def _rope_tables(pos, dk):
    half = dk // 2
    inv = ROPE_BASE ** (-jnp.arange(half, dtype=F32) / half)
    ang = pos[:, None] * inv[None, :]
    cos = jnp.cos(ang)
    sin = jnp.sin(ang)
    return jnp.concatenate([cos, cos], axis=-1), jnp.concatenate([-sin, sin], axis=-1)


def _head_major(w_b, heads, dk, dv):
    d = w_b.shape[0]
    qkw, vw = heads * dk, heads * dv
    parts = [w_b[:, 0:qkw].reshape(d, heads, dk), w_b[:, qkw:2 * qkw].reshape(d, heads, dk),
             w_b[:, 2 * qkw:2 * qkw + vw].reshape(d, heads, dv),
             w_b[:, 2 * qkw + vw:].reshape(d, heads, dv)]
    return jnp.concatenate(parts, axis=2).reshape(d, heads * (2 * dk + 2 * dv))


def _layer(x, rope, shift0, s0, r0, w):
    b, t, d = x.shape
    x2 = x.reshape(b * t, d)
    row = lambda a: a.reshape(1, -1)
    zg = _norm_matmul(x2, row(w["nmp"]), w["w_g"], BF16)
    aw = w["w_dup"].shape[1]
    ya, shift, s_new = _wkv_branch(x, row(w["nmp"]), w["w_a"], _group_major(shift0, aw), s0, w["mu"],
                                   w["w0"], w["w_dup"], w["a0"], w["w_aup"], w["w_gup"], w["k_k"],
                                   w["k_a"], w["r_k"], w["lnx_g"], w["lnx_b"])
    shift = _group_major_inverse(shift.reshape(b, -1), aw)
    yb, r_new = _ret_proj(x2, row(w["nmp"]), w["w_b"], rope[0], rope[1], r0, w["ret_g"], t)
    x1 = _merge(ya.reshape(b * t, -1), yb, zg, x2, w["w_ba"], w["w_bb"],
                w["w_out"], row(w["nmq"]))
    xo = _ffn(x1, row(w["nfp"]), w["w_fi"], w["w_fo"], row(w["nfq"]))
    return xo.reshape(b, t, d), shift, s_new, r_new


def kernel(x_prompt, x_sample, state_wkv, state_ret, state_shift, norm_mix_pre, norm_mix_post, norm_ffn_pre, norm_ffn_post, w_in, shift_mu, decay_base, w_decay_up, iclr_base, w_iclr_up, w_gate_up, key_k, key_a, bonus_rk, lnx_gain, lnx_bias, w_branch_a, ret_norm_gain, w_branch_b, w_out, w_ffn_in, w_ffn_out):
    depth = w_in.shape[0]
    bp, tp, _ = x_prompt.shape
    bs, ts, _ = x_sample.shape
    asw = shift_mu.shape[1]
    heads_a, hd_a = bonus_rk.shape[1], bonus_rk.shape[2]
    heads_b, dk, dv = state_ret.shape[2], state_ret.shape[3], state_ret.shape[4]
    b_in = 2 * heads_b * dk + 2 * heads_b * dv
    dt = x_prompt.dtype

    rope_p = _rope_tables(jnp.arange(tp, dtype=F32), dk)
    rope_s = _rope_tables(jnp.arange(ts, dtype=F32) + jnp.float32(PAST_LEN), dk)
    zero_shift = jnp.zeros((bp, asw), dt)
    zero_wkv = jnp.zeros((bp, heads_a, hd_a, hd_a), dt)
    zero_ret = jnp.zeros((bp, heads_b, dk, dv), dt)

    xp, xs = x_prompt, x_sample
    outs = [[] for _ in range(6)]
    for l in range(depth):
        w = dict(
            nmp=norm_mix_pre[l], nmq=norm_mix_post[l], nfp=norm_ffn_pre[l], nfq=norm_ffn_post[l],
            w_a=_group_major(w_in[l, :, :asw], heads_a * hd_a).astype(BF16),
            w_b=_head_major(w_in[l, :, asw:asw + b_in], heads_b, dk, dv).astype(BF16),
            w_g=w_in[l, :, asw + b_in:].astype(BF16),
            mu=_group_major(shift_mu[l], heads_a * hd_a), w0=decay_base[l], w_dup=w_decay_up[l].astype(BF16), a0=iclr_base[l],
            w_aup=w_iclr_up[l].astype(BF16), w_gup=w_gate_up[l].astype(BF16), k_k=key_k[l],
            k_a=key_a[l], r_k=bonus_rk[l], lnx_g=lnx_gain[l], lnx_b=lnx_bias[l],
            w_ba=w_branch_a[l].astype(BF16), ret_g=ret_norm_gain[l],
            w_bb=w_branch_b[l].astype(BF16), w_out=w_out[l].astype(BF16),
            w_fi=w_ffn_in[l].astype(BF16), w_fo=w_ffn_out[l].astype(BF16),
        )
        xp, sp, wkp, rtp = _layer(xp, rope_p, zero_shift, zero_wkv, zero_ret, w)
        xs, ss, wks, rts = _layer(xs, rope_s, state_shift[l], state_wkv[l], state_ret[l], w)
        for lst, val in zip(outs, (wkp, rtp, sp, wks, rts, ss)):
            lst.append(val)
    return (xp, xs) + tuple(jnp.stack(o) for o in outs)
```

```python
import functools
import math

import jax
import jax.numpy as jnp
from jax import lax
from jax.experimental import pallas as pl
from jax.experimental.pallas import tpu as pltpu

F32 = jnp.float32
BF16 = jnp.bfloat16

PAST_LEN = 4096
RMS_EPS = 1e-6
LNX_EPS = 64e-5
GN_EPS = 1e-5
ROPE_BASE = 10000.0

V7X_LANES = 128
V7X_MXU_DIM = 256
WKV_GROUP_LANES = V7X_LANES
WKV_BLOCK_LANES = V7X_MXU_DIM
WKV_CHUNK = 64
V7X_VMEM_BYTES = 64 * 1024 * 1024
VMEM_LIMIT_BYTES = V7X_VMEM_BYTES * 7 // 8
VMEM_LIMIT_RESIDENT_BYTES = V7X_VMEM_BYTES * 15 // 16


def _dot(a, b):
    return jnp.dot(a, b, preferred_element_type=F32)


def _dot_nt(a, b):
    return lax.dot_general(a, b, (((1,), (1,)), ((), ())), preferred_element_type=F32)


def _dot_tn(a, b):
    return lax.dot_general(a, b, (((0,), (0,)), ((), ())), preferred_element_type=F32)


def _rms(x, g):
    return x * lax.rsqrt(jnp.mean(x * x, axis=-1, keepdims=True) + RMS_EPS) * g


def _tile(n, target, mult=V7X_LANES):
    if n <= target:
        return n
    best = None
    for t in range(mult, target + 1, mult):
        if n % t == 0:
            best = t
    assert best is not None, (n, target, mult)
    return best


def _params(*sem, vmem_limit=VMEM_LIMIT_BYTES):
    return pltpu.CompilerParams(dimension_semantics=sem, vmem_limit_bytes=vmem_limit)


def _norm_matmul_kernel(x_ref, g_ref, w_ref, o_ref, *, tn):
    h = _rms(x_ref[...], g_ref[...]).astype(BF16)
    for j in range(w_ref.shape[1] // tn):
        cols = slice(j * tn, (j + 1) * tn)
        o_ref[:, cols] = _dot(h, w_ref[:, cols]).astype(o_ref.dtype)


def _norm_matmul(x, g, w, out_dtype, tm_target=512, tn_target=512):
    m, d = x.shape
    n = w.shape[1]
    tm = _tile(m, tm_target, 8)
    tn = _tile(n, tn_target)
    return pl.pallas_call(
        functools.partial(_norm_matmul_kernel, tn=tn),
        grid=(m // tm,),
        in_specs=[
            pl.BlockSpec((tm, d), lambda i: (i, 0)),
            pl.BlockSpec((1, d), lambda i: (0, 0)),
            pl.BlockSpec((d, n), lambda i: (0, 0), pipeline_mode=pl.Buffered(1)),
        ],
        out_specs=pl.BlockSpec((tm, n), lambda i: (i, 0)),
        out_shape=jax.ShapeDtypeStruct((m, n), out_dtype),
        compiler_params=_params("arbitrary"),
        name="norm_matmul",
    )(x, g, w)


def _split_bf16(x):
    hi = x.astype(BF16)
    lo = (x - hi.astype(F32)).astype(BF16)
    return hi, lo


def _wkv_kernel(x_ref, gn_ref, wa_ref, sh0_ref, s0_ref, mu_ref, w0_ref, wdup_ref, a0_ref, waup_ref,
                wgup_ref, kk_ref, ka_ref, rk_ref, lg_ref, lb_ref,
                ya_ref, sho_ref, so_ref,
                carry_ref, s_ref, ar_ref, bk_ref, bkh_ref, v_ref, y_ref, pc_ref, t_ref, av_ref, wcat_ref,
                v32_ref, rk2_ref, gate_ref,
                *, heads, head_dim, ranks):
    tt = x_ref.shape[1]
    aw = heads * head_dim
    gw = WKV_GROUP_LANES
    hpg = gw // head_dim
    groups = aw // gw
    c_len = min(WKV_CHUNK, tt)
    n_chunks = tt // c_len
    dr, ir, gr = ranks
    t = pl.program_id(1)

    @pl.when(t == 0)
    def _init():
        carry_ref[...] = sh0_ref[0]
        s_ref[...] = jnp.zeros_like(s_ref)
        for h in range(heads):
            g, hh = divmod(h, hpg)
            lo = hh * head_dim
            s_ref[g, lo:lo + head_dim, lo:lo + head_dim] = s0_ref[0, h]

    hn = _rms(x_ref[0], gn_ref[...]).astype(BF16)
    lw = dr + ir + gr

    def project_shifted(cols):
        z = _dot(hn, wa_ref[:, cols])
        rolled = pltpu.roll(z, 1, 0)
        row8 = lax.broadcasted_iota(jnp.int32, (8, z.shape[1]), 0)
        prev = jnp.concatenate([jnp.where(row8 == 0, carry_ref[:, cols], rolled[0:8]), rolled[8:]],
                               axis=0)
        last = z[tt - 1:tt, :]
        carry_ref[:, cols] = last
        sho_ref[0, :, cols] = last
        return z + mu_ref[:, cols] * (prev - z)

    zl = project_shifted(slice(0, lw))
    wd = zl[:, 0:dr]
    ad = zl[:, dr:dr + ir]
    gd = zl[:, dr + ir:lw]
    xw = w0_ref[...] + _dot(jnp.tanh(wd).astype(BF16), wdup_ref[...])
    ld = (-math.exp(-0.5)) * jax.nn.sigmoid(xw)
    a_ic = jax.nn.sigmoid(a0_ref[...] + _dot(ad.astype(BF16), waup_ref[...]))
    gate_ref[...] = _dot(jax.nn.sigmoid(gd).astype(BF16), wgup_ref[...])

    gi = lax.broadcasted_iota(jnp.int32, (gw, gw), 0) // head_dim
    gj = lax.broadcasted_iota(jnp.int32, (gw, gw), 1) // head_dim
    same_head = gi == gj

    sw = min(aw, WKV_BLOCK_LANES)
    si = lax.broadcasted_iota(jnp.int32, (sw, sw), 0) // head_dim
    sj = lax.broadcasted_iota(jnp.int32, (sw, sw), 1) // head_dim
    ones_sw = (si == sj).astype(BF16)

    def head_sum(x):
        xb = x.astype(BF16)
        parts = [_dot(xb[:, q * sw:(q + 1) * sw], ones_sw) for q in range(aw // sw)]
        return jnp.concatenate(parts, axis=1) if len(parts) > 1 else parts[0]

    ti = lax.broadcasted_iota(jnp.int32, (tt, tt), 0)
    tj = lax.broadcasted_iota(jnp.int32, (tt, tt), 1)
    tri = (((ti // c_len) == (tj // c_len)) & (tj <= ti)).astype(BF16)
    ld_hi, ld_lo = _split_bf16(ld)
    cum_all = _dot(tri, ld_hi) + _dot(tri, ld_lo)

    for q in range(aw // sw):
        gl = slice(q * sw, (q + 1) * sw)
        z3 = project_shifted(slice(lw + 3 * q * sw, lw + 3 * (q + 1) * sw))
        r = z3[:, 0:sw]
        k = z3[:, sw:2 * sw]
        v = z3[:, 2 * sw:3 * sw]
        a_g = a_ic[:, gl]
        ld_g = ld[:, gl]
        kk = k * kk_ref[:, gl]
        k2 = k * (1.0 + (a_g - 1.0) * ka_ref[:, gl])
        kkn = kk * jnp.minimum(lax.rsqrt(_dot((kk * kk).astype(BF16), ones_sw)), 1e12)
        cum = cum_all[:, gl]
        p_inc = jnp.exp(cum)
        p_inv = jnp.exp(-cum)
        a_t = -(kkn * jnp.exp(cum - ld_g))
        r_t = r * p_inc
        b_t = (kkn * a_g) * p_inv
        k_t = k2 * p_inv
        for c in range(n_chunks):
            rs = slice(c * c_len, (c + 1) * c_len)
            p_end = p_inc[(c + 1) * c_len - 1:(c + 1) * c_len, :]
            ar_ref[c, 0:c_len, gl] = a_t[rs].astype(BF16)
            ar_ref[c, c_len:2 * c_len, gl] = r_t[rs].astype(BF16)
            bk_ref[c, 0:c_len, gl] = b_t[rs].astype(BF16)
            bk_ref[c, c_len:2 * c_len, gl] = k_t[rs].astype(BF16)
            bkh_ref[c, 0:c_len, gl] = (b_t[rs] * p_end).astype(BF16)
            bkh_ref[c, c_len:2 * c_len, gl] = (k_t[rs] * p_end).astype(BF16)
            pc_ref[c, :, gl] = p_end
        v_ref[:, gl] = v.astype(BF16)
        v32_ref[:, gl] = v
        rk2_ref[:, gl] = r * k2 * rk_ref[:, gl]

    lane_head = lax.broadcasted_iota(jnp.int32, (1, gw), 1) // head_dim
    ci = lax.broadcasted_iota(jnp.int32, (c_len, gw), 0)
    cj = lax.broadcasted_iota(jnp.int32, (c_len, gw), 1) % head_dim
    assert c_len == head_dim
    strict = cj < ci
    incl = cj <= ci
    eye = (cj == ci).astype(F32)
    n_doubling = int(math.log2(c_len))

    def bd(m):
        zero = jnp.zeros_like(m)
        return jnp.concatenate([jnp.where(lane_head == h, m, zero) for h in range(hpg)], axis=0)

    gs = range(groups)
    sls = [slice(g * gw, (g + 1) * gw) for g in gs]

    def prep_body(c, carry):
        r0 = pl.multiple_of(c * c_len, c_len)
        ar = [ar_ref[c, :, sl] for sl in sls]
        bk = [bk_ref[c, :, sl] for sl in sls]
        vv = [v_ref[pl.ds(r0, c_len), sl] for sl in sls]
        s4 = [_dot_nt(ar[g], jnp.concatenate([bd(bk[g][0:c_len]), bd(bk[g][c_len:2 * c_len])],
                                              axis=0)) for g in gs]
        a_ab = [jnp.where(strict, s4[g][0:c_len, 0:gw], 0.0) for g in gs]
        a_ak = [jnp.where(strict, s4[g][0:c_len, gw:2 * gw], 0.0).astype(BF16) for g in gs]
        for g in gs:
            wcat_ref[c, g, :, 0:gw] = jnp.where(incl, s4[g][c_len:2 * c_len, 0:gw], 0.0).astype(BF16)
            wcat_ref[c, g, :, gw:2 * gw] = jnp.where(incl, s4[g][c_len:2 * c_len, gw:2 * gw],
                                                     0.0).astype(BF16)
        av = [_dot(a_ak[g], bd(vv[g])) for g in gs]
        for g in gs:
            av_ref[c, g] = av[g]
        tk = [eye + a_ab[g] for g in gs]
        mb = [a_ab[g].astype(BF16) for g in gs]
        mk = [_dot(mb[g], bd(mb[g])) for g in gs]
        for step in range(1, n_doubling - 1):
            mb = [mk[g].astype(BF16) for g in gs]
            rr = [_dot(mb[g], jnp.concatenate([bd(mb[g]), bd(tk[g].astype(BF16))], axis=1))
                  for g in gs]
            mk = [rr[g][:, 0:gw] for g in gs]
            tk = [tk[g] + rr[g][:, gw:2 * gw] for g in gs]
        tk = [tk[g] + _dot(mk[g].astype(BF16), bd(tk[g].astype(BF16))) for g in gs]
        for g in gs:
            t_ref[c, g] = tk[g].astype(BF16)
        return carry

    lax.fori_loop(0, n_chunks, prep_body, 0)

    def chunk_body(c, carry):
        r0 = pl.multiple_of(c * c_len, c_len)
        ar = [ar_ref[c, :, sl] for sl in sls]
        vv = [v_ref[pl.ds(r0, c_len), sl] for sl in sls]
        s0 = [s_ref[g] for g in gs]
        m1 = [_dot_nt(ar[g], s0[g].astype(BF16)) for g in gs]
        xb = [(m1[g][0:c_len] + av_ref[c, g]).astype(BF16) for g in gs]
        ub = [_dot(t_ref[c, g], bd(xb[g])).astype(BF16) for g in gs]
        ds = [_dot_tn(jnp.concatenate([ub[g], vv[g]], axis=0), bkh_ref[c, :, sls[g]]) for g in gs]
        for g in gs:
            s_ref[g] = s0[g] * pc_ref[c][:, sls[g]] + jnp.where(same_head, ds[g], 0.0)
        yy = [_dot(wcat_ref[c, g], jnp.concatenate([bd(ub[g]), bd(vv[g])], axis=0)) for g in gs]
        for g in gs:
            y_ref[pl.ds(r0, c_len), sls[g]] = m1[g][c_len:2 * c_len] + yy[g]
        return carry

    lax.fori_loop(0, n_chunks, chunk_body, 0)

    y = y_ref[...]
    inv_n = 1.0 / head_dim
    mean = head_sum(y) * inv_n
    d = y - mean
    var = head_sum(d * d) * inv_n
    yn = d * lax.rsqrt(var + LNX_EPS) * lg_ref[...] + lb_ref[...]
    bonus = head_sum(rk2_ref[...]) * v32_ref[...]
    ya_ref[0] = ((yn + bonus) * gate_ref[...]).astype(ya_ref.dtype)

    @pl.when(t == pl.num_programs(1) - 1)
    def _fin():
        for h in range(heads):
            g, hh = divmod(h, hpg)
            lo = hh * head_dim
            so_ref[0, h] = s_ref[g, lo:lo + head_dim, lo:lo + head_dim]


def _group_major(a, aw):
    lead = a.shape[:-1]
    bw = min(aw, WKV_BLOCK_LANES)
    g = aw // bw
    rkv = a[..., :3 * aw].reshape(*lead, 3, g, bw)
    rkv = jnp.swapaxes(rkv, -3, -2).reshape(*lead, 3 * aw)
    return jnp.concatenate([a[..., 3 * aw:], rkv], axis=-1)


def _group_major_inverse(a, aw):
    lead = a.shape[:-1]
    bw = min(aw, WKV_BLOCK_LANES)
    g = aw // bw
    lw = a.shape[-1] - 3 * aw
    rkv = a[..., lw:].reshape(*lead, g, 3, bw)
    rkv = jnp.swapaxes(rkv, -3, -2).reshape(*lead, 3 * aw)
    return jnp.concatenate([rkv, a[..., :lw]], axis=-1)


def _wkv_branch(x, gn, wa, shift0, s0, mu, w0, wdup, a0, waup, wgup, k_k, k_a, r_k, lnx_g, lnx_b):
    b, t, d = x.shape
    asw = wa.shape[1]
    heads, head_dim = s0.shape[1], s0.shape[2]
    aw = heads * head_dim
    ranks = (wdup.shape[0], waup.shape[0], wgup.shape[0])
    tt = _tile(t, 256, WKV_CHUNK)
    c_len = min(WKV_CHUNK, tt)
    n_chunks = tt // c_len
    groups = aw // WKV_GROUP_LANES
    row = lambda a: a.reshape(1, -1)
    const = lambda shape: pl.BlockSpec(shape, lambda i, j: (0,) * len(shape))
    kern = functools.partial(_wkv_kernel, heads=heads, head_dim=head_dim, ranks=ranks)
    return pl.pallas_call(
        kern,
        grid=(b, t // tt),
        in_specs=[
            pl.BlockSpec((1, tt, d), lambda i, j: (i, j, 0)),
            const((1, d)),
            pl.BlockSpec(wa.shape, lambda i, j: (0, 0), pipeline_mode=pl.Buffered(1)),
            pl.BlockSpec((1, 1, asw), lambda i, j: (i, 0, 0)),
            pl.BlockSpec((1, heads, head_dim, head_dim), lambda i, j: (i, 0, 0, 0)),
            const((1, asw)), const((1, aw)), const(wdup.shape), const((1, aw)), const(waup.shape),
            const(wgup.shape), const((1, aw)), const((1, aw)), const((1, aw)), const((1, aw)),
            const((1, aw)),
        ],
        out_specs=[
            pl.BlockSpec((1, tt, aw), lambda i, j: (i, j, 0)),
            pl.BlockSpec((1, 1, asw), lambda i, j: (i, 0, 0)),
            pl.BlockSpec((1, heads, head_dim, head_dim), lambda i, j: (i, 0, 0, 0)),
        ],
        out_shape=[
            jax.ShapeDtypeStruct((b, t, aw), BF16),
            jax.ShapeDtypeStruct((b, 1, asw), F32),
            jax.ShapeDtypeStruct((b, heads, head_dim, head_dim), F32),
        ],
        scratch_shapes=[
            pltpu.VMEM((1, asw), F32),
            pltpu.VMEM((groups, WKV_GROUP_LANES, WKV_GROUP_LANES), F32),
            pltpu.VMEM((n_chunks, 2 * c_len, aw), BF16),
            pltpu.VMEM((n_chunks, 2 * c_len, aw), BF16),
            pltpu.VMEM((n_chunks, 2 * c_len, aw), BF16),
            pltpu.VMEM((tt, aw), BF16),
            pltpu.VMEM((tt, aw), F32),
            pltpu.VMEM((n_chunks, 1, aw), F32),
            pltpu.VMEM((n_chunks, groups, c_len, WKV_GROUP_LANES), BF16),
            pltpu.VMEM((n_chunks, groups, c_len, WKV_GROUP_LANES), F32),
            pltpu.VMEM((n_chunks, groups, c_len, 2 * WKV_GROUP_LANES), BF16),
            pltpu.VMEM((tt, aw), F32),
            pltpu.VMEM((tt, aw), F32),
            pltpu.VMEM((tt, aw), F32),
        ],
        compiler_params=_params("arbitrary", "arbitrary"),
        name="wkv7_chunked",
    )(x, gn, wa, shift0.reshape(b, 1, asw), s0, row(mu), row(w0), wdup, row(a0), waup, wgup,
      row(k_k), row(k_a), row(r_k), row(lnx_g), row(lnx_b))


def _ret_proj_kernel(x_ref, g_ref, w_ref, cos_ref, sin_ref, r0_ref, gn_ref, yb_ref, ro_ref,
                     st_ref, dm_ref, qd_ref, kd_ref, *, heads, dk, dv, blk, steps_per_seq):
    tm = x_ref.shape[0]
    hw = 2 * dk + 2 * dv
    i = pl.program_id(0)
    log_gammas = [math.log1p(-(2.0 ** (-5.0 - h))) for h in range(heads)]

    @pl.when(i == 0)
    def _tables():
        diff = (lax.broadcasted_iota(jnp.int32, (blk, blk), 0)
                - lax.broadcasted_iota(jnp.int32, (blk, blk), 1)).astype(F32)
        pos = lax.broadcasted_iota(jnp.int32, (blk, dk), 0).astype(F32)
        for h in range(heads):
            lg = log_gammas[h]
            dm_ref[h] = jnp.where(diff >= 0, jnp.exp(jnp.maximum(diff, 0.0) * lg), 0.0)
            qd_ref[h] = jnp.exp((pos + 1.0) * lg) * (dk ** -0.5)
            kd_ref[h] = jnp.exp((blk - 1.0 - pos) * lg)

    @pl.when(i % steps_per_seq == 0)
    def _init():
        st_ref[...] = r0_ref[0]

    hn = _rms(x_ref[...], g_ref[...]).astype(BF16)
    for h in range(heads):
        zh = _dot(hn, w_ref[:, h * hw:(h + 1) * hw])
        for s in range(tm // blk):
            rows = slice(s * blk, (s + 1) * blk)
            cosv = cos_ref[rows, :]
            sinv = sin_ref[rows, :]
            qh = zh[rows, 0:dk]
            kh = zh[rows, dk:2 * dk]
            vh = zh[rows, 2 * dk:2 * dk + dv].astype(BF16)
            gate = zh[rows, 2 * dk + dv:hw]
            qrot = qh * cosv + pltpu.roll(qh, dk // 2, 1) * sinv
            kr = kh * cosv + pltpu.roll(kh, dk // 2, 1) * sinv
            scores = _dot_nt((qrot * (dk ** -0.5)).astype(BF16), kr.astype(BF16)) * dm_ref[h]
            inner = _dot(scores.astype(BF16), vh)
            st = st_ref[h]
            cross = _dot((qrot * qd_ref[h]).astype(BF16), st.astype(BF16))
            ke = kr * kd_ref[h]
            st_ref[h] = math.exp(blk * log_gammas[h]) * st + _dot_tn(ke.astype(BF16), vh)
            o = inner + cross
            mean = jnp.mean(o, axis=-1, keepdims=True)
            d = o - mean
            var = jnp.mean(d * d, axis=-1, keepdims=True)
            on = d * lax.rsqrt(var + GN_EPS) * gn_ref[:, h * dv:(h + 1) * dv]
            yb_ref[rows, h * dv:(h + 1) * dv] = (jax.nn.silu(gate) * on).astype(yb_ref.dtype)

    @pl.when(i % steps_per_seq == steps_per_seq - 1)
    def _fin():
        ro_ref[0] = st_ref[...]


def _ret_proj(x, g, w_heads, cos_t, sin_t, r0, gn_g, seq_len):
    m, d = x.shape
    heads, dk, dv = r0.shape[1], r0.shape[2], r0.shape[3]
    vw = heads * dv
    tm = _tile(seq_len, 512, 64)
    blk = _tile(tm, 256, 64)
    sps = seq_len // tm
    kern = functools.partial(_ret_proj_kernel, heads=heads, dk=dk, dv=dv, blk=blk, steps_per_seq=sps)
    return pl.pallas_call(
        kern,
        grid=(m // tm,),
        in_specs=[
            pl.BlockSpec((tm, d), lambda i: (i, 0)),
            pl.BlockSpec((1, d), lambda i: (0, 0)),
            pl.BlockSpec(w_heads.shape, lambda i: (0, 0), pipeline_mode=pl.Buffered(1)),
            pl.BlockSpec((tm, dk), lambda i: (i % sps, 0)),
            pl.BlockSpec((tm, dk), lambda i: (i % sps, 0)),
            pl.BlockSpec((1, heads, dk, dv), lambda i: (i // sps, 0, 0, 0)),
            pl.BlockSpec((1, vw), lambda i: (0, 0)),
        ],
        out_specs=[
            pl.BlockSpec((tm, vw), lambda i: (i, 0)),
            pl.BlockSpec((1, heads, dk, dv), lambda i: (i // sps, 0, 0, 0)),
        ],
        out_shape=[
            jax.ShapeDtypeStruct((m, vw), BF16),
            jax.ShapeDtypeStruct((m // seq_len, heads, dk, dv), F32),
        ],
        scratch_shapes=[pltpu.VMEM((heads, dk, dv), F32), pltpu.VMEM((heads, blk, blk), F32),
                        pltpu.VMEM((heads, blk, dk), F32), pltpu.VMEM((heads, blk, dk), F32)],
        compiler_params=_params("arbitrary", vmem_limit=VMEM_LIMIT_RESIDENT_BYTES),
        name="retention_proj",
    )(x, g, w_heads, cos_t, sin_t, r0, gn_g.reshape(1, -1))


def _merge_kernel(ya_ref, yb_ref, zg_ref, x_ref, wba_ref, wbb_ref, wo_ref, g_ref, o_ref):
    d = x_ref.shape[1]
    ga = jax.nn.sigmoid(zg_ref[:, 0:d].astype(F32))
    gb = jax.nn.sigmoid(zg_ref[:, d:2 * d].astype(F32))
    merged = ga * _dot(ya_ref[...], wba_ref[...]) + gb * _dot(yb_ref[...], wbb_ref[...])
    mo = _dot(merged.astype(BF16), wo_ref[...])
    o_ref[...] = x_ref[...] + _rms(mo, g_ref[...])


def _merge(ya, yb, zg, x, wba, wbb, wo, g):
    m, d = x.shape
    tm = _tile(m, 256, 8)
    tok = lambda w: pl.BlockSpec((tm, w), lambda i: (i, 0))
    res = lambda a: pl.BlockSpec(a.shape, lambda i: (0, 0), pipeline_mode=pl.Buffered(1))
    return pl.pallas_call(
        _merge_kernel,
        grid=(m // tm,),
        in_specs=[tok(ya.shape[1]), tok(yb.shape[1]), tok(zg.shape[1]), tok(d),
                  res(wba), res(wbb), res(wo), pl.BlockSpec((1, d), lambda i: (0, 0))],
        out_specs=tok(d),
        out_shape=jax.ShapeDtypeStruct((m, d), F32),
        compiler_params=_params("arbitrary"),
        name="merge_out",
    )(ya, yb, zg, x, wba, wbb, wo, g)


def _ffn_kernel(x_ref, gpre_ref, wg_ref, wu_ref, wo_ref, gpost_ref, o_ref, h_ref):
    j = pl.program_id(1)

    @pl.when(j == 0)
    def _():
        h_ref[...] = _rms(x_ref[...], gpre_ref[...]).astype(BF16)
        o_ref[...] = jnp.zeros_like(o_ref)

    h = h_ref[...]
    act = jax.nn.silu(_dot(h, wg_ref[...])) * _dot(h, wu_ref[...])
    o_ref[...] += _dot(act.astype(BF16), wo_ref[...])

    @pl.when(j == pl.num_programs(1) - 1)
    def _():
        o_ref[...] = x_ref[...] + _rms(o_ref[...], gpost_ref[...])


def _ffn(x, gpre, w_fi, w_fo, gpost):
    m, d = x.shape
    f = w_fo.shape[0]
    tm = _tile(m, 512, 8)
    tf = _tile(f, 512)
    nf = f // tf
    return pl.pallas_call(
        _ffn_kernel,
        grid=(m // tm, nf),
        in_specs=[
            pl.BlockSpec((tm, d), lambda i, j: (i, 0)),
            pl.BlockSpec((1, d), lambda i, j: (0, 0)),
            pl.BlockSpec((d, tf), lambda i, j: (0, j)),
            pl.BlockSpec((d, tf), lambda i, j: (0, j + nf)),
            pl.BlockSpec((tf, d), lambda i, j: (j, 0)),
            pl.BlockSpec((1, d), lambda i, j: (0, 0)),
        ],
        out_specs=pl.BlockSpec((tm, d), lambda i, j: (i, 0)),
        out_shape=jax.ShapeDtypeStruct((m, d), F32),
        scratch_shapes=[pltpu.VMEM((tm, d), BF16)],
        compiler_params=_params("arbitrary", "arbitrary"),
        name="swiglu_ffn",
    )(x, gpre, w_fi, w_fi, w_fo, gpost)


def _rope_tables(pos, dk):
    half = dk // 2
    inv = ROPE_BASE ** (-jnp.arange(half, dtype=F32) / half)
    ang = pos[:, None] * inv[None, :]
    cos = jnp.cos(ang)
    sin = jnp.sin(ang)
    return jnp.concatenate([cos, cos], axis=-1), jnp.concatenate([-sin, sin], axis=-1)


def _head_major(w_b, heads, dk, dv):
    d = w_b.shape[0]
    qkw, vw = heads * dk, heads * dv
    parts = [w_b[:, 0:qkw].reshape(d, heads, dk), w_b[:, qkw:2 * qkw].reshape(d, heads, dk),
             w_b[:, 2 * qkw:2 * qkw + vw].reshape(d, heads, dv),
             w_b[:, 2 * qkw + vw:].reshape(d, heads, dv)]
    return jnp.concatenate(parts, axis=2).reshape(d, heads * (2 * dk + 2 * dv))


def _layer(x, rope, shift0, s0, r0, w):
    b, t, d = x.shape
    x2 = x.reshape(b * t, d)
    row = lambda a: a.reshape(1, -1)
    zg = _norm_matmul(x2, row(w["nmp"]), w["w_g"], BF16)
    aw = w["w_dup"].shape[1]
    ya, shift, s_new = _wkv_branch(x, row(w["nmp"]), w["w_a"], _group_major(shift0, aw), s0, w["mu"],
                                   w["w0"], w["w_dup"], w["a0"], w["w_aup"], w["w_gup"], w["k_k"],
                                   w["k_a"], w["r_k"], w["lnx_g"], w["lnx_b"])
    shift = _group_major_inverse(shift.reshape(b, -1), aw)
    yb, r_new = _ret_proj(x2, row(w["nmp"]), w["w_b"], rope[0], rope[1], r0, w["ret_g"], t)
    x1 = _merge(ya.reshape(b * t, -1), yb, zg, x2, w["w_ba"], w["w_bb"],
                w["w_out"], row(w["nmq"]))
    xo = _ffn(x1, row(w["nfp"]), w["w_fi"], w["w_fo"], row(w["nfq"]))
    return xo.reshape(b, t, d), shift, s_new, r_new


def kernel(x_prompt, x_sample, state_wkv, state_ret, state_shift, norm_mix_pre, norm_mix_post, norm_ffn_pre, norm_ffn_post, w_in, shift_mu, decay_base, w_decay_up, iclr_base, w_iclr_up, w_gate_up, key_k, key_a, bonus_rk, lnx_gain, lnx_bias, w_branch_a, ret_norm_gain, w_branch_b, w_out, w_ffn_in, w_ffn_out):
    depth = w_in.shape[0]
    bp, tp, _ = x_prompt.shape
    bs, ts, _ = x_sample.shape
    asw = shift_mu.shape[1]
    heads_a, hd_a = bonus_rk.shape[1], bonus_rk.shape[2]
    heads_b, dk, dv = state_ret.shape[2], state_ret.shape[3], state_ret.shape[4]
    b_in = 2 * heads_b * dk + 2 * heads_b * dv
    dt = x_prompt.dtype

    rope_p = _rope_tables(jnp.arange(tp, dtype=F32), dk)
    rope_s = _rope_tables(jnp.arange(ts, dtype=F32) + jnp.float32(PAST_LEN), dk)
    zero_shift = jnp.zeros((bp, asw), dt)
    zero_wkv = jnp.zeros((bp, heads_a, hd_a, hd_a), dt)
    zero_ret = jnp.zeros((bp, heads_b, dk, dv), dt)

    xp, xs = x_prompt, x_sample
    outs = [[] for _ in range(6)]
    for l in range(depth):
        w = dict(
            nmp=norm_mix_pre[l], nmq=norm_mix_post[l], nfp=norm_ffn_pre[l], nfq=norm_ffn_post[l],
            w_a=_group_major(w_in[l, :, :asw], heads_a * hd_a).astype(BF16),
            w_b=_head_major(w_in[l, :, asw:asw + b_in], heads_b, dk, dv).astype(BF16),
            w_g=w_in[l, :, asw + b_in:].astype(BF16),
            mu=_group_major(shift_mu[l], heads_a * hd_a), w0=decay_base[l], w_dup=w_decay_up[l].astype(BF16), a0=iclr_base[l],
            w_aup=w_iclr_up[l].astype(BF16), w_gup=w_gate_up[l].astype(BF16), k_k=key_k[l],
            k_a=key_a[l], r_k=bonus_rk[l], lnx_g=lnx_gain[l], lnx_b=lnx_bias[l],
            w_ba=w_branch_a[l].astype(BF16), ret_g=ret_norm_gain[l],
            w_bb=w_branch_b[l].astype(BF16), w_out=w_out[l].astype(BF16),
            w_fi=w_ffn_in[l].astype(BF16), w_fo=w_ffn_out[l].astype(BF16),
        )
        xp, sp, wkp, rtp = _layer(xp, rope_p, zero_shift, zero_wkv, zero_ret, w)
        xs, ss, wks, rts = _layer(xs, rope_s, state_shift[l], state_wkv[l], state_ret[l], w)
        for lst, val in zip(outs, (wkp, rtp, sp, wks, rts, ss)):
            lst.append(val)
    return (xp, xs) + tuple(jnp.stack(o) for o in outs)
```

```python
import functools
import math

import jax
import jax.numpy as jnp
from jax import lax
from jax.experimental import pallas as pl
from jax.experimental.pallas import tpu as pltpu

F32 = jnp.float32
BF16 = jnp.bfloat16

PAST_LEN = 4096
RMS_EPS = 1e-6
LNX_EPS = 64e-5
GN_EPS = 1e-5
ROPE_BASE = 10000.0

V7X_LANES = 128
V7X_MXU_DIM = 256
WKV_GROUP_LANES = V7X_LANES
WKV_BLOCK_LANES = V7X_MXU_DIM
WKV_CHUNK = 64
V7X_VMEM_BYTES = 64 * 1024 * 1024
VMEM_LIMIT_BYTES = V7X_VMEM_BYTES * 7 // 8
VMEM_LIMIT_RESIDENT_BYTES = V7X_VMEM_BYTES * 15 // 16


def _dot(a, b):
    return jnp.dot(a, b, preferred_element_type=F32)


def _dot_nt(a, b):
    return lax.dot_general(a, b, (((1,), (1,)), ((), ())), preferred_element_type=F32)


def _dot_tn(a, b):
    return lax.dot_general(a, b, (((0,), (0,)), ((), ())), preferred_element_type=F32)


def _rms(x, g):
    return x * lax.rsqrt(jnp.mean(x * x, axis=-1, keepdims=True) + RMS_EPS) * g


def _tile(n, target, mult=V7X_LANES):
    if n <= target:
        return n
    best = None
    for t in range(mult, target + 1, mult):
        if n % t == 0:
            best = t
    assert best is not None, (n, target, mult)
    return best


def _params(*sem, vmem_limit=VMEM_LIMIT_BYTES):
    return pltpu.CompilerParams(dimension_semantics=sem, vmem_limit_bytes=vmem_limit)


def _norm_matmul_kernel(x_ref, g_ref, w_ref, o_ref, *, tn):
    h = _rms(x_ref[...], g_ref[...]).astype(BF16)
    for j in range(w_ref.shape[1] // tn):
        cols = slice(j * tn, (j + 1) * tn)
        o_ref[:, cols] = _dot(h, w_ref[:, cols]).astype(o_ref.dtype)


def _norm_matmul(x, g, w, out_dtype, tm_target=512, tn_target=512):
    m, d = x.shape
    n = w.shape[1]
    tm = _tile(m, tm_target, 8)
    tn = _tile(n, tn_target)
    return pl.pallas_call(
        functools.partial(_norm_matmul_kernel, tn=tn),
        grid=(m // tm,),
        in_specs=[
            pl.BlockSpec((tm, d), lambda i: (i, 0)),
            pl.BlockSpec((1, d), lambda i: (0, 0)),
            pl.BlockSpec((d, n), lambda i: (0, 0), pipeline_mode=pl.Buffered(1)),
        ],
        out_specs=pl.BlockSpec((tm, n), lambda i: (i, 0)),
        out_shape=jax.ShapeDtypeStruct((m, n), out_dtype),
        compiler_params=_params("arbitrary"),
        name="norm_matmul",
    )(x, g, w)


def _split_bf16(x):
    hi = x.astype(BF16)
    lo = (x - hi.astype(F32)).astype(BF16)
    return hi, lo


def _wkv_kernel(x_ref, gn_ref, wa_ref, sh0_ref, s0_ref, mu_ref, w0_ref, wdup_ref, a0_ref, waup_ref,
                wgup_ref, kk_ref, ka_ref, rk_ref, lg_ref, lb_ref,
                ya_ref, sho_ref, so_ref,
                carry_ref, s_ref, ar_ref, bk_ref, bkh_ref, v_ref, y_ref, pc_ref, t_ref, av_ref, wcat_ref,
                v32_ref, rk2_ref, gate_ref,
                *, heads, head_dim, ranks):
    tt = x_ref.shape[1]
    aw = heads * head_dim
    gw = WKV_GROUP_LANES
    hpg = gw // head_dim
    groups = aw // gw
    c_len = min(WKV_CHUNK, tt)
    n_chunks = tt // c_len
    dr, ir, gr = ranks
    t = pl.program_id(1)

    @pl.when(t == 0)
    def _init():
        carry_ref[...] = sh0_ref[0]
        s_ref[...] = jnp.zeros_like(s_ref)
        for h in range(heads):
            g, hh = divmod(h, hpg)
            lo = hh * head_dim
            s_ref[g, lo:lo + head_dim, lo:lo + head_dim] = s0_ref[0, h]

    hn = _rms(x_ref[0], gn_ref[...]).astype(BF16)
    lw = dr + ir + gr

    def project_shifted(cols):
        z = _dot(hn, wa_ref[:, cols])
        rolled = pltpu.roll(z, 1, 0)
        row8 = lax.broadcasted_iota(jnp.int32, (8, z.shape[1]), 0)
        prev = jnp.concatenate([jnp.where(row8 == 0, carry_ref[:, cols], rolled[0:8]), rolled[8:]],
                               axis=0)
        last = z[tt - 1:tt, :]
        carry_ref[:, cols] = last
        sho_ref[0, :, cols] = last
        return z + mu_ref[:, cols] * (prev - z)

    zl = project_shifted(slice(0, lw))
    wd = zl[:, 0:dr]
    ad = zl[:, dr:dr + ir]
    gd = zl[:, dr + ir:lw]
    xw = w0_ref[...] + _dot(jnp.tanh(wd).astype(BF16), wdup_ref[...])
    ld = (-math.exp(-0.5)) * jax.nn.sigmoid(xw)
    a_ic = jax.nn.sigmoid(a0_ref[...] + _dot(ad.astype(BF16), waup_ref[...]))
    gate_ref[...] = _dot(jax.nn.sigmoid(gd).astype(BF16), wgup_ref[...])

    gi = lax.broadcasted_iota(jnp.int32, (gw, gw), 0) // head_dim
    gj = lax.broadcasted_iota(jnp.int32, (gw, gw), 1) // head_dim
    same_head = gi == gj

    sw = min(aw, WKV_BLOCK_LANES)
    si = lax.broadcasted_iota(jnp.int32, (sw, sw), 0) // head_dim
    sj = lax.broadcasted_iota(jnp.int32, (sw, sw), 1) // head_dim
    ones_sw = (si == sj).astype(BF16)

    def head_sum(x):
        xb = x.astype(BF16)
        parts = [_dot(xb[:, q * sw:(q + 1) * sw], ones_sw) for q in range(aw // sw)]
        return jnp.concatenate(parts, axis=1) if len(parts) > 1 else parts[0]

    ti = lax.broadcasted_iota(jnp.int32, (tt, tt), 0)
    tj = lax.broadcasted_iota(jnp.int32, (tt, tt), 1)
    tri = (((ti // c_len) == (tj // c_len)) & (tj <= ti)).astype(BF16)
    ld_hi, ld_lo = _split_bf16(ld)
    cum_all = _dot(tri, ld_hi) + _dot(tri, ld_lo)

    for q in range(aw // sw):
        gl = slice(q * sw, (q + 1) * sw)
        z3 = project_shifted(slice(lw + 3 * q * sw, lw + 3 * (q + 1) * sw))
        r = z3[:, 0:sw]
        k = z3[:, sw:2 * sw]
        v = z3[:, 2 * sw:3 * sw]
        a_g = a_ic[:, gl]
        ld_g = ld[:, gl]
        kk = k * kk_ref[:, gl]
        k2 = k * (1.0 + (a_g - 1.0) * ka_ref[:, gl])
        kkn = kk * jnp.minimum(lax.rsqrt(_dot((kk * kk).astype(BF16), ones_sw)), 1e12)
        cum = cum_all[:, gl]
        p_inc = jnp.exp(cum)
        p_inv = jnp.exp(-cum)
        a_t = -(kkn * jnp.exp(cum - ld_g))
        r_t = r * p_inc
        b_t = (kkn * a_g) * p_inv
        k_t = k2 * p_inv
        for c in range(n_chunks):
            rs = slice(c * c_len, (c + 1) * c_len)
            p_end = p_inc[(c + 1) * c_len - 1:(c + 1) * c_len, :]
            ar_ref[c, 0:c_len, gl] = a_t[rs].astype(BF16)
            ar_ref[c, c_len:2 * c_len, gl] = r_t[rs].astype(BF16)
            bk_ref[c, 0:c_len, gl] = b_t[rs].astype(BF16)
            bk_ref[c, c_len:2 * c_len, gl] = k_t[rs].astype(BF16)
            bkh_ref[c, 0:c_len, gl] = (b_t[rs] * p_end).astype(BF16)
            bkh_ref[c, c_len:2 * c_len, gl] = (k_t[rs] * p_end).astype(BF16)
            pc_ref[c, :, gl] = p_end
        v_ref[:, gl] = v.astype(BF16)
        v32_ref[:, gl] = v
        rk2_ref[:, gl] = r * k2 * rk_ref[:, gl]

    lane_head = lax.broadcasted_iota(jnp.int32, (1, gw), 1) // head_dim
    ci = lax.broadcasted_iota(jnp.int32, (c_len, gw), 0)
    cj = lax.broadcasted_iota(jnp.int32, (c_len, gw), 1) % head_dim
    assert c_len == head_dim
    strict = cj < ci
    incl = cj <= ci
    eye = (cj == ci).astype(F32)
    n_doubling = int(math.log2(c_len))

    def bd(m):
        zero = jnp.zeros_like(m)
        return jnp.concatenate([jnp.where(lane_head == h, m, zero) for h in range(hpg)], axis=0)

    gs = range(groups)
    sls = [slice(g * gw, (g + 1) * gw) for g in gs]

    def prep_body(c, carry):
        r0 = pl.multiple_of(c * c_len, c_len)
        ar = [ar_ref[c, :, sl] for sl in sls]
        bk = [bk_ref[c, :, sl] for sl in sls]
        vv = [v_ref[pl.ds(r0, c_len), sl] for sl in sls]
        s4 = [_dot_nt(ar[g], jnp.concatenate([bd(bk[g][0:c_len]), bd(bk[g][c_len:2 * c_len])],
                                              axis=0)) for g in gs]
        a_ab = [jnp.where(strict, s4[g][0:c_len, 0:gw], 0.0) for g in gs]
        a_ak = [jnp.where(strict, s4[g][0:c_len, gw:2 * gw], 0.0).astype(BF16) for g in gs]
        for g in gs:
            wcat_ref[c, g, :, 0:gw] = jnp.where(incl, s4[g][c_len:2 * c_len, 0:gw], 0.0).astype(BF16)
            wcat_ref[c, g, :, gw:2 * gw] = jnp.where(incl, s4[g][c_len:2 * c_len, gw:2 * gw],
                                                     0.0).astype(BF16)
        av = [_dot(a_ak[g], bd(vv[g])) for g in gs]
        for g in gs:
            av_ref[c, g] = av[g]
        tk = [eye + a_ab[g] for g in gs]
        mb = [a_ab[g].astype(BF16) for g in gs]
        mk = [_dot(mb[g], bd(mb[g])) for g in gs]
        for step in range(1, n_doubling - 1):
            mb = [mk[g].astype(BF16) for g in gs]
            rr = [_dot(mb[g], jnp.concatenate([bd(mb[g]), bd(tk[g].astype(BF16))], axis=1))
                  for g in gs]
            mk = [rr[g][:, 0:gw] for g in gs]
            tk = [tk[g] + rr[g][:, gw:2 * gw] for g in gs]
        tk = [tk[g] + _dot(mk[g].astype(BF16), bd(tk[g].astype(BF16))) for g in gs]
        for g in gs:
            t_ref[c, g] = tk[g].astype(BF16)
        return carry

    lax.fori_loop(0, n_chunks, prep_body, 0)

    def chunk_body(c, carry):
        r0 = pl.multiple_of(c * c_len, c_len)
        ar = [ar_ref[c, :, sl] for sl in sls]
        vv = [v_ref[pl.ds(r0, c_len), sl] for sl in sls]
        s0 = [s_ref[g] for g in gs]
        m1 = [_dot_nt(ar[g], s0[g].astype(BF16)) for g in gs]
        xb = [(m1[g][0:c_len] + av_ref[c, g]).astype(BF16) for g in gs]
        ub = [_dot(t_ref[c, g], bd(xb[g])).astype(BF16) for g in gs]
        ds = [_dot_tn(jnp.concatenate([ub[g], vv[g]], axis=0), bkh_ref[c, :, sls[g]]) for g in gs]
        for g in gs:
            s_ref[g] = s0[g] * pc_ref[c][:, sls[g]] + jnp.where(same_head, ds[g], 0.0)
        yy = [_dot(wcat_ref[c, g], jnp.concatenate([bd(ub[g]), bd(vv[g])], axis=0)) for g in gs]
        for g in gs:
            y_ref[pl.ds(r0, c_len), sls[g]] = m1[g][c_len:2 * c_len] + yy[g]
        return carry

    lax.fori_loop(0, n_chunks, chunk_body, 0)

    y = y_ref[...]
    inv_n = 1.0 / head_dim
    mean = head_sum(y) * inv_n
    d = y - mean
    var = head_sum(d * d) * inv_n
    yn = d * lax.rsqrt(var + LNX_EPS) * lg_ref[...] + lb_ref[...]
    bonus = head_sum(rk2_ref[...]) * v32_ref[...]
    ya_ref[0] = ((yn + bonus) * gate_ref[...]).astype(ya_ref.dtype)

    @pl.when(t == pl.num_programs(1) - 1)
    def _fin():
        for h in range(heads):
            g, hh = divmod(h, hpg)
            lo = hh * head_dim
            so_ref[0, h] = s_ref[g, lo:lo + head_dim, lo:lo + head_dim]


def _group_major(a, aw):
    lead = a.shape[:-1]
    bw = min(aw, WKV_BLOCK_LANES)
    g = aw // bw
    rkv = a[..., :3 * aw].reshape(*lead, 3, g, bw)
    rkv = jnp.swapaxes(rkv, -3, -2).reshape(*lead, 3 * aw)
    return jnp.concatenate([a[..., 3 * aw:], rkv], axis=-1)


def _group_major_inverse(a, aw):
    lead = a.shape[:-1]
    bw = min(aw, WKV_BLOCK_LANES)
    g = aw // bw
    lw = a.shape[-1] - 3 * aw
    rkv = a[..., lw:].reshape(*lead, g, 3, bw)
    rkv = jnp.swapaxes(rkv, -3, -2).reshape(*lead, 3 * aw)
    return jnp.concatenate([rkv, a[..., :lw]], axis=-1)


def _wkv_branch(x, gn, wa, shift0, s0, mu, w0, wdup, a0, waup, wgup, k_k, k_a, r_k, lnx_g, lnx_b):
    b, t, d = x.shape
    asw = wa.shape[1]
    heads, head_dim = s0.shape[1], s0.shape[2]
    aw = heads * head_dim
    ranks = (wdup.shape[0], waup.shape[0], wgup.shape[0])
    tt = _tile(t, 256, WKV_CHUNK)
    c_len = min(WKV_CHUNK, tt)
    n_chunks = tt // c_len
    groups = aw // WKV_GROUP_LANES
    row = lambda a: a.reshape(1, -1)
    const = lambda shape: pl.BlockSpec(shape, lambda i, j: (0,) * len(shape))
    kern = functools.partial(_wkv_kernel, heads=heads, head_dim=head_dim, ranks=ranks)
    return pl.pallas_call(
        kern,
        grid=(b, t // tt),
        in_specs=[
            pl.BlockSpec((1, tt, d), lambda i, j: (i, j, 0)),
            const((1, d)),
            pl.BlockSpec(wa.shape, lambda i, j: (0, 0), pipeline_mode=pl.Buffered(1)),
            pl.BlockSpec((1, 1, asw), lambda i, j: (i, 0, 0)),
            pl.BlockSpec((1, heads, head_dim, head_dim), lambda i, j: (i, 0, 0, 0)),
            const((1, asw)), const((1, aw)), const(wdup.shape), const((1, aw)), const(waup.shape),
            const(wgup.shape), const((1, aw)), const((1, aw)), const((1, aw)), const((1, aw)),
            const((1, aw)),
        ],
        out_specs=[
            pl.BlockSpec((1, tt, aw), lambda i, j: (i, j, 0)),
            pl.BlockSpec((1, 1, asw), lambda i, j: (i, 0, 0)),
            pl.BlockSpec((1, heads, head_dim, head_dim), lambda i, j: (i, 0, 0, 0)),
        ],
        out_shape=[
            jax.ShapeDtypeStruct((b, t, aw), BF16),
            jax.ShapeDtypeStruct((b, 1, asw), F32),
            jax.ShapeDtypeStruct((b, heads, head_dim, head_dim), F32),
        ],
        scratch_shapes=[
            pltpu.VMEM((1, asw), F32),
            pltpu.VMEM((groups, WKV_GROUP_LANES, WKV_GROUP_LANES), F32),
            pltpu.VMEM((n_chunks, 2 * c_len, aw), BF16),
            pltpu.VMEM((n_chunks, 2 * c_len, aw), BF16),
            pltpu.VMEM((n_chunks, 2 * c_len, aw), BF16),
            pltpu.VMEM((tt, aw), BF16),
            pltpu.VMEM((tt, aw), F32),
            pltpu.VMEM((n_chunks, 1, aw), F32),
            pltpu.VMEM((n_chunks, groups, c_len, WKV_GROUP_LANES), BF16),
            pltpu.VMEM((n_chunks, groups, c_len, WKV_GROUP_LANES), F32),
            pltpu.VMEM((n_chunks, groups, c_len, 2 * WKV_GROUP_LANES), BF16),
            pltpu.VMEM((tt, aw), F32),
            pltpu.VMEM((tt, aw), F32),
            pltpu.VMEM((tt, aw), F32),
        ],
        compiler_params=_params("arbitrary", "arbitrary"),
        name="wkv7_chunked",
    )(x, gn, wa, shift0.reshape(b, 1, asw), s0, row(mu), row(w0), wdup, row(a0), waup, wgup,
      row(k_k), row(k_a), row(r_k), row(lnx_g), row(lnx_b))


def _ret_proj_kernel(x_ref, g_ref, w_ref, cos_ref, sin_ref, r0_ref, gn_ref, yb_ref, ro_ref,
                     st_ref, dm_ref, qd_ref, kd_ref, *, heads, dk, dv, blk, steps_per_seq):
    tm = x_ref.shape[0]
    hw = 2 * dk + 2 * dv
    i = pl.program_id(0)
    log_gammas = [math.log1p(-(2.0 ** (-5.0 - h))) for h in range(heads)]

    @pl.when(i == 0)
    def _tables():
        diff = (lax.broadcasted_iota(jnp.int32, (blk, blk), 0)
                - lax.broadcasted_iota(jnp.int32, (blk, blk), 1)).astype(F32)
        pos = lax.broadcasted_iota(jnp.int32, (blk, dk), 0).astype(F32)
        for h in range(heads):
            lg = log_gammas[h]
            dm_ref[h] = jnp.where(diff >= 0, jnp.exp(jnp.maximum(diff, 0.0) * lg), 0.0)
            qd_ref[h] = jnp.exp((pos + 1.0) * lg) * (dk ** -0.5)
            kd_ref[h] = jnp.exp((blk - 1.0 - pos) * lg)

    @pl.when(i % steps_per_seq == 0)
    def _init():
        st_ref[...] = r0_ref[0]

    hn = _rms(x_ref[...], g_ref[...]).astype(BF16)
    for h in range(heads):
        zh = _dot(hn, w_ref[:, h * hw:(h + 1) * hw])
        for s in range(tm // blk):
            rows = slice(s * blk, (s + 1) * blk)
            cosv = cos_ref[rows, :]
            sinv = sin_ref[rows, :]
            qh = zh[rows, 0:dk]
            kh = zh[rows, dk:2 * dk]
            vh = zh[rows, 2 * dk:2 * dk + dv].astype(BF16)
            gate = zh[rows, 2 * dk + dv:hw]
            qrot = qh * cosv + pltpu.roll(qh, dk // 2, 1) * sinv
            kr = kh * cosv + pltpu.roll(kh, dk // 2, 1) * sinv
            scores = _dot_nt((qrot * (dk ** -0.5)).astype(BF16), kr.astype(BF16)) * dm_ref[h]
            inner = _dot(scores.astype(BF16), vh)
            st = st_ref[h]
            cross = _dot((qrot * qd_ref[h]).astype(BF16), st.astype(BF16))
            ke = kr * kd_ref[h]
            st_ref[h] = math.exp(blk * log_gammas[h]) * st + _dot_tn(ke.astype(BF16), vh)
            o = inner + cross
            mean = jnp.mean(o, axis=-1, keepdims=True)
            d = o - mean
            var = jnp.mean(d * d, axis=-1, keepdims=True)
            on = d * lax.rsqrt(var + GN_EPS) * gn_ref[:, h * dv:(h + 1) * dv]
            yb_ref[rows, h * dv:(h + 1) * dv] = (jax.nn.silu(gate) * on).astype(yb_ref.dtype)

    @pl.when(i % steps_per_seq == steps_per_seq - 1)
    def _fin():
        ro_ref[0] = st_ref[...]


def _ret_proj(x, g, w_heads, cos_t, sin_t, r0, gn_g, seq_len):
    m, d = x.shape
    heads, dk, dv = r0.shape[1], r0.shape[2], r0.shape[3]
    vw = heads * dv
    tm = _tile(seq_len, 512, 64)
    blk = _tile(tm, 256, 64)
    sps = seq_len // tm
    kern = functools.partial(_ret_proj_kernel, heads=heads, dk=dk, dv=dv, blk=blk, steps_per_seq=sps)
    return pl.pallas_call(
        kern,
        grid=(m // tm,),
        in_specs=[
            pl.BlockSpec((tm, d), lambda i: (i, 0)),
            pl.BlockSpec((1, d), lambda i: (0, 0)),
            pl.BlockSpec(w_heads.shape, lambda i: (0, 0), pipeline_mode=pl.Buffered(1)),
            pl.BlockSpec((tm, dk), lambda i: (i % sps, 0)),
            pl.BlockSpec((tm, dk), lambda i: (i % sps, 0)),
            pl.BlockSpec((1, heads, dk, dv), lambda i: (i // sps, 0, 0, 0)),
            pl.BlockSpec((1, vw), lambda i: (0, 0)),
        ],
        out_specs=[
            pl.BlockSpec((tm, vw), lambda i: (i, 0)),
            pl.BlockSpec((1, heads, dk, dv), lambda i: (i // sps, 0, 0, 0)),
        ],
        out_shape=[
            jax.ShapeDtypeStruct((m, vw), BF16),
            jax.ShapeDtypeStruct((m // seq_len, heads, dk, dv), F32),
        ],
        scratch_shapes=[pltpu.VMEM((heads, dk, dv), F32), pltpu.VMEM((heads, blk, blk), F32),
                        pltpu.VMEM((heads, blk, dk), F32), pltpu.VMEM((heads, blk, dk), F32)],
        compiler_params=_params("arbitrary", vmem_limit=VMEM_LIMIT_RESIDENT_BYTES),
        name="retention_proj",
    )(x, g, w_heads, cos_t, sin_t, r0, gn_g.reshape(1, -1))


def _merge_kernel(ya_ref, yb_ref, zg_ref, x_ref, wba_ref, wbb_ref, wo_ref, g_ref, o_ref):
    d = x_ref.shape[1]
    ga = jax.nn.sigmoid(zg_ref[:, 0:d].astype(F32))
    gb = jax.nn.sigmoid(zg_ref[:, d:2 * d].astype(F32))
    merged = ga * _dot(ya_ref[...], wba_ref[...]) + gb * _dot(yb_ref[...], wbb_ref[...])
    mo = _dot(merged.astype(BF16), wo_ref[...])
    o_ref[...] = x_ref[...] + _rms(mo, g_ref[...])


def _merge(ya, yb, zg, x, wba, wbb, wo, g):
    m, d = x.shape
    tm = _tile(m, 256, 8)
    tok = lambda w: pl.BlockSpec((tm, w), lambda i: (i, 0))
    res = lambda a: pl.BlockSpec(a.shape, lambda i: (0, 0), pipeline_mode=pl.Buffered(1))
    return pl.pallas_call(
        _merge_kernel,
        grid=(m // tm,),
        in_specs=[tok(ya.shape[1]), tok(yb.shape[1]), tok(zg.shape[1]), tok(d),
                  res(wba), res(wbb), res(wo), pl.BlockSpec((1, d), lambda i: (0, 0))],
        out_specs=tok(d),
        out_shape=jax.ShapeDtypeStruct((m, d), F32),
        compiler_params=_params("arbitrary"),
        name="merge_out",
    )(ya, yb, zg, x, wba, wbb, wo, g)


FFN_CHUNK = V7X_MXU_DIM
FFN_PASS_CHUNKS = 6


def _ffn_pass_kernel(x_ref, gpre_ref, wg_ref, wu_ref, wo_ref, *rest, first, last):
    rest = list(rest)
    acc = None if first else rest.pop(0)[...]
    gpost_ref = rest.pop(0) if last else None
    o_ref = rest.pop(0)
    x = x_ref[...]
    h = _rms(x, gpre_ref[...]).astype(BF16)
    for c in range(wg_ref.shape[1] // FFN_CHUNK):
        cols = slice(c * FFN_CHUNK, (c + 1) * FFN_CHUNK)
        act = jax.nn.silu(_dot(h, wg_ref[:, cols])) * _dot(h, wu_ref[:, cols])
        part = _dot(act.astype(BF16), wo_ref[cols, :])
        acc = part if acc is None else acc + part
    o_ref[...] = x + _rms(acc, gpost_ref[...]) if last else acc


def _ffn(x, gpre, ffn_slices, gpost):
    m, d = x.shape
    tm = _tile(m, 512, 8)
    tok = pl.BlockSpec((tm, d), lambda i: (i, 0))
    vec = pl.BlockSpec((1, d), lambda i: (0, 0))
    res = lambda a: pl.BlockSpec(a.shape, lambda i: (0, 0), pipeline_mode=pl.Buffered(1))
    acc = None
    for p, (wg, wu, wo) in enumerate(ffn_slices):
        first, last = p == 0, p == len(ffn_slices) - 1
        args = [x, gpre, wg, wu, wo] + ([] if first else [acc]) + ([gpost] if last else [])
        specs = [tok, vec, res(wg), res(wu), res(wo)] + ([] if first else [tok]) + ([vec] if last else [])
        acc = pl.pallas_call(
            functools.partial(_ffn_pass_kernel, first=first, last=last),
            grid=(m // tm,),
            in_specs=specs,
            out_specs=tok,
            out_shape=jax.ShapeDtypeStruct((m, d), F32),
            compiler_params=_params("arbitrary"),
            name="swiglu_ffn_pass",
        )(*args)
    return acc


def _ffn_slices(w_fi, w_fo):
    f = w_fo.shape[0]
    n_chunks = f // FFN_CHUNK
    assert n_chunks * FFN_CHUNK == f
    n_pass = -(-n_chunks // FFN_PASS_CHUNKS)
    sizes = [n_chunks // n_pass + (1 if p < n_chunks % n_pass else 0) for p in range(n_pass)]
    out, off = [], 0
    for sz in sizes:
        w = sz * FFN_CHUNK
        out.append((w_fi[:, off:off + w].astype(BF16), w_fi[:, f + off:f + off + w].astype(BF16),
                    w_fo[off:off + w, :].astype(BF16)))
        off += w
    return out


def _rope_tables(pos, dk):
    half = dk // 2
    inv = ROPE_BASE ** (-jnp.arange(half, dtype=F32) / half)
    ang = pos[:, None] * inv[None, :]
    cos = jnp.cos(ang)
    sin = jnp.sin(ang)
    return jnp.concatenate([cos, cos], axis=-1), jnp.concatenate([-sin, sin], axis=-1)


def _head_major(w_b, heads, dk, dv):
    d = w_b.shape[0]
    qkw, vw = heads * dk, heads * dv
    parts = [w_b[:, 0:qkw].reshape(d, heads, dk), w_b[:, qkw:2 * qkw].reshape(d, heads, dk),
             w_b[:, 2 * qkw:2 * qkw + vw].reshape(d, heads, dv),
             w_b[:, 2 * qkw + vw:].reshape(d, heads, dv)]
    return jnp.concatenate(parts, axis=2).reshape(d, heads * (2 * dk + 2 * dv))


def _layer(x, rope, shift0, s0, r0, w):
    b, t, d = x.shape
    x2 = x.reshape(b * t, d)
    row = lambda a: a.reshape(1, -1)
    zg = _norm_matmul(x2, row(w["nmp"]), w["w_g"], BF16)
    aw = w["w_dup"].shape[1]
    ya, shift, s_new = _wkv_branch(x, row(w["nmp"]), w["w_a"], _group_major(shift0, aw), s0, w["mu"],
                                   w["w0"], w["w_dup"], w["a0"], w["w_aup"], w["w_gup"], w["k_k"],
                                   w["k_a"], w["r_k"], w["lnx_g"], w["lnx_b"])
    shift = _group_major_inverse(shift.reshape(b, -1), aw)
    yb, r_new = _ret_proj(x2, row(w["nmp"]), w["w_b"], rope[0], rope[1], r0, w["ret_g"], t)
    x1 = _merge(ya.reshape(b * t, -1), yb, zg, x2, w["w_ba"], w["w_bb"],
                w["w_out"], row(w["nmq"]))
    xo = _ffn(x1, row(w["nfp"]), w["ffn"], row(w["nfq"]))
    return xo.reshape(b, t, d), shift, s_new, r_new


def kernel(x_prompt, x_sample, state_wkv, state_ret, state_shift, norm_mix_pre, norm_mix_post, norm_ffn_pre, norm_ffn_post, w_in, shift_mu, decay_base, w_decay_up, iclr_base, w_iclr_up, w_gate_up, key_k, key_a, bonus_rk, lnx_gain, lnx_bias, w_branch_a, ret_norm_gain, w_branch_b, w_out, w_ffn_in, w_ffn_out):
    depth = w_in.shape[0]
    bp, tp, _ = x_prompt.shape
    bs, ts, _ = x_sample.shape
    asw = shift_mu.shape[1]
    heads_a, hd_a = bonus_rk.shape[1], bonus_rk.shape[2]
    heads_b, dk, dv = state_ret.shape[2], state_ret.shape[3], state_ret.shape[4]
    b_in = 2 * heads_b * dk + 2 * heads_b * dv
    dt = x_prompt.dtype

    rope_p = _rope_tables(jnp.arange(tp, dtype=F32), dk)
    rope_s = _rope_tables(jnp.arange(ts, dtype=F32) + jnp.float32(PAST_LEN), dk)
    zero_shift = jnp.zeros((bp, asw), dt)
    zero_wkv = jnp.zeros((bp, heads_a, hd_a, hd_a), dt)
    zero_ret = jnp.zeros((bp, heads_b, dk, dv), dt)

    xp, xs = x_prompt, x_sample
    outs = [[] for _ in range(6)]
    for l in range(depth):
        w = dict(
            nmp=norm_mix_pre[l], nmq=norm_mix_post[l], nfp=norm_ffn_pre[l], nfq=norm_ffn_post[l],
            w_a=_group_major(w_in[l, :, :asw], heads_a * hd_a).astype(BF16),
            w_b=_head_major(w_in[l, :, asw:asw + b_in], heads_b, dk, dv).astype(BF16),
            w_g=w_in[l, :, asw + b_in:].astype(BF16),
            mu=_group_major(shift_mu[l], heads_a * hd_a), w0=decay_base[l], w_dup=w_decay_up[l].astype(BF16), a0=iclr_base[l],
            w_aup=w_iclr_up[l].astype(BF16), w_gup=w_gate_up[l].astype(BF16), k_k=key_k[l],
            k_a=key_a[l], r_k=bonus_rk[l], lnx_g=lnx_gain[l], lnx_b=lnx_bias[l],
            w_ba=w_branch_a[l].astype(BF16), ret_g=ret_norm_gain[l],
            w_bb=w_branch_b[l].astype(BF16), w_out=w_out[l].astype(BF16),
            ffn=_ffn_slices(w_ffn_in[l], w_ffn_out[l]),
        )
        xp, sp, wkp, rtp = _layer(xp, rope_p, zero_shift, zero_wkv, zero_ret, w)
        xs, ss, wks, rts = _layer(xs, rope_s, state_shift[l], state_wkv[l], state_ret[l], w)
        for lst, val in zip(outs, (wkp, rtp, sp, wks, rts, ss)):
            lst.append(val)
    return (xp, xs) + tuple(jnp.stack(o) for o in outs)
```

```python
import functools
import math

import jax
import jax.numpy as jnp
from jax import lax
from jax.experimental import pallas as pl
from jax.experimental.pallas import tpu as pltpu

F32 = jnp.float32
BF16 = jnp.bfloat16

PAST_LEN = 4096
RMS_EPS = 1e-6
LNX_EPS = 64e-5
GN_EPS = 1e-5
ROPE_BASE = 10000.0

V7X_LANES = 128
V7X_MXU_DIM = 256
WKV_GROUP_LANES = V7X_LANES
WKV_BLOCK_LANES = V7X_MXU_DIM
WKV_CHUNK = 64
V7X_VMEM_BYTES = 64 * 1024 * 1024
VMEM_LIMIT_BYTES = V7X_VMEM_BYTES * 7 // 8
VMEM_LIMIT_RESIDENT_BYTES = V7X_VMEM_BYTES * 15 // 16


def _dot(a, b):
    return jnp.dot(a, b, preferred_element_type=F32)


def _dot_nt(a, b):
    return lax.dot_general(a, b, (((1,), (1,)), ((), ())), preferred_element_type=F32)


def _dot_tn(a, b):
    return lax.dot_general(a, b, (((0,), (0,)), ((), ())), preferred_element_type=F32)


def _rms(x, g):
    return x * lax.rsqrt(jnp.mean(x * x, axis=-1, keepdims=True) + RMS_EPS) * g


def _tile(n, target, mult=V7X_LANES):
    if n <= target:
        return n
    best = None
    for t in range(mult, target + 1, mult):
        if n % t == 0:
            best = t
    assert best is not None, (n, target, mult)
    return best


def _params(*sem, vmem_limit=VMEM_LIMIT_BYTES):
    return pltpu.CompilerParams(dimension_semantics=sem, vmem_limit_bytes=vmem_limit)


def _norm_matmul_kernel(x_ref, g_ref, w_ref, o_ref, *, tn):
    h = _rms(x_ref[...], g_ref[...]).astype(BF16)
    for j in range(w_ref.shape[1] // tn):
        cols = slice(j * tn, (j + 1) * tn)
        o_ref[:, cols] = _dot(h, w_ref[:, cols]).astype(o_ref.dtype)


def _norm_matmul(x, g, w, out_dtype, tm_target=512, tn_target=512):
    m, d = x.shape
    n = w.shape[1]
    tm = _tile(m, tm_target, 8)
    tn = _tile(n, tn_target)
    return pl.pallas_call(
        functools.partial(_norm_matmul_kernel, tn=tn),
        grid=(m // tm,),
        in_specs=[
            pl.BlockSpec((tm, d), lambda i: (i, 0)),
            pl.BlockSpec((1, d), lambda i: (0, 0)),
            pl.BlockSpec((d, n), lambda i: (0, 0), pipeline_mode=pl.Buffered(1)),
        ],
        out_specs=pl.BlockSpec((tm, n), lambda i: (i, 0)),
        out_shape=jax.ShapeDtypeStruct((m, n), out_dtype),
        compiler_params=_params("arbitrary"),
        name="norm_matmul",
    )(x, g, w)


def _split_bf16(x):
    hi = x.astype(BF16)
    lo = (x - hi.astype(F32)).astype(BF16)
    return hi, lo


def _wkv_kernel(x_ref, gn_ref, wa_ref, sh0_ref, s0_ref, mu_ref, w0_ref, wdup_ref, a0_ref, waup_ref,
                wgup_ref, kk_ref, ka_ref, rk_ref, lg_ref, lb_ref,
                ya_ref, sho_ref, so_ref,
                carry_ref, s_ref, ar_ref, bk_ref, bkh_ref, v_ref, y_ref, pc_ref, t_ref, av_ref, wcat_ref,
                v32_ref, rk2_ref, gate_ref,
                *, heads, head_dim, ranks):
    tt = x_ref.shape[1]
    aw = heads * head_dim
    gw = WKV_GROUP_LANES
    hpg = gw // head_dim
    groups = aw // gw
    c_len = min(WKV_CHUNK, tt)
    n_chunks = tt // c_len
    dr, ir, gr = ranks
    t = pl.program_id(1)

    @pl.when(t == 0)
    def _init():
        carry_ref[...] = sh0_ref[0]
        s_ref[...] = jnp.zeros_like(s_ref)
        for h in range(heads):
            g, hh = divmod(h, hpg)
            lo = hh * head_dim
            s_ref[g, lo:lo + head_dim, lo:lo + head_dim] = s0_ref[0, h]

    hn = _rms(x_ref[0], gn_ref[...]).astype(BF16)
    lw = dr + ir + gr

    def project_shifted(cols):
        z = _dot(hn, wa_ref[:, cols])
        rolled = pltpu.roll(z, 1, 0)
        row8 = lax.broadcasted_iota(jnp.int32, (8, z.shape[1]), 0)
        prev = jnp.concatenate([jnp.where(row8 == 0, carry_ref[:, cols], rolled[0:8]), rolled[8:]],
                               axis=0)
        last = z[tt - 1:tt, :]
        carry_ref[:, cols] = last
        sho_ref[0, :, cols] = last
        return z + mu_ref[:, cols] * (prev - z)

    zl = project_shifted(slice(0, lw))
    wd = zl[:, 0:dr]
    ad = zl[:, dr:dr + ir]
    gd = zl[:, dr + ir:lw]
    xw = w0_ref[...] + _dot(jnp.tanh(wd).astype(BF16), wdup_ref[...])
    ld = (-math.exp(-0.5)) * jax.nn.sigmoid(xw)
    a_ic = jax.nn.sigmoid(a0_ref[...] + _dot(ad.astype(BF16), waup_ref[...]))
    gate_ref[...] = _dot(jax.nn.sigmoid(gd).astype(BF16), wgup_ref[...])

    gi = lax.broadcasted_iota(jnp.int32, (gw, gw), 0) // head_dim
    gj = lax.broadcasted_iota(jnp.int32, (gw, gw), 1) // head_dim
    same_head = gi == gj

    sw = min(aw, WKV_BLOCK_LANES)
    si = lax.broadcasted_iota(jnp.int32, (sw, sw), 0) // head_dim
    sj = lax.broadcasted_iota(jnp.int32, (sw, sw), 1) // head_dim
    ones_sw = (si == sj).astype(BF16)

    def head_sum(x):
        xb = x.astype(BF16)
        parts = [_dot(xb[:, q * sw:(q + 1) * sw], ones_sw) for q in range(aw // sw)]
        return jnp.concatenate(parts, axis=1) if len(parts) > 1 else parts[0]

    ti = lax.broadcasted_iota(jnp.int32, (tt, tt), 0)
    tj = lax.broadcasted_iota(jnp.int32, (tt, tt), 1)
    tri = (((ti // c_len) == (tj // c_len)) & (tj <= ti)).astype(BF16)
    ld_hi, ld_lo = _split_bf16(ld)
    cum_all = _dot(tri, ld_hi) + _dot(tri, ld_lo)

    for q in range(aw // sw):
        gl = slice(q * sw, (q + 1) * sw)
        z3 = project_shifted(slice(lw + 3 * q * sw, lw + 3 * (q + 1) * sw))
        r = z3[:, 0:sw]
        k = z3[:, sw:2 * sw]
        v = z3[:, 2 * sw:3 * sw]
        a_g = a_ic[:, gl]
        ld_g = ld[:, gl]
        kk = k * kk_ref[:, gl]
        k2 = k * (1.0 + (a_g - 1.0) * ka_ref[:, gl])
        kkn = kk * jnp.minimum(lax.rsqrt(_dot((kk * kk).astype(BF16), ones_sw)), 1e12)
        cum = cum_all[:, gl]
        p_inc = jnp.exp(cum)
        p_inv = jnp.exp(-cum)
        a_t = -(kkn * jnp.exp(cum - ld_g))
        r_t = r * p_inc
        b_t = (kkn * a_g) * p_inv
        k_t = k2 * p_inv
        for c in range(n_chunks):
            rs = slice(c * c_len, (c + 1) * c_len)
            p_end = p_inc[(c + 1) * c_len - 1:(c + 1) * c_len, :]
            ar_ref[c, 0:c_len, gl] = a_t[rs].astype(BF16)
            ar_ref[c, c_len:2 * c_len, gl] = r_t[rs].astype(BF16)
            bk_ref[c, 0:c_len, gl] = b_t[rs].astype(BF16)
            bk_ref[c, c_len:2 * c_len, gl] = k_t[rs].astype(BF16)
            bkh_ref[c, 0:c_len, gl] = (b_t[rs] * p_end).astype(BF16)
            bkh_ref[c, c_len:2 * c_len, gl] = (k_t[rs] * p_end).astype(BF16)
            pc_ref[c, :, gl] = p_end
        v_ref[:, gl] = v.astype(BF16)
        v32_ref[:, gl] = v
        rk2_ref[:, gl] = r * k2 * rk_ref[:, gl]

    lane_head = lax.broadcasted_iota(jnp.int32, (1, gw), 1) // head_dim
    ci = lax.broadcasted_iota(jnp.int32, (c_len, gw), 0)
    cj = lax.broadcasted_iota(jnp.int32, (c_len, gw), 1) % head_dim
    assert c_len == head_dim
    strict = cj < ci
    incl = cj <= ci
    eye = (cj == ci).astype(F32)
    n_doubling = int(math.log2(c_len))

    def bd(m):
        zero = jnp.zeros_like(m)
        return jnp.concatenate([jnp.where(lane_head == h, m, zero) for h in range(hpg)], axis=0)

    gs = range(groups)
    sls = [slice(g * gw, (g + 1) * gw) for g in gs]

    def prep_body(c, carry):
        r0 = c * c_len
        ar = [ar_ref[c, :, sl] for sl in sls]
        bk = [bk_ref[c, :, sl] for sl in sls]
        vv = [v_ref[pl.ds(r0, c_len), sl] for sl in sls]
        s4 = [_dot_nt(ar[g], jnp.concatenate([bd(bk[g][0:c_len]), bd(bk[g][c_len:2 * c_len])],
                                              axis=0)) for g in gs]
        a_ab = [jnp.where(strict, s4[g][0:c_len, 0:gw], 0.0) for g in gs]
        a_ak = [jnp.where(strict, s4[g][0:c_len, gw:2 * gw], 0.0).astype(BF16) for g in gs]
        for g in gs:
            wcat_ref[c, g, :, 0:gw] = jnp.where(incl, s4[g][c_len:2 * c_len, 0:gw], 0.0).astype(BF16)
            wcat_ref[c, g, :, gw:2 * gw] = jnp.where(incl, s4[g][c_len:2 * c_len, gw:2 * gw],
                                                     0.0).astype(BF16)
        av = [_dot(a_ak[g], bd(vv[g])) for g in gs]
        for g in gs:
            av_ref[c, g] = av[g]
        tk = [eye + a_ab[g] for g in gs]
        mb = [a_ab[g].astype(BF16) for g in gs]
        mk = [_dot(mb[g], bd(mb[g])) for g in gs]
        for step in range(1, n_doubling - 1):
            mb = [mk[g].astype(BF16) for g in gs]
            rr = [_dot(mb[g], jnp.concatenate([bd(mb[g]), bd(tk[g].astype(BF16))], axis=1))
                  for g in gs]
            mk = [rr[g][:, 0:gw] for g in gs]
            tk = [tk[g] + rr[g][:, gw:2 * gw] for g in gs]
        tk = [tk[g] + _dot(mk[g].astype(BF16), bd(tk[g].astype(BF16))) for g in gs]
        for g in gs:
            t_ref[c, g] = tk[g].astype(BF16)
        return carry

    for c in range(n_chunks):
        prep_body(c, 0)

    def chunk_body(c, carry):
        r0 = c * c_len
        ar = [ar_ref[c, :, sl] for sl in sls]
        vv = [v_ref[pl.ds(r0, c_len), sl] for sl in sls]
        s0 = [s_ref[g] for g in gs]
        m1 = [_dot_nt(ar[g], s0[g].astype(BF16)) for g in gs]
        xb = [(m1[g][0:c_len] + av_ref[c, g]).astype(BF16) for g in gs]
        ub = [_dot(t_ref[c, g], bd(xb[g])).astype(BF16) for g in gs]
        ds = [_dot_tn(jnp.concatenate([ub[g], vv[g]], axis=0), bkh_ref[c, :, sls[g]]) for g in gs]
        for g in gs:
            s_ref[g] = s0[g] * pc_ref[c][:, sls[g]] + jnp.where(same_head, ds[g], 0.0)
        yy = [_dot(wcat_ref[c, g], jnp.concatenate([bd(ub[g]), bd(vv[g])], axis=0)) for g in gs]
        for g in gs:
            y_ref[pl.ds(r0, c_len), sls[g]] = m1[g][c_len:2 * c_len] + yy[g]
        return carry

    for c in range(n_chunks):
        chunk_body(c, 0)

    y = y_ref[...]
    inv_n = 1.0 / head_dim
    mean = head_sum(y) * inv_n
    d = y - mean
    var = head_sum(d * d) * inv_n
    yn = d * lax.rsqrt(var + LNX_EPS) * lg_ref[...] + lb_ref[...]
    bonus = head_sum(rk2_ref[...]) * v32_ref[...]
    ya_ref[0] = ((yn + bonus) * gate_ref[...]).astype(ya_ref.dtype)

    @pl.when(t == pl.num_programs(1) - 1)
    def _fin():
        for h in range(heads):
            g, hh = divmod(h, hpg)
            lo = hh * head_dim
            so_ref[0, h] = s_ref[g, lo:lo + head_dim, lo:lo + head_dim]


def _group_major(a, aw):
    lead = a.shape[:-1]
    bw = min(aw, WKV_BLOCK_LANES)
    g = aw // bw
    rkv = a[..., :3 * aw].reshape(*lead, 3, g, bw)
    rkv = jnp.swapaxes(rkv, -3, -2).reshape(*lead, 3 * aw)
    return jnp.concatenate([a[..., 3 * aw:], rkv], axis=-1)


def _group_major_inverse(a, aw):
    lead = a.shape[:-1]
    bw = min(aw, WKV_BLOCK_LANES)
    g = aw // bw
    lw = a.shape[-1] - 3 * aw
    rkv = a[..., lw:].reshape(*lead, g, 3, bw)
    rkv = jnp.swapaxes(rkv, -3, -2).reshape(*lead, 3 * aw)
    return jnp.concatenate([rkv, a[..., :lw]], axis=-1)


def _wkv_branch(x, gn, wa, shift0, s0, mu, w0, wdup, a0, waup, wgup, k_k, k_a, r_k, lnx_g, lnx_b):
    b, t, d = x.shape
    asw = wa.shape[1]
    heads, head_dim = s0.shape[1], s0.shape[2]
    aw = heads * head_dim
    ranks = (wdup.shape[0], waup.shape[0], wgup.shape[0])
    tt = _tile(t, 256, WKV_CHUNK)
    c_len = min(WKV_CHUNK, tt)
    n_chunks = tt // c_len
    groups = aw // WKV_GROUP_LANES
    row = lambda a: a.reshape(1, -1)
    const = lambda shape: pl.BlockSpec(shape, lambda i, j: (0,) * len(shape))
    kern = functools.partial(_wkv_kernel, heads=heads, head_dim=head_dim, ranks=ranks)
    return pl.pallas_call(
        kern,
        grid=(b, t // tt),
        in_specs=[
            pl.BlockSpec((1, tt, d), lambda i, j: (i, j, 0)),
            const((1, d)),
            pl.BlockSpec(wa.shape, lambda i, j: (0, 0), pipeline_mode=pl.Buffered(1)),
            pl.BlockSpec((1, 1, asw), lambda i, j: (i, 0, 0)),
            pl.BlockSpec((1, heads, head_dim, head_dim), lambda i, j: (i, 0, 0, 0)),
            const((1, asw)), const((1, aw)), const(wdup.shape), const((1, aw)), const(waup.shape),
            const(wgup.shape), const((1, aw)), const((1, aw)), const((1, aw)), const((1, aw)),
            const((1, aw)),
        ],
        out_specs=[
            pl.BlockSpec((1, tt, aw), lambda i, j: (i, j, 0)),
            pl.BlockSpec((1, 1, asw), lambda i, j: (i, 0, 0)),
            pl.BlockSpec((1, heads, head_dim, head_dim), lambda i, j: (i, 0, 0, 0)),
        ],
        out_shape=[
            jax.ShapeDtypeStruct((b, t, aw), BF16),
            jax.ShapeDtypeStruct((b, 1, asw), F32),
            jax.ShapeDtypeStruct((b, heads, head_dim, head_dim), F32),
        ],
        scratch_shapes=[
            pltpu.VMEM((1, asw), F32),
            pltpu.VMEM((groups, WKV_GROUP_LANES, WKV_GROUP_LANES), F32),
            pltpu.VMEM((n_chunks, 2 * c_len, aw), BF16),
            pltpu.VMEM((n_chunks, 2 * c_len, aw), BF16),
            pltpu.VMEM((n_chunks, 2 * c_len, aw), BF16),
            pltpu.VMEM((tt, aw), BF16),
            pltpu.VMEM((tt, aw), F32),
            pltpu.VMEM((n_chunks, 1, aw), F32),
            pltpu.VMEM((n_chunks, groups, c_len, WKV_GROUP_LANES), BF16),
            pltpu.VMEM((n_chunks, groups, c_len, WKV_GROUP_LANES), F32),
            pltpu.VMEM((n_chunks, groups, c_len, 2 * WKV_GROUP_LANES), BF16),
            pltpu.VMEM((tt, aw), F32),
            pltpu.VMEM((tt, aw), F32),
            pltpu.VMEM((tt, aw), F32),
        ],
        compiler_params=_params("arbitrary", "arbitrary"),
        name="wkv7_chunked",
    )(x, gn, wa, shift0.reshape(b, 1, asw), s0, row(mu), row(w0), wdup, row(a0), waup, wgup,
      row(k_k), row(k_a), row(r_k), row(lnx_g), row(lnx_b))


def _ret_proj_kernel(x_ref, g_ref, w_ref, cos_ref, sin_ref, r0_ref, gn_ref, yb_ref, ro_ref,
                     st_ref, dm_ref, qd_ref, kd_ref, *, heads, dk, dv, blk, steps_per_seq):
    tm = x_ref.shape[0]
    hw = 2 * dk + 2 * dv
    i = pl.program_id(0)
    log_gammas = [math.log1p(-(2.0 ** (-5.0 - h))) for h in range(heads)]

    @pl.when(i == 0)
    def _tables():
        diff = (lax.broadcasted_iota(jnp.int32, (blk, blk), 0)
                - lax.broadcasted_iota(jnp.int32, (blk, blk), 1)).astype(F32)
        pos = lax.broadcasted_iota(jnp.int32, (blk, dk), 0).astype(F32)
        for h in range(heads):
            lg = log_gammas[h]
            dm_ref[h] = jnp.where(diff >= 0, jnp.exp(jnp.maximum(diff, 0.0) * lg), 0.0)
            qd_ref[h] = jnp.exp((pos + 1.0) * lg) * (dk ** -0.5)
            kd_ref[h] = jnp.exp((blk - 1.0 - pos) * lg)

    @pl.when(i % steps_per_seq == 0)
    def _init():
        st_ref[...] = r0_ref[0]

    hn = _rms(x_ref[...], g_ref[...]).astype(BF16)
    for h in range(heads):
        zh = _dot(hn, w_ref[:, h * hw:(h + 1) * hw])
        for s in range(tm // blk):
            rows = slice(s * blk, (s + 1) * blk)
            cosv = cos_ref[rows, :]
            sinv = sin_ref[rows, :]
            qh = zh[rows, 0:dk]
            kh = zh[rows, dk:2 * dk]
            vh = zh[rows, 2 * dk:2 * dk + dv].astype(BF16)
            gate = zh[rows, 2 * dk + dv:hw]
            qrot = qh * cosv + pltpu.roll(qh, dk // 2, 1) * sinv
            kr = kh * cosv + pltpu.roll(kh, dk // 2, 1) * sinv
            scores = _dot_nt((qrot * (dk ** -0.5)).astype(BF16), kr.astype(BF16)) * dm_ref[h]
            inner = _dot(scores.astype(BF16), vh)
            st = st_ref[h]
            cross = _dot((qrot * qd_ref[h]).astype(BF16), st.astype(BF16))
            ke = kr * kd_ref[h]
            st_ref[h] = math.exp(blk * log_gammas[h]) * st + _dot_tn(ke.astype(BF16), vh)
            o = inner + cross
            mean = jnp.mean(o, axis=-1, keepdims=True)
            d = o - mean
            var = jnp.mean(d * d, axis=-1, keepdims=True)
            on = d * lax.rsqrt(var + GN_EPS) * gn_ref[:, h * dv:(h + 1) * dv]
            yb_ref[rows, h * dv:(h + 1) * dv] = (jax.nn.silu(gate) * on).astype(yb_ref.dtype)

    @pl.when(i % steps_per_seq == steps_per_seq - 1)
    def _fin():
        ro_ref[0] = st_ref[...]


def _ret_proj(x, g, w_heads, cos_t, sin_t, r0, gn_g, seq_len):
    m, d = x.shape
    heads, dk, dv = r0.shape[1], r0.shape[2], r0.shape[3]
    vw = heads * dv
    tm = _tile(seq_len, 512, 64)
    blk = _tile(tm, 256, 64)
    sps = seq_len // tm
    kern = functools.partial(_ret_proj_kernel, heads=heads, dk=dk, dv=dv, blk=blk, steps_per_seq=sps)
    return pl.pallas_call(
        kern,
        grid=(m // tm,),
        in_specs=[
            pl.BlockSpec((tm, d), lambda i: (i, 0)),
            pl.BlockSpec((1, d), lambda i: (0, 0)),
            pl.BlockSpec(w_heads.shape, lambda i: (0, 0), pipeline_mode=pl.Buffered(1)),
            pl.BlockSpec((tm, dk), lambda i: (i % sps, 0)),
            pl.BlockSpec((tm, dk), lambda i: (i % sps, 0)),
            pl.BlockSpec((1, heads, dk, dv), lambda i: (i // sps, 0, 0, 0)),
            pl.BlockSpec((1, vw), lambda i: (0, 0)),
        ],
        out_specs=[
            pl.BlockSpec((tm, vw), lambda i: (i, 0)),
            pl.BlockSpec((1, heads, dk, dv), lambda i: (i // sps, 0, 0, 0)),
        ],
        out_shape=[
            jax.ShapeDtypeStruct((m, vw), BF16),
            jax.ShapeDtypeStruct((m // seq_len, heads, dk, dv), F32),
        ],
        scratch_shapes=[pltpu.VMEM((heads, dk, dv), F32), pltpu.VMEM((heads, blk, blk), F32),
                        pltpu.VMEM((heads, blk, dk), F32), pltpu.VMEM((heads, blk, dk), F32)],
        compiler_params=_params("arbitrary", vmem_limit=VMEM_LIMIT_RESIDENT_BYTES),
        name="retention_proj",
    )(x, g, w_heads, cos_t, sin_t, r0, gn_g.reshape(1, -1))


def _merge_kernel(ya_ref, yb_ref, zg_ref, x_ref, wba_ref, wbb_ref, wo_ref, g_ref, o_ref):
    d = x_ref.shape[1]
    ga = jax.nn.sigmoid(zg_ref[:, 0:d].astype(F32))
    gb = jax.nn.sigmoid(zg_ref[:, d:2 * d].astype(F32))
    merged = ga * _dot(ya_ref[...], wba_ref[...]) + gb * _dot(yb_ref[...], wbb_ref[...])
    mo = _dot(merged.astype(BF16), wo_ref[...])
    o_ref[...] = x_ref[...] + _rms(mo, g_ref[...])


def _merge(ya, yb, zg, x, wba, wbb, wo, g):
    m, d = x.shape
    tm = _tile(m, 256, 8)
    tok = lambda w: pl.BlockSpec((tm, w), lambda i: (i, 0))
    res = lambda a: pl.BlockSpec(a.shape, lambda i: (0, 0), pipeline_mode=pl.Buffered(1))
    return pl.pallas_call(
        _merge_kernel,
        grid=(m // tm,),
        in_specs=[tok(ya.shape[1]), tok(yb.shape[1]), tok(zg.shape[1]), tok(d),
                  res(wba), res(wbb), res(wo), pl.BlockSpec((1, d), lambda i: (0, 0))],
        out_specs=tok(d),
        out_shape=jax.ShapeDtypeStruct((m, d), F32),
        compiler_params=_params("arbitrary"),
        name="merge_out",
    )(ya, yb, zg, x, wba, wbb, wo, g)


FFN_CHUNK = V7X_MXU_DIM
FFN_PASS_CHUNKS = 6


def _ffn_pass_kernel(x_ref, gpre_ref, wg_ref, wu_ref, wo_ref, *rest, first, last):
    rest = list(rest)
    acc = None if first else rest.pop(0)[...]
    gpost_ref = rest.pop(0) if last else None
    o_ref = rest.pop(0)
    x = x_ref[...]
    h = _rms(x, gpre_ref[...]).astype(BF16)
    for c in range(wg_ref.shape[1] // FFN_CHUNK):
        cols = slice(c * FFN_CHUNK, (c + 1) * FFN_CHUNK)
        act = jax.nn.silu(_dot(h, wg_ref[:, cols])) * _dot(h, wu_ref[:, cols])
        part = _dot(act.astype(BF16), wo_ref[cols, :])
        acc = part if acc is None else acc + part
    o_ref[...] = x + _rms(acc, gpost_ref[...]) if last else acc


def _ffn(x, gpre, ffn_slices, gpost):
    m, d = x.shape
    tm = _tile(m, 512, 8)
    tok = pl.BlockSpec((tm, d), lambda i: (i, 0))
    vec = pl.BlockSpec((1, d), lambda i: (0, 0))
    res = lambda a: pl.BlockSpec(a.shape, lambda i: (0, 0), pipeline_mode=pl.Buffered(1))
    acc = None
    for p, (wg, wu, wo) in enumerate(ffn_slices):
        first, last = p == 0, p == len(ffn_slices) - 1
        args = [x, gpre, wg, wu, wo] + ([] if first else [acc]) + ([gpost] if last else [])
        specs = [tok, vec, res(wg), res(wu), res(wo)] + ([] if first else [tok]) + ([vec] if last else [])
        acc = pl.pallas_call(
            functools.partial(_ffn_pass_kernel, first=first, last=last),
            grid=(m // tm,),
            in_specs=specs,
            out_specs=tok,
            out_shape=jax.ShapeDtypeStruct((m, d), F32),
            compiler_params=_params("arbitrary"),
            name="swiglu_ffn_pass",
        )(*args)
    return acc


def _ffn_slices(w_fi, w_fo):
    f = w_fo.shape[0]
    n_chunks = f // FFN_CHUNK
    assert n_chunks * FFN_CHUNK == f
    n_pass = -(-n_chunks // FFN_PASS_CHUNKS)
    sizes = [n_chunks // n_pass + (1 if p < n_chunks % n_pass else 0) for p in range(n_pass)]
    out, off = [], 0
    for sz in sizes:
        w = sz * FFN_CHUNK
        out.append((w_fi[:, off:off + w].astype(BF16), w_fi[:, f + off:f + off + w].astype(BF16),
                    w_fo[off:off + w, :].astype(BF16)))
        off += w
    return out


def _rope_tables(pos, dk):
    half = dk // 2
    inv = ROPE_BASE ** (-jnp.arange(half, dtype=F32) / half)
    ang = pos[:, None] * inv[None, :]
    cos = jnp.cos(ang)
    sin = jnp.sin(ang)
    return jnp.concatenate([cos, cos], axis=-1), jnp.concatenate([-sin, sin], axis=-1)


def _head_major(w_b, heads, dk, dv):
    d = w_b.shape[0]
    qkw, vw = heads * dk, heads * dv
    parts = [w_b[:, 0:qkw].reshape(d, heads, dk), w_b[:, qkw:2 * qkw].reshape(d, heads, dk),
             w_b[:, 2 * qkw:2 * qkw + vw].reshape(d, heads, dv),
             w_b[:, 2 * qkw + vw:].reshape(d, heads, dv)]
    return jnp.concatenate(parts, axis=2).reshape(d, heads * (2 * dk + 2 * dv))


def _layer(x, rope, shift0, s0, r0, w):
    b, t, d = x.shape
    x2 = x.reshape(b * t, d)
    row = lambda a: a.reshape(1, -1)
    zg = _norm_matmul(x2, row(w["nmp"]), w["w_g"], BF16)
    aw = w["w_dup"].shape[1]
    ya, shift, s_new = _wkv_branch(x, row(w["nmp"]), w["w_a"], _group_major(shift0, aw), s0, w["mu"],
                                   w["w0"], w["w_dup"], w["a0"], w["w_aup"], w["w_gup"], w["k_k"],
                                   w["k_a"], w["r_k"], w["lnx_g"], w["lnx_b"])
    shift = _group_major_inverse(shift.reshape(b, -1), aw)
    yb, r_new = _ret_proj(x2, row(w["nmp"]), w["w_b"], rope[0], rope[1], r0, w["ret_g"], t)
    x1 = _merge(ya.reshape(b * t, -1), yb, zg, x2, w["w_ba"], w["w_bb"],
                w["w_out"], row(w["nmq"]))
    xo = _ffn(x1, row(w["nfp"]), w["ffn"], row(w["nfq"]))
    return xo.reshape(b, t, d), shift, s_new, r_new


def kernel(x_prompt, x_sample, state_wkv, state_ret, state_shift, norm_mix_pre, norm_mix_post, norm_ffn_pre, norm_ffn_post, w_in, shift_mu, decay_base, w_decay_up, iclr_base, w_iclr_up, w_gate_up, key_k, key_a, bonus_rk, lnx_gain, lnx_bias, w_branch_a, ret_norm_gain, w_branch_b, w_out, w_ffn_in, w_ffn_out):
    depth = w_in.shape[0]
    bp, tp, _ = x_prompt.shape
    bs, ts, _ = x_sample.shape
    asw = shift_mu.shape[1]
    heads_a, hd_a = bonus_rk.shape[1], bonus_rk.shape[2]
    heads_b, dk, dv = state_ret.shape[2], state_ret.shape[3], state_ret.shape[4]
    b_in = 2 * heads_b * dk + 2 * heads_b * dv
    dt = x_prompt.dtype

    rope_p = _rope_tables(jnp.arange(tp, dtype=F32), dk)
    rope_s = _rope_tables(jnp.arange(ts, dtype=F32) + jnp.float32(PAST_LEN), dk)
    zero_shift = jnp.zeros((bp, asw), dt)
    zero_wkv = jnp.zeros((bp, heads_a, hd_a, hd_a), dt)
    zero_ret = jnp.zeros((bp, heads_b, dk, dv), dt)

    xp, xs = x_prompt, x_sample
    outs = [[] for _ in range(6)]
    for l in range(depth):
        w = dict(
            nmp=norm_mix_pre[l], nmq=norm_mix_post[l], nfp=norm_ffn_pre[l], nfq=norm_ffn_post[l],
            w_a=_group_major(w_in[l, :, :asw], heads_a * hd_a).astype(BF16),
            w_b=_head_major(w_in[l, :, asw:asw + b_in], heads_b, dk, dv).astype(BF16),
            w_g=w_in[l, :, asw + b_in:].astype(BF16),
            mu=_group_major(shift_mu[l], heads_a * hd_a), w0=decay_base[l], w_dup=w_decay_up[l].astype(BF16), a0=iclr_base[l],
            w_aup=w_iclr_up[l].astype(BF16), w_gup=w_gate_up[l].astype(BF16), k_k=key_k[l],
            k_a=key_a[l], r_k=bonus_rk[l], lnx_g=lnx_gain[l], lnx_b=lnx_bias[l],
            w_ba=w_branch_a[l].astype(BF16), ret_g=ret_norm_gain[l],
            w_bb=w_branch_b[l].astype(BF16), w_out=w_out[l].astype(BF16),
            ffn=_ffn_slices(w_ffn_in[l], w_ffn_out[l]),
        )
        xp, sp, wkp, rtp = _layer(xp, rope_p, zero_shift, zero_wkv, zero_ret, w)
        xs, ss, wks, rts = _layer(xs, rope_s, state_shift[l], state_wkv[l], state_ret[l], w)
        for lst, val in zip(outs, (wkp, rtp, sp, wks, rts, ss)):
            lst.append(val)
    return (xp, xs) + tuple(jnp.stack(o) for o in outs)
```

```python
import functools
import math

import jax
import jax.numpy as jnp
from jax import lax
from jax.experimental import pallas as pl
from jax.experimental.pallas import tpu as pltpu

F32 = jnp.float32
BF16 = jnp.bfloat16

PAST_LEN = 4096
RMS_EPS = 1e-6
LNX_EPS = 64e-5
GN_EPS = 1e-5
ROPE_BASE = 10000.0

V7X_LANES = 128
V7X_MXU_DIM = 256
WKV_GROUP_LANES = V7X_LANES
WKV_BLOCK_LANES = V7X_MXU_DIM
WKV_CHUNK = 64
V7X_VMEM_BYTES = 64 * 1024 * 1024
VMEM_LIMIT_BYTES = V7X_VMEM_BYTES * 7 // 8
VMEM_LIMIT_RESIDENT_BYTES = V7X_VMEM_BYTES * 15 // 16


def _dot(a, b):
    return jnp.dot(a, b, preferred_element_type=F32)


def _dot_nt(a, b):
    return lax.dot_general(a, b, (((1,), (1,)), ((), ())), preferred_element_type=F32)


def _dot_tn(a, b):
    return lax.dot_general(a, b, (((0,), (0,)), ((), ())), preferred_element_type=F32)


def _rms(x, g):
    return x * lax.rsqrt(jnp.mean(x * x, axis=-1, keepdims=True) + RMS_EPS) * g


def _tile(n, target, mult=V7X_LANES):
    if n <= target:
        return n
    best = None
    for t in range(mult, target + 1, mult):
        if n % t == 0:
            best = t
    assert best is not None, (n, target, mult)
    return best


def _params(*sem, vmem_limit=VMEM_LIMIT_BYTES):
    return pltpu.CompilerParams(dimension_semantics=sem, vmem_limit_bytes=vmem_limit)


def _norm_matmul_kernel(x_ref, g_ref, w_ref, o_ref, *, tn):
    h = _rms(x_ref[...], g_ref[...]).astype(BF16)
    for j in range(w_ref.shape[1] // tn):
        cols = slice(j * tn, (j + 1) * tn)
        o_ref[:, cols] = _dot(h, w_ref[:, cols]).astype(o_ref.dtype)


def _norm_matmul(x, g, w, out_dtype, tm_target=512, tn_target=512):
    m, d = x.shape
    n = w.shape[1]
    tm = _tile(m, tm_target, 8)
    tn = _tile(n, tn_target)
    return pl.pallas_call(
        functools.partial(_norm_matmul_kernel, tn=tn),
        grid=(m // tm,),
        in_specs=[
            pl.BlockSpec((tm, d), lambda i: (i, 0)),
            pl.BlockSpec((1, d), lambda i: (0, 0)),
            pl.BlockSpec((d, n), lambda i: (0, 0), pipeline_mode=pl.Buffered(1)),
        ],
        out_specs=pl.BlockSpec((tm, n), lambda i: (i, 0)),
        out_shape=jax.ShapeDtypeStruct((m, n), out_dtype),
        compiler_params=_params("arbitrary"),
        name="norm_matmul",
    )(x, g, w)


def _split_bf16(x):
    hi = x.astype(BF16)
    lo = (x - hi.astype(F32)).astype(BF16)
    return hi, lo


def _wkv_kernel(x_ref, gn_ref, wa_ref, sh0_ref, s0_ref, mu_ref, w0_ref, wdup_ref, a0_ref, waup_ref,
                wgup_ref, kk_ref, ka_ref, rk_ref, lg_ref, lb_ref,
                ya_ref, sho_ref, so_ref,
                carry_ref, s_ref, ar_ref, bk_ref, bkh_ref, v_ref, y_ref, pc_ref, t_ref, av_ref, wcat_ref,
                v32_ref, rk2_ref, gate_ref,
                *, heads, head_dim, ranks):
    tt = x_ref.shape[1]
    aw = heads * head_dim
    gw = WKV_GROUP_LANES
    hpg = gw // head_dim
    groups = aw // gw
    c_len = min(WKV_CHUNK, tt)
    n_chunks = tt // c_len
    dr, ir, gr = ranks
    t = pl.program_id(1)

    @pl.when(t == 0)
    def _init():
        carry_ref[...] = sh0_ref[0]
        s_ref[...] = jnp.zeros_like(s_ref)
        for h in range(heads):
            g, hh = divmod(h, hpg)
            lo = hh * head_dim
            s_ref[g, lo:lo + head_dim, lo:lo + head_dim] = s0_ref[0, h]

    hn = _rms(x_ref[0], gn_ref[...]).astype(BF16)
    lw = dr + ir + gr

    def project_shifted(cols):
        z = _dot(hn, wa_ref[:, cols])
        rolled = pltpu.roll(z, 1, 0)
        row8 = lax.broadcasted_iota(jnp.int32, (8, z.shape[1]), 0)
        prev = jnp.concatenate([jnp.where(row8 == 0, carry_ref[:, cols], rolled[0:8]), rolled[8:]],
                               axis=0)
        last = z[tt - 1:tt, :]
        carry_ref[:, cols] = last
        sho_ref[0, :, cols] = last
        return z + mu_ref[:, cols] * (prev - z)

    zl = project_shifted(slice(0, lw))
    wd = zl[:, 0:dr]
    ad = zl[:, dr:dr + ir]
    gd = zl[:, dr + ir:lw]
    xw = w0_ref[...] + _dot(jnp.tanh(wd).astype(BF16), wdup_ref[...])
    ld = (-math.exp(-0.5)) * jax.nn.sigmoid(xw)
    a_ic = jax.nn.sigmoid(a0_ref[...] + _dot(ad.astype(BF16), waup_ref[...]))
    gate_ref[...] = _dot(jax.nn.sigmoid(gd).astype(BF16), wgup_ref[...])

    gi = lax.broadcasted_iota(jnp.int32, (gw, gw), 0) // head_dim
    gj = lax.broadcasted_iota(jnp.int32, (gw, gw), 1) // head_dim
    same_head = gi == gj

    sw = min(aw, WKV_BLOCK_LANES)
    si = lax.broadcasted_iota(jnp.int32, (sw, sw), 0) // head_dim
    sj = lax.broadcasted_iota(jnp.int32, (sw, sw), 1) // head_dim
    ones_sw = (si == sj).astype(BF16)

    def head_sum(x):
        xb = x.astype(BF16)
        parts = [_dot(xb[:, q * sw:(q + 1) * sw], ones_sw) for q in range(aw // sw)]
        return jnp.concatenate(parts, axis=1) if len(parts) > 1 else parts[0]

    tb = min(tt, V7X_MXU_DIM)
    ti = lax.broadcasted_iota(jnp.int32, (tb, tb), 0)
    tj = lax.broadcasted_iota(jnp.int32, (tb, tb), 1)
    tri = (((ti // c_len) == (tj // c_len)) & (tj <= ti)).astype(BF16)
    ld_hi, ld_lo = _split_bf16(ld)
    cum_all = jnp.concatenate(
        [_dot(tri, ld_hi[i * tb:(i + 1) * tb]) + _dot(tri, ld_lo[i * tb:(i + 1) * tb])
         for i in range(tt // tb)], axis=0)

    for q in range(aw // sw):
        gl = slice(q * sw, (q + 1) * sw)
        z3 = project_shifted(slice(lw + 3 * q * sw, lw + 3 * (q + 1) * sw))
        r = z3[:, 0:sw]
        k = z3[:, sw:2 * sw]
        v = z3[:, 2 * sw:3 * sw]
        a_g = a_ic[:, gl]
        ld_g = ld[:, gl]
        kk = k * kk_ref[:, gl]
        k2 = k * (1.0 + (a_g - 1.0) * ka_ref[:, gl])
        kkn = kk * jnp.minimum(lax.rsqrt(_dot((kk * kk).astype(BF16), ones_sw)), 1e12)
        cum = cum_all[:, gl]
        p_inc = jnp.exp(cum)
        p_inv = jnp.exp(-cum)
        a_t = -(kkn * jnp.exp(cum - ld_g))
        r_t = r * p_inc
        b_t = (kkn * a_g) * p_inv
        k_t = k2 * p_inv
        for c in range(n_chunks):
            rs = slice(c * c_len, (c + 1) * c_len)
            p_end = p_inc[(c + 1) * c_len - 1:(c + 1) * c_len, :]
            ar_ref[c, 0:c_len, gl] = a_t[rs].astype(BF16)
            ar_ref[c, c_len:2 * c_len, gl] = r_t[rs].astype(BF16)
            bk_ref[c, 0:c_len, gl] = b_t[rs].astype(BF16)
            bk_ref[c, c_len:2 * c_len, gl] = k_t[rs].astype(BF16)
            bkh_ref[c, 0:c_len, gl] = (b_t[rs] * p_end).astype(BF16)
            bkh_ref[c, c_len:2 * c_len, gl] = (k_t[rs] * p_end).astype(BF16)
            pc_ref[c, :, gl] = p_end
        v_ref[:, gl] = v.astype(BF16)
        v32_ref[:, gl] = v
        rk2_ref[:, gl] = r * k2 * rk_ref[:, gl]

    lane_head = lax.broadcasted_iota(jnp.int32, (1, gw), 1) // head_dim
    ci = lax.broadcasted_iota(jnp.int32, (c_len, gw), 0)
    cj = lax.broadcasted_iota(jnp.int32, (c_len, gw), 1) % head_dim
    assert c_len == head_dim
    strict = cj < ci
    incl = cj <= ci
    eye = (cj == ci).astype(F32)
    n_doubling = int(math.log2(c_len))

    def bd(m):
        zero = jnp.zeros_like(m)
        return jnp.concatenate([jnp.where(lane_head == h, m, zero) for h in range(hpg)], axis=0)

    gs = range(groups)
    sls = [slice(g * gw, (g + 1) * gw) for g in gs]

    def prep_body(c, carry):
        r0 = c * c_len
        ar = [ar_ref[c, :, sl] for sl in sls]
        bk = [bk_ref[c, :, sl] for sl in sls]
        vv = [v_ref[pl.ds(r0, c_len), sl] for sl in sls]
        s4 = [_dot_nt(ar[g], jnp.concatenate([bd(bk[g][0:c_len]), bd(bk[g][c_len:2 * c_len])],
                                              axis=0)) for g in gs]
        a_ab = [jnp.where(strict, s4[g][0:c_len, 0:gw], 0.0) for g in gs]
        a_ak = [jnp.where(strict, s4[g][0:c_len, gw:2 * gw], 0.0).astype(BF16) for g in gs]
        for g in gs:
            wcat_ref[c, g, :, 0:gw] = jnp.where(incl, s4[g][c_len:2 * c_len, 0:gw], 0.0).astype(BF16)
            wcat_ref[c, g, :, gw:2 * gw] = jnp.where(incl, s4[g][c_len:2 * c_len, gw:2 * gw],
                                                     0.0).astype(BF16)
        av = [_dot(a_ak[g], bd(vv[g])) for g in gs]
        for g in gs:
            av_ref[c, g] = av[g]
        tk = [eye + a_ab[g] for g in gs]
        mb = [a_ab[g].astype(BF16) for g in gs]
        mk = [_dot(mb[g], bd(mb[g])) for g in gs]
        for step in range(1, n_doubling - 1):
            mb = [mk[g].astype(BF16) for g in gs]
            rr = [_dot(mb[g], jnp.concatenate([bd(mb[g]), bd(tk[g].astype(BF16))], axis=1))
                  for g in gs]
            mk = [rr[g][:, 0:gw] for g in gs]
            tk = [tk[g] + rr[g][:, gw:2 * gw] for g in gs]
        tk = [tk[g] + _dot(mk[g].astype(BF16), bd(tk[g].astype(BF16))) for g in gs]
        for g in gs:
            t_ref[c, g] = tk[g].astype(BF16)
        return carry

    for c in range(n_chunks):
        prep_body(c, 0)

    def chunk_body(c, carry):
        r0 = c * c_len
        ar = [ar_ref[c, :, sl] for sl in sls]
        vv = [v_ref[pl.ds(r0, c_len), sl] for sl in sls]
        s0 = [s_ref[g] for g in gs]
        m1 = [_dot_nt(ar[g], s0[g].astype(BF16)) for g in gs]
        xb = [(m1[g][0:c_len] + av_ref[c, g]).astype(BF16) for g in gs]
        ub = [_dot(t_ref[c, g], bd(xb[g])).astype(BF16) for g in gs]
        ds = [_dot_tn(jnp.concatenate([ub[g], vv[g]], axis=0), bkh_ref[c, :, sls[g]]) for g in gs]
        for g in gs:
            s_ref[g] = s0[g] * pc_ref[c][:, sls[g]] + jnp.where(same_head, ds[g], 0.0)
        yy = [_dot(wcat_ref[c, g], jnp.concatenate([bd(ub[g]), bd(vv[g])], axis=0)) for g in gs]
        for g in gs:
            y_ref[pl.ds(r0, c_len), sls[g]] = m1[g][c_len:2 * c_len] + yy[g]
        return carry

    for c in range(n_chunks):
        chunk_body(c, 0)

    y = y_ref[...]
    inv_n = 1.0 / head_dim
    mean = head_sum(y) * inv_n
    d = y - mean
    var = head_sum(d * d) * inv_n
    yn = d * lax.rsqrt(var + LNX_EPS) * lg_ref[...] + lb_ref[...]
    bonus = head_sum(rk2_ref[...]) * v32_ref[...]
    ya_ref[0] = ((yn + bonus) * gate_ref[...]).astype(ya_ref.dtype)

    @pl.when(t == pl.num_programs(1) - 1)
    def _fin():
        for h in range(heads):
            g, hh = divmod(h, hpg)
            lo = hh * head_dim
            so_ref[0, h] = s_ref[g, lo:lo + head_dim, lo:lo + head_dim]


def _group_major(a, aw):
    lead = a.shape[:-1]
    bw = min(aw, WKV_BLOCK_LANES)
    g = aw // bw
    rkv = a[..., :3 * aw].reshape(*lead, 3, g, bw)
    rkv = jnp.swapaxes(rkv, -3, -2).reshape(*lead, 3 * aw)
    return jnp.concatenate([a[..., 3 * aw:], rkv], axis=-1)


def _group_major_inverse(a, aw):
    lead = a.shape[:-1]
    bw = min(aw, WKV_BLOCK_LANES)
    g = aw // bw
    lw = a.shape[-1] - 3 * aw
    rkv = a[..., lw:].reshape(*lead, g, 3, bw)
    rkv = jnp.swapaxes(rkv, -3, -2).reshape(*lead, 3 * aw)
    return jnp.concatenate([rkv, a[..., :lw]], axis=-1)


def _wkv_branch(x, gn, wa, shift0, s0, mu, w0, wdup, a0, waup, wgup, k_k, k_a, r_k, lnx_g, lnx_b):
    b, t, d = x.shape
    asw = wa.shape[1]
    heads, head_dim = s0.shape[1], s0.shape[2]
    aw = heads * head_dim
    ranks = (wdup.shape[0], waup.shape[0], wgup.shape[0])
    tt = _tile(t, 512, WKV_CHUNK)
    c_len = min(WKV_CHUNK, tt)
    n_chunks = tt // c_len
    groups = aw // WKV_GROUP_LANES
    row = lambda a: a.reshape(1, -1)
    const = lambda shape: pl.BlockSpec(shape, lambda i, j: (0,) * len(shape))
    kern = functools.partial(_wkv_kernel, heads=heads, head_dim=head_dim, ranks=ranks)
    return pl.pallas_call(
        kern,
        grid=(b, t // tt),
        in_specs=[
            pl.BlockSpec((1, tt, d), lambda i, j: (i, j, 0)),
            const((1, d)),
            pl.BlockSpec(wa.shape, lambda i, j: (0, 0), pipeline_mode=pl.Buffered(1)),
            pl.BlockSpec((1, 1, asw), lambda i, j: (i, 0, 0)),
            pl.BlockSpec((1, heads, head_dim, head_dim), lambda i, j: (i, 0, 0, 0)),
            const((1, asw)), const((1, aw)), const(wdup.shape), const((1, aw)), const(waup.shape),
            const(wgup.shape), const((1, aw)), const((1, aw)), const((1, aw)), const((1, aw)),
            const((1, aw)),
        ],
        out_specs=[
            pl.BlockSpec((1, tt, aw), lambda i, j: (i, j, 0)),
            pl.BlockSpec((1, 1, asw), lambda i, j: (i, 0, 0)),
            pl.BlockSpec((1, heads, head_dim, head_dim), lambda i, j: (i, 0, 0, 0)),
        ],
        out_shape=[
            jax.ShapeDtypeStruct((b, t, aw), BF16),
            jax.ShapeDtypeStruct((b, 1, asw), F32),
            jax.ShapeDtypeStruct((b, heads, head_dim, head_dim), F32),
        ],
        scratch_shapes=[
            pltpu.VMEM((1, asw), F32),
            pltpu.VMEM((groups, WKV_GROUP_LANES, WKV_GROUP_LANES), F32),
            pltpu.VMEM((n_chunks, 2 * c_len, aw), BF16),
            pltpu.VMEM((n_chunks, 2 * c_len, aw), BF16),
            pltpu.VMEM((n_chunks, 2 * c_len, aw), BF16),
            pltpu.VMEM((tt, aw), BF16),
            pltpu.VMEM((tt, aw), F32),
            pltpu.VMEM((n_chunks, 1, aw), F32),
            pltpu.VMEM((n_chunks, groups, c_len, WKV_GROUP_LANES), BF16),
            pltpu.VMEM((n_chunks, groups, c_len, WKV_GROUP_LANES), F32),
            pltpu.VMEM((n_chunks, groups, c_len, 2 * WKV_GROUP_LANES), BF16),
            pltpu.VMEM((tt, aw), F32),
            pltpu.VMEM((tt, aw), F32),
            pltpu.VMEM((tt, aw), F32),
        ],
        compiler_params=_params("arbitrary", "arbitrary"),
        name="wkv7_chunked",
    )(x, gn, wa, shift0.reshape(b, 1, asw), s0, row(mu), row(w0), wdup, row(a0), waup, wgup,
      row(k_k), row(k_a), row(r_k), row(lnx_g), row(lnx_b))


def _ret_proj_kernel(x_ref, g_ref, w_ref, cos_ref, sin_ref, r0_ref, gn_ref, yb_ref, ro_ref,
                     st_ref, dm_ref, qd_ref, kd_ref, *, heads, dk, dv, blk, steps_per_seq):
    tm = x_ref.shape[0]
    hw = 2 * dk + 2 * dv
    i = pl.program_id(0)
    log_gammas = [math.log1p(-(2.0 ** (-5.0 - h))) for h in range(heads)]

    @pl.when(i == 0)
    def _tables():
        diff = (lax.broadcasted_iota(jnp.int32, (blk, blk), 0)
                - lax.broadcasted_iota(jnp.int32, (blk, blk), 1)).astype(F32)
        pos = lax.broadcasted_iota(jnp.int32, (blk, dk), 0).astype(F32)
        for h in range(heads):
            lg = log_gammas[h]
            dm_ref[h] = jnp.where(diff >= 0, jnp.exp(jnp.maximum(diff, 0.0) * lg), 0.0)
            qd_ref[h] = jnp.exp((pos + 1.0) * lg) * (dk ** -0.5)
            kd_ref[h] = jnp.exp((blk - 1.0 - pos) * lg)

    @pl.when(i % steps_per_seq == 0)
    def _init():
        st_ref[...] = r0_ref[0]

    hn = _rms(x_ref[...], g_ref[...]).astype(BF16)
    for h in range(heads):
        zh = _dot(hn, w_ref[:, h * hw:(h + 1) * hw])
        for s in range(tm // blk):
            rows = slice(s * blk, (s + 1) * blk)
            cosv = cos_ref[rows, :]
            sinv = sin_ref[rows, :]
            qh = zh[rows, 0:dk]
            kh = zh[rows, dk:2 * dk]
            vh = zh[rows, 2 * dk:2 * dk + dv].astype(BF16)
            gate = zh[rows, 2 * dk + dv:hw]
            qrot = qh * cosv + pltpu.roll(qh, dk // 2, 1) * sinv
            kr = kh * cosv + pltpu.roll(kh, dk // 2, 1) * sinv
            scores = _dot_nt((qrot * (dk ** -0.5)).astype(BF16), kr.astype(BF16)) * dm_ref[h]
            inner = _dot(scores.astype(BF16), vh)
            st = st_ref[h]
            cross = _dot((qrot * qd_ref[h]).astype(BF16), st.astype(BF16))
            ke = kr * kd_ref[h]
            st_ref[h] = math.exp(blk * log_gammas[h]) * st + _dot_tn(ke.astype(BF16), vh)
            o = inner + cross
            mean = jnp.mean(o, axis=-1, keepdims=True)
            d = o - mean
            var = jnp.mean(d * d, axis=-1, keepdims=True)
            on = d * lax.rsqrt(var + GN_EPS) * gn_ref[:, h * dv:(h + 1) * dv]
            yb_ref[rows, h * dv:(h + 1) * dv] = (jax.nn.silu(gate) * on).astype(yb_ref.dtype)

    @pl.when(i % steps_per_seq == steps_per_seq - 1)
    def _fin():
        ro_ref[0] = st_ref[...]


def _ret_proj(x, g, w_heads, cos_t, sin_t, r0, gn_g, seq_len):
    m, d = x.shape
    heads, dk, dv = r0.shape[1], r0.shape[2], r0.shape[3]
    vw = heads * dv
    tm = _tile(seq_len, 512, 64)
    blk = _tile(tm, 256, 64)
    sps = seq_len // tm
    kern = functools.partial(_ret_proj_kernel, heads=heads, dk=dk, dv=dv, blk=blk, steps_per_seq=sps)
    return pl.pallas_call(
        kern,
        grid=(m // tm,),
        in_specs=[
            pl.BlockSpec((tm, d), lambda i: (i, 0)),
            pl.BlockSpec((1, d), lambda i: (0, 0)),
            pl.BlockSpec(w_heads.shape, lambda i: (0, 0), pipeline_mode=pl.Buffered(1)),
            pl.BlockSpec((tm, dk), lambda i: (i % sps, 0)),
            pl.BlockSpec((tm, dk), lambda i: (i % sps, 0)),
            pl.BlockSpec((1, heads, dk, dv), lambda i: (i // sps, 0, 0, 0)),
            pl.BlockSpec((1, vw), lambda i: (0, 0)),
        ],
        out_specs=[
            pl.BlockSpec((tm, vw), lambda i: (i, 0)),
            pl.BlockSpec((1, heads, dk, dv), lambda i: (i // sps, 0, 0, 0)),
        ],
        out_shape=[
            jax.ShapeDtypeStruct((m, vw), BF16),
            jax.ShapeDtypeStruct((m // seq_len, heads, dk, dv), F32),
        ],
        scratch_shapes=[pltpu.VMEM((heads, dk, dv), F32), pltpu.VMEM((heads, blk, blk), F32),
                        pltpu.VMEM((heads, blk, dk), F32), pltpu.VMEM((heads, blk, dk), F32)],
        compiler_params=_params("arbitrary", vmem_limit=VMEM_LIMIT_RESIDENT_BYTES),
        name="retention_proj",
    )(x, g, w_heads, cos_t, sin_t, r0, gn_g.reshape(1, -1))


def _merge_kernel(ya_ref, yb_ref, zg_ref, x_ref, wba_ref, wbb_ref, wo_ref, g_ref, o_ref):
    d = x_ref.shape[1]
    ga = jax.nn.sigmoid(zg_ref[:, 0:d].astype(F32))
    gb = jax.nn.sigmoid(zg_ref[:, d:2 * d].astype(F32))
    merged = ga * _dot(ya_ref[...], wba_ref[...]) + gb * _dot(yb_ref[...], wbb_ref[...])
    mo = _dot(merged.astype(BF16), wo_ref[...])
    o_ref[...] = x_ref[...] + _rms(mo, g_ref[...])


def _merge(ya, yb, zg, x, wba, wbb, wo, g):
    m, d = x.shape
    tm = _tile(m, 256, 8)
    tok = lambda w: pl.BlockSpec((tm, w), lambda i: (i, 0))
    res = lambda a: pl.BlockSpec(a.shape, lambda i: (0, 0), pipeline_mode=pl.Buffered(1))
    return pl.pallas_call(
        _merge_kernel,
        grid=(m // tm,),
        in_specs=[tok(ya.shape[1]), tok(yb.shape[1]), tok(zg.shape[1]), tok(d),
                  res(wba), res(wbb), res(wo), pl.BlockSpec((1, d), lambda i: (0, 0))],
        out_specs=tok(d),
        out_shape=jax.ShapeDtypeStruct((m, d), F32),
        compiler_params=_params("arbitrary"),
        name="merge_out",
    )(ya, yb, zg, x, wba, wbb, wo, g)


FFN_CHUNK = V7X_MXU_DIM
FFN_PASS_CHUNKS = 6


def _ffn_pass_kernel(x_ref, gpre_ref, wg_ref, wu_ref, wo_ref, *rest, first, last):
    rest = list(rest)
    acc = None if first else rest.pop(0)[...]
    gpost_ref = rest.pop(0) if last else None
    o_ref = rest.pop(0)
    x = x_ref[...]
    h = _rms(x, gpre_ref[...]).astype(BF16)
    for c in range(wg_ref.shape[1] // FFN_CHUNK):
        cols = slice(c * FFN_CHUNK, (c + 1) * FFN_CHUNK)
        act = jax.nn.silu(_dot(h, wg_ref[:, cols])) * _dot(h, wu_ref[:, cols])
        part = _dot(act.astype(BF16), wo_ref[cols, :])
        acc = part if acc is None else acc + part
    o_ref[...] = x + _rms(acc, gpost_ref[...]) if last else acc


def _ffn(x, gpre, ffn_slices, gpost):
    m, d = x.shape
    tm = _tile(m, 512, 8)
    tok = pl.BlockSpec((tm, d), lambda i: (i, 0))
    vec = pl.BlockSpec((1, d), lambda i: (0, 0))
    res = lambda a: pl.BlockSpec(a.shape, lambda i: (0, 0), pipeline_mode=pl.Buffered(1))
    acc = None
    for p, (wg, wu, wo) in enumerate(ffn_slices):
        first, last = p == 0, p == len(ffn_slices) - 1
        args = [x, gpre, wg, wu, wo] + ([] if first else [acc]) + ([gpost] if last else [])
        specs = [tok, vec, res(wg), res(wu), res(wo)] + ([] if first else [tok]) + ([vec] if last else [])
        acc = pl.pallas_call(
            functools.partial(_ffn_pass_kernel, first=first, last=last),
            grid=(m // tm,),
            in_specs=specs,
            out_specs=tok,
            out_shape=jax.ShapeDtypeStruct((m, d), F32),
            compiler_params=_params("arbitrary"),
            name="swiglu_ffn_pass",
        )(*args)
    return acc


def _ffn_slices(w_fi, w_fo):
    f = w_fo.shape[0]
    n_chunks = f // FFN_CHUNK
    assert n_chunks * FFN_CHUNK == f
    n_pass = -(-n_chunks // FFN_PASS_CHUNKS)
    sizes = [n_chunks // n_pass + (1 if p < n_chunks % n_pass else 0) for p in range(n_pass)]
    out, off = [], 0
    for sz in sizes:
        w = sz * FFN_CHUNK
        out.append((w_fi[:, off:off + w].astype(BF16), w_fi[:, f + off:f + off + w].astype(BF16),
                    w_fo[off:off + w, :].astype(BF16)))
        off += w
    return out


def _rope_tables(pos, dk):
    half = dk // 2
    inv = ROPE_BASE ** (-jnp.arange(half, dtype=F32) / half)
    ang = pos[:, None] * inv[None, :]
    cos = jnp.cos(ang)
    sin = jnp.sin(ang)
    return jnp.concatenate([cos, cos], axis=-1), jnp.concatenate([-sin, sin], axis=-1)


def _head_major(w_b, heads, dk, dv):
    d = w_b.shape[0]
    qkw, vw = heads * dk, heads * dv
    parts = [w_b[:, 0:qkw].reshape(d, heads, dk), w_b[:, qkw:2 * qkw].reshape(d, heads, dk),
             w_b[:, 2 * qkw:2 * qkw + vw].reshape(d, heads, dv),
             w_b[:, 2 * qkw + vw:].reshape(d, heads, dv)]
    return jnp.concatenate(parts, axis=2).reshape(d, heads * (2 * dk + 2 * dv))


def _layer(x, rope, shift0, s0, r0, w):
    b, t, d = x.shape
    x2 = x.reshape(b * t, d)
    row = lambda a: a.reshape(1, -1)
    zg = _norm_matmul(x2, row(w["nmp"]), w["w_g"], BF16)
    aw = w["w_dup"].shape[1]
    ya, shift, s_new = _wkv_branch(x, row(w["nmp"]), w["w_a"], _group_major(shift0, aw), s0, w["mu"],
                                   w["w0"], w["w_dup"], w["a0"], w["w_aup"], w["w_gup"], w["k_k"],
                                   w["k_a"], w["r_k"], w["lnx_g"], w["lnx_b"])
    shift = _group_major_inverse(shift.reshape(b, -1), aw)
    yb, r_new = _ret_proj(x2, row(w["nmp"]), w["w_b"], rope[0], rope[1], r0, w["ret_g"], t)
    x1 = _merge(ya.reshape(b * t, -1), yb, zg, x2, w["w_ba"], w["w_bb"],
                w["w_out"], row(w["nmq"]))
    xo = _ffn(x1, row(w["nfp"]), w["ffn"], row(w["nfq"]))
    return xo.reshape(b, t, d), shift, s_new, r_new


def kernel(x_prompt, x_sample, state_wkv, state_ret, state_shift, norm_mix_pre, norm_mix_post, norm_ffn_pre, norm_ffn_post, w_in, shift_mu, decay_base, w_decay_up, iclr_base, w_iclr_up, w_gate_up, key_k, key_a, bonus_rk, lnx_gain, lnx_bias, w_branch_a, ret_norm_gain, w_branch_b, w_out, w_ffn_in, w_ffn_out):
    depth = w_in.shape[0]
    bp, tp, _ = x_prompt.shape
    bs, ts, _ = x_sample.shape
    asw = shift_mu.shape[1]
    heads_a, hd_a = bonus_rk.shape[1], bonus_rk.shape[2]
    heads_b, dk, dv = state_ret.shape[2], state_ret.shape[3], state_ret.shape[4]
    b_in = 2 * heads_b * dk + 2 * heads_b * dv
    dt = x_prompt.dtype

    rope_p = _rope_tables(jnp.arange(tp, dtype=F32), dk)
    rope_s = _rope_tables(jnp.arange(ts, dtype=F32) + jnp.float32(PAST_LEN), dk)
    zero_shift = jnp.zeros((bp, asw), dt)
    zero_wkv = jnp.zeros((bp, heads_a, hd_a, hd_a), dt)
    zero_ret = jnp.zeros((bp, heads_b, dk, dv), dt)

    xp, xs = x_prompt, x_sample
    outs = [[] for _ in range(6)]
    for l in range(depth):
        w = dict(
            nmp=norm_mix_pre[l], nmq=norm_mix_post[l], nfp=norm_ffn_pre[l], nfq=norm_ffn_post[l],
            w_a=_group_major(w_in[l, :, :asw], heads_a * hd_a).astype(BF16),
            w_b=_head_major(w_in[l, :, asw:asw + b_in], heads_b, dk, dv).astype(BF16),
            w_g=w_in[l, :, asw + b_in:].astype(BF16),
            mu=_group_major(shift_mu[l], heads_a * hd_a), w0=decay_base[l], w_dup=w_decay_up[l].astype(BF16), a0=iclr_base[l],
            w_aup=w_iclr_up[l].astype(BF16), w_gup=w_gate_up[l].astype(BF16), k_k=key_k[l],
            k_a=key_a[l], r_k=bonus_rk[l], lnx_g=lnx_gain[l], lnx_b=lnx_bias[l],
            w_ba=w_branch_a[l].astype(BF16), ret_g=ret_norm_gain[l],
            w_bb=w_branch_b[l].astype(BF16), w_out=w_out[l].astype(BF16),
            ffn=_ffn_slices(w_ffn_in[l], w_ffn_out[l]),
        )
        xp, sp, wkp, rtp = _layer(xp, rope_p, zero_shift, zero_wkv, zero_ret, w)
        xs, ss, wks, rts = _layer(xs, rope_s, state_shift[l], state_wkv[l], state_ret[l], w)
        for lst, val in zip(outs, (wkp, rtp, sp, wks, rts, ss)):
            lst.append(val)
    return (xp, xs) + tuple(jnp.stack(o) for o in outs)
```

```python
import functools
import math

import jax
import jax.numpy as jnp
from jax import lax
from jax.experimental import pallas as pl
from jax.experimental.pallas import tpu as pltpu

F32 = jnp.float32
BF16 = jnp.bfloat16

PAST_LEN = 4096
RMS_EPS = 1e-6
LNX_EPS = 64e-5
GN_EPS = 1e-5
ROPE_BASE = 10000.0

V7X_LANES = 128
V7X_MXU_DIM = 256
WKV_GROUP_LANES = V7X_LANES
WKV_BLOCK_LANES = V7X_MXU_DIM
WKV_CHUNK = 64
WKV_PREP_BATCH = 2
V7X_VMEM_BYTES = 64 * 1024 * 1024
VMEM_LIMIT_BYTES = V7X_VMEM_BYTES * 7 // 8
VMEM_LIMIT_RESIDENT_BYTES = V7X_VMEM_BYTES * 15 // 16


def _dot(a, b):
    return jnp.dot(a, b, preferred_element_type=F32)


def _dot_nt(a, b):
    return lax.dot_general(a, b, (((1,), (1,)), ((), ())), preferred_element_type=F32)


def _dot_tn(a, b):
    return lax.dot_general(a, b, (((0,), (0,)), ((), ())), preferred_element_type=F32)


def _rms(x, g):
    return x * lax.rsqrt(jnp.mean(x * x, axis=-1, keepdims=True) + RMS_EPS) * g


def _tile(n, target, mult=V7X_LANES):
    if n <= target:
        return n
    best = None
    for t in range(mult, target + 1, mult):
        if n % t == 0:
            best = t
    assert best is not None, (n, target, mult)
    return best


def _params(*sem, vmem_limit=VMEM_LIMIT_BYTES):
    return pltpu.CompilerParams(dimension_semantics=sem, vmem_limit_bytes=vmem_limit)


def _norm_matmul_kernel(x_ref, g_ref, w_ref, o_ref, *, tn):
    h = _rms(x_ref[...], g_ref[...]).astype(BF16)
    for j in range(w_ref.shape[1] // tn):
        cols = slice(j * tn, (j + 1) * tn)
        o_ref[:, cols] = _dot(h, w_ref[:, cols]).astype(o_ref.dtype)


def _norm_matmul(x, g, w, out_dtype, tm_target=512, tn_target=512):
    m, d = x.shape
    n = w.shape[1]
    tm = _tile(m, tm_target, 8)
    tn = _tile(n, tn_target)
    return pl.pallas_call(
        functools.partial(_norm_matmul_kernel, tn=tn),
        grid=(m // tm,),
        in_specs=[
            pl.BlockSpec((tm, d), lambda i: (i, 0)),
            pl.BlockSpec((1, d), lambda i: (0, 0)),
            pl.BlockSpec((d, n), lambda i: (0, 0), pipeline_mode=pl.Buffered(1)),
        ],
        out_specs=pl.BlockSpec((tm, n), lambda i: (i, 0)),
        out_shape=jax.ShapeDtypeStruct((m, n), out_dtype),
        compiler_params=_params("arbitrary"),
        name="norm_matmul",
    )(x, g, w)


def _split_bf16(x):
    hi = x.astype(BF16)
    lo = (x - hi.astype(F32)).astype(BF16)
    return hi, lo


def _wkv_kernel(x_ref, gn_ref, wa_ref, sh0_ref, s0_ref, mu_ref, w0_ref, wdup_ref, a0_ref, waup_ref,
                wgup_ref, kk_ref, ka_ref, rk_ref, lg_ref, lb_ref,
                ya_ref, sho_ref, so_ref,
                carry_ref, s_ref, ar_ref, bk_ref, bkh_ref, v_ref, y_ref, pc_ref, t_ref, av_ref, wcat_ref,
                v32_ref, rk2_ref, gate_ref,
                *, heads, head_dim, ranks):
    tt = x_ref.shape[1]
    aw = heads * head_dim
    gw = WKV_GROUP_LANES
    hpg = gw // head_dim
    groups = aw // gw
    c_len = min(WKV_CHUNK, tt)
    n_chunks = tt // c_len
    dr, ir, gr = ranks
    t = pl.program_id(1)

    @pl.when(t == 0)
    def _init():
        carry_ref[...] = sh0_ref[0]
        s_ref[...] = jnp.zeros_like(s_ref)
        for h in range(heads):
            g, hh = divmod(h, hpg)
            lo = hh * head_dim
            s_ref[g, lo:lo + head_dim, lo:lo + head_dim] = s0_ref[0, h]

    hn = _rms(x_ref[0], gn_ref[...]).astype(BF16)
    lw = dr + ir + gr

    def project_shifted(cols):
        z = _dot(hn, wa_ref[:, cols])
        rolled = pltpu.roll(z, 1, 0)
        row8 = lax.broadcasted_iota(jnp.int32, (8, z.shape[1]), 0)
        prev = jnp.concatenate([jnp.where(row8 == 0, carry_ref[:, cols], rolled[0:8]), rolled[8:]],
                               axis=0)
        last = z[tt - 1:tt, :]
        carry_ref[:, cols] = last
        sho_ref[0, :, cols] = last
        return z + mu_ref[:, cols] * (prev - z)

    zl = project_shifted(slice(0, lw))
    wd = zl[:, 0:dr]
    ad = zl[:, dr:dr + ir]
    gd = zl[:, dr + ir:lw]
    xw = w0_ref[...] + _dot(jnp.tanh(wd).astype(BF16), wdup_ref[...])
    ld = (-math.exp(-0.5)) * jax.nn.sigmoid(xw)
    a_ic = jax.nn.sigmoid(a0_ref[...] + _dot(ad.astype(BF16), waup_ref[...]))
    gate_ref[...] = _dot(jax.nn.sigmoid(gd).astype(BF16), wgup_ref[...])

    gi = lax.broadcasted_iota(jnp.int32, (gw, gw), 0) // head_dim
    gj = lax.broadcasted_iota(jnp.int32, (gw, gw), 1) // head_dim
    same_head = gi == gj

    sw = min(aw, WKV_BLOCK_LANES)
    si = lax.broadcasted_iota(jnp.int32, (sw, sw), 0) // head_dim
    sj = lax.broadcasted_iota(jnp.int32, (sw, sw), 1) // head_dim
    ones_sw = (si == sj).astype(BF16)

    def head_sum(x):
        xb = x.astype(BF16)
        parts = [_dot(xb[:, q * sw:(q + 1) * sw], ones_sw) for q in range(aw // sw)]
        return jnp.concatenate(parts, axis=1) if len(parts) > 1 else parts[0]

    tb = min(tt, V7X_MXU_DIM)
    ti = lax.broadcasted_iota(jnp.int32, (tb, tb), 0)
    tj = lax.broadcasted_iota(jnp.int32, (tb, tb), 1)
    tri = (((ti // c_len) == (tj // c_len)) & (tj <= ti)).astype(BF16)
    ld_hi, ld_lo = _split_bf16(ld)
    cum_all = jnp.concatenate(
        [_dot(tri, ld_hi[i * tb:(i + 1) * tb]) + _dot(tri, ld_lo[i * tb:(i + 1) * tb])
         for i in range(tt // tb)], axis=0)

    for q in range(aw // sw):
        gl = slice(q * sw, (q + 1) * sw)
        z3 = project_shifted(slice(lw + 3 * q * sw, lw + 3 * (q + 1) * sw))
        r = z3[:, 0:sw]
        k = z3[:, sw:2 * sw]
        v = z3[:, 2 * sw:3 * sw]
        a_g = a_ic[:, gl]
        ld_g = ld[:, gl]
        kk = k * kk_ref[:, gl]
        k2 = k * (1.0 + (a_g - 1.0) * ka_ref[:, gl])
        kkn = kk * jnp.minimum(lax.rsqrt(_dot((kk * kk).astype(BF16), ones_sw)), 1e12)
        cum = cum_all[:, gl]
        p_inc = jnp.exp(cum)
        p_inv = jnp.exp(-cum)
        a_t = -(kkn * jnp.exp(cum - ld_g))
        r_t = r * p_inc
        b_t = (kkn * a_g) * p_inv
        k_t = k2 * p_inv
        for c in range(n_chunks):
            rs = slice(c * c_len, (c + 1) * c_len)
            p_end = p_inc[(c + 1) * c_len - 1:(c + 1) * c_len, :]
            ar_ref[c, 0:c_len, gl] = a_t[rs].astype(BF16)
            ar_ref[c, c_len:2 * c_len, gl] = r_t[rs].astype(BF16)
            bk_ref[c, 0:c_len, gl] = b_t[rs].astype(BF16)
            bk_ref[c, c_len:2 * c_len, gl] = k_t[rs].astype(BF16)
            bkh_ref[c, 0:c_len, gl] = (b_t[rs] * p_end).astype(BF16)
            bkh_ref[c, c_len:2 * c_len, gl] = (k_t[rs] * p_end).astype(BF16)
            pc_ref[c, :, gl] = p_end
        v_ref[:, gl] = v.astype(BF16)
        v32_ref[:, gl] = v
        rk2_ref[:, gl] = r * k2 * rk_ref[:, gl]

    lane_head = lax.broadcasted_iota(jnp.int32, (1, gw), 1) // head_dim
    ci = lax.broadcasted_iota(jnp.int32, (c_len, gw), 0)
    cj = lax.broadcasted_iota(jnp.int32, (c_len, gw), 1) % head_dim
    assert c_len == head_dim
    strict = cj < ci
    incl = cj <= ci
    eye = (cj == ci).astype(F32)
    n_doubling = int(math.log2(c_len))

    def bd(m):
        zero = jnp.zeros_like(m)
        return jnp.concatenate([jnp.where(lane_head == h, m, zero) for h in range(hpg)], axis=0)

    gs = range(groups)
    sls = [slice(g * gw, (g + 1) * gw) for g in gs]

    def prep(chunks):
        items = [(c, g) for c in chunks for g in gs]
        n = range(len(items))
        ar = [ar_ref[c, :, sls[g]] for c, g in items]
        bk = [bk_ref[c, :, sls[g]] for c, g in items]
        vv = [v_ref[c * c_len:(c + 1) * c_len, sls[g]] for c, g in items]
        s4 = [_dot_nt(ar[i], jnp.concatenate([bd(bk[i][0:c_len]), bd(bk[i][c_len:2 * c_len])],
                                              axis=0)) for i in n]
        a_ab = [jnp.where(strict, s4[i][0:c_len, 0:gw], 0.0) for i in n]
        a_ak = [jnp.where(strict, s4[i][0:c_len, gw:2 * gw], 0.0).astype(BF16) for i in n]
        for i, (c, g) in enumerate(items):
            wcat_ref[c, g, :, 0:gw] = jnp.where(incl, s4[i][c_len:2 * c_len, 0:gw], 0.0).astype(BF16)
            wcat_ref[c, g, :, gw:2 * gw] = jnp.where(incl, s4[i][c_len:2 * c_len, gw:2 * gw],
                                                     0.0).astype(BF16)
        av = [_dot(a_ak[i], bd(vv[i])) for i in n]
        for i, (c, g) in enumerate(items):
            av_ref[c, g] = av[i]
        tk = [eye + a_ab[i] for i in n]
        mb = [a_ab[i].astype(BF16) for i in n]
        mk = [_dot(mb[i], bd(mb[i])) for i in n]
        for step in range(1, n_doubling - 1):
            mb = [mk[i].astype(BF16) for i in n]
            rr = [_dot(mb[i], jnp.concatenate([bd(mb[i]), bd(tk[i].astype(BF16))], axis=1))
                  for i in n]
            mk = [rr[i][:, 0:gw] for i in n]
            tk = [tk[i] + rr[i][:, gw:2 * gw] for i in n]
        tk = [tk[i] + _dot(mk[i].astype(BF16), bd(tk[i].astype(BF16))) for i in n]
        for i, (c, g) in enumerate(items):
            t_ref[c, g] = tk[i].astype(BF16)

    def recur(c):
        r0 = c * c_len
        ar = [ar_ref[c, :, sl] for sl in sls]
        vv = [v_ref[r0:r0 + c_len, sl] for sl in sls]
        s0 = [s_ref[g] for g in gs]
        m1 = [_dot_nt(ar[g], s0[g].astype(BF16)) for g in gs]
        xb = [(m1[g][0:c_len] + av_ref[c, g]).astype(BF16) for g in gs]
        ub = [_dot(t_ref[c, g], bd(xb[g])).astype(BF16) for g in gs]
        ds = [_dot_tn(jnp.concatenate([ub[g], vv[g]], axis=0), bkh_ref[c, :, sls[g]]) for g in gs]
        for g in gs:
            s_ref[g] = s0[g] * pc_ref[c][:, sls[g]] + jnp.where(same_head, ds[g], 0.0)
        yy = [_dot(wcat_ref[c, g], jnp.concatenate([bd(ub[g]), bd(vv[g])], axis=0)) for g in gs]
        for g in gs:
            y_ref[r0:r0 + c_len, sls[g]] = m1[g][c_len:2 * c_len] + yy[g]

    for c0 in range(0, n_chunks, WKV_PREP_BATCH):
        prep(range(c0, min(c0 + WKV_PREP_BATCH, n_chunks)))
    for c in range(n_chunks):
        recur(c)

    y = y_ref[...]
    inv_n = 1.0 / head_dim
    mean = head_sum(y) * inv_n
    d = y - mean
    var = head_sum(d * d) * inv_n
    yn = d * lax.rsqrt(var + LNX_EPS) * lg_ref[...] + lb_ref[...]
    bonus = head_sum(rk2_ref[...]) * v32_ref[...]
    ya_ref[0] = ((yn + bonus) * gate_ref[...]).astype(ya_ref.dtype)

    @pl.when(t == pl.num_programs(1) - 1)
    def _fin():
        for h in range(heads):
            g, hh = divmod(h, hpg)
            lo = hh * head_dim
            so_ref[0, h] = s_ref[g, lo:lo + head_dim, lo:lo + head_dim]


def _group_major(a, aw):
    lead = a.shape[:-1]
    bw = min(aw, WKV_BLOCK_LANES)
    g = aw // bw
    rkv = a[..., :3 * aw].reshape(*lead, 3, g, bw)
    rkv = jnp.swapaxes(rkv, -3, -2).reshape(*lead, 3 * aw)
    return jnp.concatenate([a[..., 3 * aw:], rkv], axis=-1)


def _group_major_inverse(a, aw):
    lead = a.shape[:-1]
    bw = min(aw, WKV_BLOCK_LANES)
    g = aw // bw
    lw = a.shape[-1] - 3 * aw
    rkv = a[..., lw:].reshape(*lead, g, 3, bw)
    rkv = jnp.swapaxes(rkv, -3, -2).reshape(*lead, 3 * aw)
    return jnp.concatenate([rkv, a[..., :lw]], axis=-1)


def _wkv_branch(x, gn, wa, shift0, s0, mu, w0, wdup, a0, waup, wgup, k_k, k_a, r_k, lnx_g, lnx_b):
    b, t, d = x.shape
    asw = wa.shape[1]
    heads, head_dim = s0.shape[1], s0.shape[2]
    aw = heads * head_dim
    ranks = (wdup.shape[0], waup.shape[0], wgup.shape[0])
    tt = _tile(t, 512, WKV_CHUNK)
    c_len = min(WKV_CHUNK, tt)
    n_chunks = tt // c_len
    groups = aw // WKV_GROUP_LANES
    row = lambda a: a.reshape(1, -1)
    const = lambda shape: pl.BlockSpec(shape, lambda i, j: (0,) * len(shape))
    kern = functools.partial(_wkv_kernel, heads=heads, head_dim=head_dim, ranks=ranks)
    return pl.pallas_call(
        kern,
        grid=(b, t // tt),
        in_specs=[
            pl.BlockSpec((1, tt, d), lambda i, j: (i, j, 0)),
            const((1, d)),
            pl.BlockSpec(wa.shape, lambda i, j: (0, 0), pipeline_mode=pl.Buffered(1)),
            pl.BlockSpec((1, 1, asw), lambda i, j: (i, 0, 0)),
            pl.BlockSpec((1, heads, head_dim, head_dim), lambda i, j: (i, 0, 0, 0)),
            const((1, asw)), const((1, aw)), const(wdup.shape), const((1, aw)), const(waup.shape),
            const(wgup.shape), const((1, aw)), const((1, aw)), const((1, aw)), const((1, aw)),
            const((1, aw)),
        ],
        out_specs=[
            pl.BlockSpec((1, tt, aw), lambda i, j: (i, j, 0)),
            pl.BlockSpec((1, 1, asw), lambda i, j: (i, 0, 0)),
            pl.BlockSpec((1, heads, head_dim, head_dim), lambda i, j: (i, 0, 0, 0)),
        ],
        out_shape=[
            jax.ShapeDtypeStruct((b, t, aw), BF16),
            jax.ShapeDtypeStruct((b, 1, asw), F32),
            jax.ShapeDtypeStruct((b, heads, head_dim, head_dim), F32),
        ],
        scratch_shapes=[
            pltpu.VMEM((1, asw), F32),
            pltpu.VMEM((groups, WKV_GROUP_LANES, WKV_GROUP_LANES), F32),
            pltpu.VMEM((n_chunks, 2 * c_len, aw), BF16),
            pltpu.VMEM((n_chunks, 2 * c_len, aw), BF16),
            pltpu.VMEM((n_chunks, 2 * c_len, aw), BF16),
            pltpu.VMEM((tt, aw), BF16),
            pltpu.VMEM((tt, aw), F32),
            pltpu.VMEM((n_chunks, 1, aw), F32),
            pltpu.VMEM((n_chunks, groups, c_len, WKV_GROUP_LANES), BF16),
            pltpu.VMEM((n_chunks, groups, c_len, WKV_GROUP_LANES), F32),
            pltpu.VMEM((n_chunks, groups, c_len, 2 * WKV_GROUP_LANES), BF16),
            pltpu.VMEM((tt, aw), F32),
            pltpu.VMEM((tt, aw), F32),
            pltpu.VMEM((tt, aw), F32),
        ],
        compiler_params=_params("arbitrary", "arbitrary"),
        name="wkv7_chunked",
    )(x, gn, wa, shift0.reshape(b, 1, asw), s0, row(mu), row(w0), wdup, row(a0), waup, wgup,
      row(k_k), row(k_a), row(r_k), row(lnx_g), row(lnx_b))


def _ret_proj_kernel(x_ref, g_ref, w_ref, cos_ref, sin_ref, r0_ref, gn_ref, yb_ref, ro_ref,
                     st_ref, dm_ref, qd_ref, kd_ref, *, heads, dk, dv, blk, steps_per_seq):
    tm = x_ref.shape[0]
    hw = 2 * dk + 2 * dv
    i = pl.program_id(0)
    log_gammas = [math.log1p(-(2.0 ** (-5.0 - h))) for h in range(heads)]

    @pl.when(i == 0)
    def _tables():
        diff = (lax.broadcasted_iota(jnp.int32, (blk, blk), 0)
                - lax.broadcasted_iota(jnp.int32, (blk, blk), 1)).astype(F32)
        pos = lax.broadcasted_iota(jnp.int32, (blk, dk), 0).astype(F32)
        for h in range(heads):
            lg = log_gammas[h]
            dm_ref[h] = jnp.where(diff >= 0, jnp.exp(jnp.maximum(diff, 0.0) * lg), 0.0)
            qd_ref[h] = jnp.exp((pos + 1.0) * lg) * (dk ** -0.5)
            kd_ref[h] = jnp.exp((blk - 1.0 - pos) * lg)

    @pl.when(i % steps_per_seq == 0)
    def _init():
        st_ref[...] = r0_ref[0]

    hn = _rms(x_ref[...], g_ref[...]).astype(BF16)
    for h in range(heads):
        zh = _dot(hn, w_ref[:, h * hw:(h + 1) * hw])
        for s in range(tm // blk):
            rows = slice(s * blk, (s + 1) * blk)
            cosv = cos_ref[rows, :]
            sinv = sin_ref[rows, :]
            qh = zh[rows, 0:dk]
            kh = zh[rows, dk:2 * dk]
            vh = zh[rows, 2 * dk:2 * dk + dv].astype(BF16)
            gate = zh[rows, 2 * dk + dv:hw]
            qrot = qh * cosv + pltpu.roll(qh, dk // 2, 1) * sinv
            kr = kh * cosv + pltpu.roll(kh, dk // 2, 1) * sinv
            scores = _dot_nt((qrot * (dk ** -0.5)).astype(BF16), kr.astype(BF16)) * dm_ref[h]
            inner = _dot(scores.astype(BF16), vh)
            st = st_ref[h]
            cross = _dot((qrot * qd_ref[h]).astype(BF16), st.astype(BF16))
            ke = kr * kd_ref[h]
            st_ref[h] = math.exp(blk * log_gammas[h]) * st + _dot_tn(ke.astype(BF16), vh)
            o = inner + cross
            mean = jnp.mean(o, axis=-1, keepdims=True)
            d = o - mean
            var = jnp.mean(d * d, axis=-1, keepdims=True)
            on = d * lax.rsqrt(var + GN_EPS) * gn_ref[:, h * dv:(h + 1) * dv]
            yb_ref[rows, h * dv:(h + 1) * dv] = (jax.nn.silu(gate) * on).astype(yb_ref.dtype)

    @pl.when(i % steps_per_seq == steps_per_seq - 1)
    def _fin():
        ro_ref[0] = st_ref[...]


def _ret_proj(x, g, w_heads, cos_t, sin_t, r0, gn_g, seq_len):
    m, d = x.shape
    heads, dk, dv = r0.shape[1], r0.shape[2], r0.shape[3]
    vw = heads * dv
    tm = _tile(seq_len, 512, 64)
    blk = _tile(tm, 256, 64)
    sps = seq_len // tm
    kern = functools.partial(_ret_proj_kernel, heads=heads, dk=dk, dv=dv, blk=blk, steps_per_seq=sps)
    return pl.pallas_call(
        kern,
        grid=(m // tm,),
        in_specs=[
            pl.BlockSpec((tm, d), lambda i: (i, 0)),
            pl.BlockSpec((1, d), lambda i: (0, 0)),
            pl.BlockSpec(w_heads.shape, lambda i: (0, 0), pipeline_mode=pl.Buffered(1)),
            pl.BlockSpec((tm, dk), lambda i: (i % sps, 0)),
            pl.BlockSpec((tm, dk), lambda i: (i % sps, 0)),
            pl.BlockSpec((1, heads, dk, dv), lambda i: (i // sps, 0, 0, 0)),
            pl.BlockSpec((1, vw), lambda i: (0, 0)),
        ],
        out_specs=[
            pl.BlockSpec((tm, vw), lambda i: (i, 0)),
            pl.BlockSpec((1, heads, dk, dv), lambda i: (i // sps, 0, 0, 0)),
        ],
        out_shape=[
            jax.ShapeDtypeStruct((m, vw), BF16),
            jax.ShapeDtypeStruct((m // seq_len, heads, dk, dv), F32),
        ],
        scratch_shapes=[pltpu.VMEM((heads, dk, dv), F32), pltpu.VMEM((heads, blk, blk), F32),
                        pltpu.VMEM((heads, blk, dk), F32), pltpu.VMEM((heads, blk, dk), F32)],
        compiler_params=_params("arbitrary", vmem_limit=VMEM_LIMIT_RESIDENT_BYTES),
        name="retention_proj",
    )(x, g, w_heads, cos_t, sin_t, r0, gn_g.reshape(1, -1))


def _merge_kernel(ya_ref, yb_ref, zg_ref, x_ref, wba_ref, wbb_ref, wo_ref, g_ref, o_ref):
    d = x_ref.shape[1]
    ga = jax.nn.sigmoid(zg_ref[:, 0:d].astype(F32))
    gb = jax.nn.sigmoid(zg_ref[:, d:2 * d].astype(F32))
    merged = ga * _dot(ya_ref[...], wba_ref[...]) + gb * _dot(yb_ref[...], wbb_ref[...])
    mo = _dot(merged.astype(BF16), wo_ref[...])
    o_ref[...] = x_ref[...] + _rms(mo, g_ref[...])


def _merge(ya, yb, zg, x, wba, wbb, wo, g):
    m, d = x.shape
    tm = _tile(m, 256, 8)
    tok = lambda w: pl.BlockSpec((tm, w), lambda i: (i, 0))
    res = lambda a: pl.BlockSpec(a.shape, lambda i: (0, 0), pipeline_mode=pl.Buffered(1))
    return pl.pallas_call(
        _merge_kernel,
        grid=(m // tm,),
        in_specs=[tok(ya.shape[1]), tok(yb.shape[1]), tok(zg.shape[1]), tok(d),
                  res(wba), res(wbb), res(wo), pl.BlockSpec((1, d), lambda i: (0, 0))],
        out_specs=tok(d),
        out_shape=jax.ShapeDtypeStruct((m, d), F32),
        compiler_params=_params("arbitrary"),
        name="merge_out",
    )(ya, yb, zg, x, wba, wbb, wo, g)


FFN_CHUNK = V7X_MXU_DIM
FFN_PASS_CHUNKS = 6


def _ffn_pass_kernel(x_ref, gpre_ref, wg_ref, wu_ref, wo_ref, *rest, first, last):
    rest = list(rest)
    acc = None if first else rest.pop(0)[...]
    gpost_ref = rest.pop(0) if last else None
    o_ref = rest.pop(0)
    x = x_ref[...]
    h = _rms(x, gpre_ref[...]).astype(BF16)
    for c in range(wg_ref.shape[1] // FFN_CHUNK):
        cols = slice(c * FFN_CHUNK, (c + 1) * FFN_CHUNK)
        act = jax.nn.silu(_dot(h, wg_ref[:, cols])) * _dot(h, wu_ref[:, cols])
        part = _dot(act.astype(BF16), wo_ref[cols, :])
        acc = part if acc is None else acc + part
    o_ref[...] = x + _rms(acc, gpost_ref[...]) if last else acc


def _ffn(x, gpre, ffn_slices, gpost):
    m, d = x.shape
    tm = _tile(m, 512, 8)
    tok = pl.BlockSpec((tm, d), lambda i: (i, 0))
    vec = pl.BlockSpec((1, d), lambda i: (0, 0))
    res = lambda a: pl.BlockSpec(a.shape, lambda i: (0, 0), pipeline_mode=pl.Buffered(1))
    acc = None
    for p, (wg, wu, wo) in enumerate(ffn_slices):
        first, last = p == 0, p == len(ffn_slices) - 1
        args = [x, gpre, wg, wu, wo] + ([] if first else [acc]) + ([gpost] if last else [])
        specs = [tok, vec, res(wg), res(wu), res(wo)] + ([] if first else [tok]) + ([vec] if last else [])
        acc = pl.pallas_call(
            functools.partial(_ffn_pass_kernel, first=first, last=last),
            grid=(m // tm,),
            in_specs=specs,
            out_specs=tok,
            out_shape=jax.ShapeDtypeStruct((m, d), F32),
            compiler_params=_params("arbitrary"),
            name="swiglu_ffn_pass",
        )(*args)
    return acc


def _ffn_slices(w_fi, w_fo):
    f = w_fo.shape[0]
    n_chunks = f // FFN_CHUNK
    assert n_chunks * FFN_CHUNK == f
    n_pass = -(-n_chunks // FFN_PASS_CHUNKS)
    sizes = [n_chunks // n_pass + (1 if p < n_chunks % n_pass else 0) for p in range(n_pass)]
    out, off = [], 0
    for sz in sizes:
        w = sz * FFN_CHUNK
        out.append((w_fi[:, off:off + w].astype(BF16), w_fi[:, f + off:f + off + w].astype(BF16),
                    w_fo[off:off + w, :].astype(BF16)))
        off += w
    return out


def _rope_tables(pos, dk):
    half = dk // 2
    inv = ROPE_BASE ** (-jnp.arange(half, dtype=F32) / half)
    ang = pos[:, None] * inv[None, :]
    cos = jnp.cos(ang)
    sin = jnp.sin(ang)
    return jnp.concatenate([cos, cos], axis=-1), jnp.concatenate([-sin, sin], axis=-1)


def _head_major(w_b, heads, dk, dv):
    d = w_b.shape[0]
    qkw, vw = heads * dk, heads * dv
    parts = [w_b[:, 0:qkw].reshape(d, heads, dk), w_b[:, qkw:2 * qkw].reshape(d, heads, dk),
             w_b[:, 2 * qkw:2 * qkw + vw].reshape(d, heads, dv),
             w_b[:, 2 * qkw + vw:].reshape(d, heads, dv)]
    return jnp.concatenate(parts, axis=2).reshape(d, heads * (2 * dk + 2 * dv))


def _layer(x, rope, shift0, s0, r0, w):
    b, t, d = x.shape
    x2 = x.reshape(b * t, d)
    row = lambda a: a.reshape(1, -1)
    zg = _norm_matmul(x2, row(w["nmp"]), w["w_g"], BF16)
    aw = w["w_dup"].shape[1]
    ya, shift, s_new = _wkv_branch(x, row(w["nmp"]), w["w_a"], _group_major(shift0, aw), s0, w["mu"],
                                   w["w0"], w["w_dup"], w["a0"], w["w_aup"], w["w_gup"], w["k_k"],
                                   w["k_a"], w["r_k"], w["lnx_g"], w["lnx_b"])
    shift = _group_major_inverse(shift.reshape(b, -1), aw)
    yb, r_new = _ret_proj(x2, row(w["nmp"]), w["w_b"], rope[0], rope[1], r0, w["ret_g"], t)
    x1 = _merge(ya.reshape(b * t, -1), yb, zg, x2, w["w_ba"], w["w_bb"],
                w["w_out"], row(w["nmq"]))
    xo = _ffn(x1, row(w["nfp"]), w["ffn"], row(w["nfq"]))
    return xo.reshape(b, t, d), shift, s_new, r_new


def kernel(x_prompt, x_sample, state_wkv, state_ret, state_shift, norm_mix_pre, norm_mix_post, norm_ffn_pre, norm_ffn_post, w_in, shift_mu, decay_base, w_decay_up, iclr_base, w_iclr_up, w_gate_up, key_k, key_a, bonus_rk, lnx_gain, lnx_bias, w_branch_a, ret_norm_gain, w_branch_b, w_out, w_ffn_in, w_ffn_out):
    depth = w_in.shape[0]
    bp, tp, _ = x_prompt.shape
    bs, ts, _ = x_sample.shape
    asw = shift_mu.shape[1]
    heads_a, hd_a = bonus_rk.shape[1], bonus_rk.shape[2]
    heads_b, dk, dv = state_ret.shape[2], state_ret.shape[3], state_ret.shape[4]
    b_in = 2 * heads_b * dk + 2 * heads_b * dv
    dt = x_prompt.dtype

    rope_p = _rope_tables(jnp.arange(tp, dtype=F32), dk)
    rope_s = _rope_tables(jnp.arange(ts, dtype=F32) + jnp.float32(PAST_LEN), dk)
    zero_shift = jnp.zeros((bp, asw), dt)
    zero_wkv = jnp.zeros((bp, heads_a, hd_a, hd_a), dt)
    zero_ret = jnp.zeros((bp, heads_b, dk, dv), dt)

    xp, xs = x_prompt, x_sample
    outs = [[] for _ in range(6)]
    for l in range(depth):
        w = dict(
            nmp=norm_mix_pre[l], nmq=norm_mix_post[l], nfp=norm_ffn_pre[l], nfq=norm_ffn_post[l],
            w_a=_group_major(w_in[l, :, :asw], heads_a * hd_a).astype(BF16),
            w_b=_head_major(w_in[l, :, asw:asw + b_in], heads_b, dk, dv).astype(BF16),
            w_g=w_in[l, :, asw + b_in:].astype(BF16),
            mu=_group_major(shift_mu[l], heads_a * hd_a), w0=decay_base[l], w_dup=w_decay_up[l].astype(BF16), a0=iclr_base[l],
            w_aup=w_iclr_up[l].astype(BF16), w_gup=w_gate_up[l].astype(BF16), k_k=key_k[l],
            k_a=key_a[l], r_k=bonus_rk[l], lnx_g=lnx_gain[l], lnx_b=lnx_bias[l],
            w_ba=w_branch_a[l].astype(BF16), ret_g=ret_norm_gain[l],
            w_bb=w_branch_b[l].astype(BF16), w_out=w_out[l].astype(BF16),
            ffn=_ffn_slices(w_ffn_in[l], w_ffn_out[l]),
        )
        xp, sp, wkp, rtp = _layer(xp, rope_p, zero_shift, zero_wkv, zero_ret, w)
        xs, ss, wks, rts = _layer(xs, rope_s, state_shift[l], state_wkv[l], state_ret[l], w)
        for lst, val in zip(outs, (wkp, rtp, sp, wks, rts, ss)):
            lst.append(val)
    return (xp, xs) + tuple(jnp.stack(o) for o in outs)
```

```python
import functools
import math

import jax
import jax.numpy as jnp
from jax import lax
from jax.experimental import pallas as pl
from jax.experimental.pallas import tpu as pltpu

F32 = jnp.float32
BF16 = jnp.bfloat16

PAST_LEN = 4096
RMS_EPS = 1e-6
LNX_EPS = 64e-5
GN_EPS = 1e-5
ROPE_BASE = 10000.0

V7X_LANES = 128
V7X_MXU_DIM = 256
WKV_GROUP_LANES = V7X_LANES
WKV_BLOCK_LANES = V7X_MXU_DIM
WKV_CHUNK = 64
WKV_PREP_BATCH = 2
V7X_VMEM_BYTES = 64 * 1024 * 1024
VMEM_LIMIT_BYTES = V7X_VMEM_BYTES * 7 // 8
VMEM_LIMIT_RESIDENT_BYTES = V7X_VMEM_BYTES * 15 // 16


def _dot(a, b):
    return jnp.dot(a, b, preferred_element_type=F32)


def _dot_nt(a, b):
    return lax.dot_general(a, b, (((1,), (1,)), ((), ())), preferred_element_type=F32)


def _dot_tn(a, b):
    return lax.dot_general(a, b, (((0,), (0,)), ((), ())), preferred_element_type=F32)


def _rms(x, g):
    return x * lax.rsqrt(jnp.mean(x * x, axis=-1, keepdims=True) + RMS_EPS) * g


def _tile(n, target, mult=V7X_LANES):
    if n <= target:
        return n
    best = None
    for t in range(mult, target + 1, mult):
        if n % t == 0:
            best = t
    assert best is not None, (n, target, mult)
    return best


def _params(*sem, vmem_limit=VMEM_LIMIT_BYTES):
    return pltpu.CompilerParams(dimension_semantics=sem, vmem_limit_bytes=vmem_limit)


def _norm_matmul_kernel(x_ref, g_ref, w_ref, o_ref, *, tn):
    h = _rms(x_ref[...], g_ref[...]).astype(BF16)
    for j in range(w_ref.shape[1] // tn):
        cols = slice(j * tn, (j + 1) * tn)
        o_ref[:, cols] = _dot(h, w_ref[:, cols]).astype(o_ref.dtype)


def _norm_matmul(x, g, w, out_dtype, tm_target=512, tn_target=512):
    m, d = x.shape
    n = w.shape[1]
    tm = _tile(m, tm_target, 8)
    tn = _tile(n, tn_target)
    return pl.pallas_call(
        functools.partial(_norm_matmul_kernel, tn=tn),
        grid=(m // tm,),
        in_specs=[
            pl.BlockSpec((tm, d), lambda i: (i, 0)),
            pl.BlockSpec((1, d), lambda i: (0, 0)),
            pl.BlockSpec((d, n), lambda i: (0, 0), pipeline_mode=pl.Buffered(1)),
        ],
        out_specs=pl.BlockSpec((tm, n), lambda i: (i, 0)),
        out_shape=jax.ShapeDtypeStruct((m, n), out_dtype),
        compiler_params=_params("arbitrary"),
        name="norm_matmul",
    )(x, g, w)


def _split_bf16(x):
    hi = x.astype(BF16)
    lo = (x - hi.astype(F32)).astype(BF16)
    return hi, lo


def _wkv_kernel(x_ref, gn_ref, wa_ref, sh0_ref, s0_ref, mu_ref, w0_ref, wdup_ref, a0_ref, waup_ref,
                wgup_ref, kk_ref, ka_ref, rk_ref, lg_ref, lb_ref,
                ya_ref, sho_ref, so_ref,
                carry_ref, s_ref, ar_ref, bk_ref, bkh_ref, v_ref, y_ref, pc_ref, t_ref, av_ref, wcat_ref,
                v32_ref, rk2_ref, gate_ref,
                *, heads, head_dim, ranks):
    ns, tt = x_ref.shape[0], x_ref.shape[1]
    rows = ns * tt
    aw = heads * head_dim
    gw = WKV_GROUP_LANES
    hpg = gw // head_dim
    groups = aw // gw
    c_len = min(WKV_CHUNK, tt)
    n_chunks = tt // c_len
    dr, ir, gr = ranks
    t = pl.program_id(1)

    @pl.when(t == 0)
    def _init():
        s_ref[...] = jnp.zeros_like(s_ref)
        for s in range(ns):
            carry_ref[s] = sh0_ref[s]
            for h in range(heads):
                g, hh = divmod(h, hpg)
                lo = hh * head_dim
                s_ref[s * groups + g, lo:lo + head_dim, lo:lo + head_dim] = s0_ref[s, h]

    hn = _rms(x_ref[...].reshape(rows, x_ref.shape[2]), gn_ref[...]).astype(BF16)
    lw = dr + ir + gr

    def project_shifted(cols):
        z = _dot(hn, wa_ref[:, cols])
        rolled = pltpu.roll(z, 1, 0)
        row8 = lax.broadcasted_iota(jnp.int32, (8, z.shape[1]), 0)
        pieces = []
        for s in range(ns):
            r0 = s * tt
            pieces += [jnp.where(row8 == 0, carry_ref[s, :, cols], rolled[r0:r0 + 8]),
                       rolled[r0 + 8:r0 + tt]]
            last = z[r0 + tt - 1:r0 + tt, :]
            carry_ref[s, :, cols] = last
            sho_ref[s, :, cols] = last
        prev = jnp.concatenate(pieces, axis=0)
        return z + mu_ref[:, cols] * (prev - z)

    zl = project_shifted(slice(0, lw))
    wd = zl[:, 0:dr]
    ad = zl[:, dr:dr + ir]
    gd = zl[:, dr + ir:lw]
    xw = w0_ref[...] + _dot(jnp.tanh(wd).astype(BF16), wdup_ref[...])
    ld = (-math.exp(-0.5)) * jax.nn.sigmoid(xw)
    a_ic = jax.nn.sigmoid(a0_ref[...] + _dot(ad.astype(BF16), waup_ref[...]))
    gate_ref[...] = _dot(jax.nn.sigmoid(gd).astype(BF16), wgup_ref[...])

    gi = lax.broadcasted_iota(jnp.int32, (gw, gw), 0) // head_dim
    gj = lax.broadcasted_iota(jnp.int32, (gw, gw), 1) // head_dim
    same_head = gi == gj

    sw = min(aw, WKV_BLOCK_LANES)
    si = lax.broadcasted_iota(jnp.int32, (sw, sw), 0) // head_dim
    sj = lax.broadcasted_iota(jnp.int32, (sw, sw), 1) // head_dim
    ones_sw = (si == sj).astype(BF16)

    def head_sum(x):
        xb = x.astype(BF16)
        parts = [_dot(xb[:, q * sw:(q + 1) * sw], ones_sw) for q in range(aw // sw)]
        return jnp.concatenate(parts, axis=1) if len(parts) > 1 else parts[0]

    tb = min(tt, V7X_MXU_DIM)
    n_all = rows // c_len
    ti = lax.broadcasted_iota(jnp.int32, (tb, tb), 0)
    tj = lax.broadcasted_iota(jnp.int32, (tb, tb), 1)
    tri = (((ti // c_len) == (tj // c_len)) & (tj <= ti)).astype(BF16)
    ld_hi, ld_lo = _split_bf16(ld)
    cum_all = jnp.concatenate(
        [_dot(tri, ld_hi[i * tb:(i + 1) * tb]) + _dot(tri, ld_lo[i * tb:(i + 1) * tb])
         for i in range(rows // tb)], axis=0)

    for q in range(aw // sw):
        gl = slice(q * sw, (q + 1) * sw)
        z3 = project_shifted(slice(lw + 3 * q * sw, lw + 3 * (q + 1) * sw))
        r = z3[:, 0:sw]
        k = z3[:, sw:2 * sw]
        v = z3[:, 2 * sw:3 * sw]
        a_g = a_ic[:, gl]
        ld_g = ld[:, gl]
        kk = k * kk_ref[:, gl]
        k2 = k * (1.0 + (a_g - 1.0) * ka_ref[:, gl])
        kkn = kk * jnp.minimum(lax.rsqrt(_dot((kk * kk).astype(BF16), ones_sw)), 1e12)
        cum = cum_all[:, gl]
        p_inc = jnp.exp(cum)
        p_inv = jnp.exp(-cum)
        a_t = -(kkn * jnp.exp(cum - ld_g))
        r_t = r * p_inc
        b_t = (kkn * a_g) * p_inv
        k_t = k2 * p_inv
        for c in range(n_all):
            rs = slice(c * c_len, (c + 1) * c_len)
            p_end = p_inc[(c + 1) * c_len - 1:(c + 1) * c_len, :]
            ar_ref[c, 0:c_len, gl] = a_t[rs].astype(BF16)
            ar_ref[c, c_len:2 * c_len, gl] = r_t[rs].astype(BF16)
            bk_ref[c, 0:c_len, gl] = b_t[rs].astype(BF16)
            bk_ref[c, c_len:2 * c_len, gl] = k_t[rs].astype(BF16)
            bkh_ref[c, 0:c_len, gl] = (b_t[rs] * p_end).astype(BF16)
            bkh_ref[c, c_len:2 * c_len, gl] = (k_t[rs] * p_end).astype(BF16)
            pc_ref[c, :, gl] = p_end
        v_ref[:, gl] = v.astype(BF16)
        v32_ref[:, gl] = v
        rk2_ref[:, gl] = r * k2 * rk_ref[:, gl]

    lane_head = lax.broadcasted_iota(jnp.int32, (1, gw), 1) // head_dim
    ci = lax.broadcasted_iota(jnp.int32, (c_len, gw), 0)
    cj = lax.broadcasted_iota(jnp.int32, (c_len, gw), 1) % head_dim
    assert c_len == head_dim
    strict = cj < ci
    incl = cj <= ci
    eye = (cj == ci).astype(F32)
    n_doubling = int(math.log2(c_len))

    def bd(m):
        zero = jnp.zeros_like(m)
        return jnp.concatenate([jnp.where(lane_head == h, m, zero) for h in range(hpg)], axis=0)

    gs = range(groups)
    sls = [slice(g * gw, (g + 1) * gw) for g in gs]

    def prep(chunks):
        items = [(c, g) for c in chunks for g in gs]
        n = range(len(items))
        ar = [ar_ref[c, :, sls[g]] for c, g in items]
        bk = [bk_ref[c, :, sls[g]] for c, g in items]
        vv = [v_ref[c * c_len:(c + 1) * c_len, sls[g]] for c, g in items]
        s4 = [_dot_nt(ar[i], jnp.concatenate([bd(bk[i][0:c_len]), bd(bk[i][c_len:2 * c_len])],
                                              axis=0)) for i in n]
        a_ab = [jnp.where(strict, s4[i][0:c_len, 0:gw], 0.0) for i in n]
        a_ak = [jnp.where(strict, s4[i][0:c_len, gw:2 * gw], 0.0).astype(BF16) for i in n]
        for i, (c, g) in enumerate(items):
            wcat_ref[c, g, :, 0:gw] = jnp.where(incl, s4[i][c_len:2 * c_len, 0:gw], 0.0).astype(BF16)
            wcat_ref[c, g, :, gw:2 * gw] = jnp.where(incl, s4[i][c_len:2 * c_len, gw:2 * gw],
                                                     0.0).astype(BF16)
        av = [_dot(a_ak[i], bd(vv[i])) for i in n]
        for i, (c, g) in enumerate(items):
            av_ref[c, g] = av[i]
        tk = [eye + a_ab[i] for i in n]
        mb = [a_ab[i].astype(BF16) for i in n]
        mk = [_dot(mb[i], bd(mb[i])) for i in n]
        for step in range(1, n_doubling - 1):
            mb = [mk[i].astype(BF16) for i in n]
            rr = [_dot(mb[i], jnp.concatenate([bd(mb[i]), bd(tk[i].astype(BF16))], axis=1))
                  for i in n]
            mk = [rr[i][:, 0:gw] for i in n]
            tk = [tk[i] + rr[i][:, gw:2 * gw] for i in n]
        tk = [tk[i] + _dot(mk[i].astype(BF16), bd(tk[i].astype(BF16))) for i in n]
        for i, (c, g) in enumerate(items):
            t_ref[c, g] = tk[i].astype(BF16)

    def recur(lc):
        items = [(s * n_chunks + lc, g, s * groups + g) for s in range(ns) for g in gs]
        n = range(len(items))
        ar = [ar_ref[c, :, sls[g]] for c, g, _ in items]
        vv = [v_ref[c * c_len:(c + 1) * c_len, sls[g]] for c, g, _ in items]
        s0 = [s_ref[si] for _, _, si in items]
        m1 = [_dot_nt(ar[i], s0[i].astype(BF16)) for i in n]
        xb = [(m1[i][0:c_len] + av_ref[c, g]).astype(BF16) for i, (c, g, _) in enumerate(items)]
        ub = [_dot(t_ref[c, g], bd(xb[i])).astype(BF16) for i, (c, g, _) in enumerate(items)]
        ds = [_dot_tn(jnp.concatenate([ub[i], vv[i]], axis=0), bkh_ref[c, :, sls[g]])
              for i, (c, g, _) in enumerate(items)]
        for i, (c, g, si) in enumerate(items):
            s_ref[si] = s0[i] * pc_ref[c][:, sls[g]] + jnp.where(same_head, ds[i], 0.0)
        yy = [_dot(wcat_ref[c, g], jnp.concatenate([bd(ub[i]), bd(vv[i])], axis=0))
              for i, (c, g, _) in enumerate(items)]
        for i, (c, g, _) in enumerate(items):
            y_ref[c * c_len:(c + 1) * c_len, sls[g]] = m1[i][c_len:2 * c_len] + yy[i]

    for c0 in range(0, n_all, WKV_PREP_BATCH):
        prep(range(c0, min(c0 + WKV_PREP_BATCH, n_all)))
    for lc in range(n_chunks):
        recur(lc)

    y = y_ref[...]
    inv_n = 1.0 / head_dim
    mean = head_sum(y) * inv_n
    d = y - mean
    var = head_sum(d * d) * inv_n
    yn = d * lax.rsqrt(var + LNX_EPS) * lg_ref[...] + lb_ref[...]
    bonus = head_sum(rk2_ref[...]) * v32_ref[...]
    ya_ref[...] = ((yn + bonus) * gate_ref[...]).astype(ya_ref.dtype).reshape(ya_ref.shape)

    @pl.when(t == pl.num_programs(1) - 1)
    def _fin():
        for s in range(ns):
            for h in range(heads):
                g, hh = divmod(h, hpg)
                lo = hh * head_dim
                so_ref[s, h] = s_ref[s * groups + g, lo:lo + head_dim, lo:lo + head_dim]


def _group_major(a, aw):
    lead = a.shape[:-1]
    bw = min(aw, WKV_BLOCK_LANES)
    g = aw // bw
    rkv = a[..., :3 * aw].reshape(*lead, 3, g, bw)
    rkv = jnp.swapaxes(rkv, -3, -2).reshape(*lead, 3 * aw)
    return jnp.concatenate([a[..., 3 * aw:], rkv], axis=-1)


def _group_major_inverse(a, aw):
    lead = a.shape[:-1]
    bw = min(aw, WKV_BLOCK_LANES)
    g = aw // bw
    lw = a.shape[-1] - 3 * aw
    rkv = a[..., lw:].reshape(*lead, g, 3, bw)
    rkv = jnp.swapaxes(rkv, -3, -2).reshape(*lead, 3 * aw)
    return jnp.concatenate([rkv, a[..., :lw]], axis=-1)


def _wkv_branch(x, gn, wa, shift0, s0, mu, w0, wdup, a0, waup, wgup, k_k, k_a, r_k, lnx_g, lnx_b):
    b, t, d = x.shape
    asw = wa.shape[1]
    heads, head_dim = s0.shape[1], s0.shape[2]
    aw = heads * head_dim
    ranks = (wdup.shape[0], waup.shape[0], wgup.shape[0])
    ns = 2 if b % 2 == 0 else 1
    tt = _tile(t, 512 // ns, WKV_CHUNK)
    c_len = min(WKV_CHUNK, tt)
    n_all = ns * tt // c_len
    groups = aw // WKV_GROUP_LANES
    row = lambda a: a.reshape(1, -1)
    const = lambda shape: pl.BlockSpec(shape, lambda i, j: (0,) * len(shape))
    kern = functools.partial(_wkv_kernel, heads=heads, head_dim=head_dim, ranks=ranks)
    return pl.pallas_call(
        kern,
        grid=(b // ns, t // tt),
        in_specs=[
            pl.BlockSpec((ns, tt, d), lambda i, j: (i, j, 0)),
            const((1, d)),
            pl.BlockSpec(wa.shape, lambda i, j: (0, 0), pipeline_mode=pl.Buffered(1)),
            pl.BlockSpec((ns, 1, asw), lambda i, j: (i, 0, 0)),
            pl.BlockSpec((ns, heads, head_dim, head_dim), lambda i, j: (i, 0, 0, 0)),
            const((1, asw)), const((1, aw)), const(wdup.shape), const((1, aw)), const(waup.shape),
            const(wgup.shape), const((1, aw)), const((1, aw)), const((1, aw)), const((1, aw)),
            const((1, aw)),
        ],
        out_specs=[
            pl.BlockSpec((ns, tt, aw), lambda i, j: (i, j, 0)),
            pl.BlockSpec((ns, 1, asw), lambda i, j: (i, 0, 0)),
            pl.BlockSpec((ns, heads, head_dim, head_dim), lambda i, j: (i, 0, 0, 0)),
        ],
        out_shape=[
            jax.ShapeDtypeStruct((b, t, aw), BF16),
            jax.ShapeDtypeStruct((b, 1, asw), F32),
            jax.ShapeDtypeStruct((b, heads, head_dim, head_dim), F32),
        ],
        scratch_shapes=[
            pltpu.VMEM((ns, 1, asw), F32),
            pltpu.VMEM((ns * groups, WKV_GROUP_LANES, WKV_GROUP_LANES), F32),
            pltpu.VMEM((n_all, 2 * c_len, aw), BF16),
            pltpu.VMEM((n_all, 2 * c_len, aw), BF16),
            pltpu.VMEM((n_all, 2 * c_len, aw), BF16),
            pltpu.VMEM((ns * tt, aw), BF16),
            pltpu.VMEM((ns * tt, aw), F32),
            pltpu.VMEM((n_all, 1, aw), F32),
            pltpu.VMEM((n_all, groups, c_len, WKV_GROUP_LANES), BF16),
            pltpu.VMEM((n_all, groups, c_len, WKV_GROUP_LANES), F32),
            pltpu.VMEM((n_all, groups, c_len, 2 * WKV_GROUP_LANES), BF16),
            pltpu.VMEM((ns * tt, aw), F32),
            pltpu.VMEM((ns * tt, aw), F32),
            pltpu.VMEM((ns * tt, aw), F32),
        ],
        compiler_params=_params("arbitrary", "arbitrary"),
        name="wkv7_chunked",
    )(x, gn, wa, shift0.reshape(b, 1, asw), s0, row(mu), row(w0), wdup, row(a0), waup, wgup,
      row(k_k), row(k_a), row(r_k), row(lnx_g), row(lnx_b))


def _ret_proj_kernel(x_ref, g_ref, w_ref, cos_ref, sin_ref, r0_ref, gn_ref, yb_ref, ro_ref,
                     st_ref, dm_ref, qd_ref, kd_ref, *, heads, dk, dv, blk, steps_per_seq):
    tm = x_ref.shape[0]
    hw = 2 * dk + 2 * dv
    i = pl.program_id(0)
    log_gammas = [math.log1p(-(2.0 ** (-5.0 - h))) for h in range(heads)]

    @pl.when(i == 0)
    def _tables():
        diff = (lax.broadcasted_iota(jnp.int32, (blk, blk), 0)
                - lax.broadcasted_iota(jnp.int32, (blk, blk), 1)).astype(F32)
        pos = lax.broadcasted_iota(jnp.int32, (blk, dk), 0).astype(F32)
        for h in range(heads):
            lg = log_gammas[h]
            dm_ref[h] = jnp.where(diff >= 0, jnp.exp(jnp.maximum(diff, 0.0) * lg), 0.0)
            qd_ref[h] = jnp.exp((pos + 1.0) * lg) * (dk ** -0.5)
            kd_ref[h] = jnp.exp((blk - 1.0 - pos) * lg)

    @pl.when(i % steps_per_seq == 0)
    def _init():
        st_ref[...] = r0_ref[0]

    hn = _rms(x_ref[...], g_ref[...]).astype(BF16)
    for h in range(heads):
        zh = _dot(hn, w_ref[:, h * hw:(h + 1) * hw])
        for s in range(tm // blk):
            rows = slice(s * blk, (s + 1) * blk)
            cosv = cos_ref[rows, :]
            sinv = sin_ref[rows, :]
            qh = zh[rows, 0:dk]
            kh = zh[rows, dk:2 * dk]
            vh = zh[rows, 2 * dk:2 * dk + dv].astype(BF16)
            gate = zh[rows, 2 * dk + dv:hw]
            qrot = qh * cosv + pltpu.roll(qh, dk // 2, 1) * sinv
            kr = kh * cosv + pltpu.roll(kh, dk // 2, 1) * sinv
            scores = _dot_nt((qrot * (dk ** -0.5)).astype(BF16), kr.astype(BF16)) * dm_ref[h]
            inner = _dot(scores.astype(BF16), vh)
            st = st_ref[h]
            cross = _dot((qrot * qd_ref[h]).astype(BF16), st.astype(BF16))
            ke = kr * kd_ref[h]
            st_ref[h] = math.exp(blk * log_gammas[h]) * st + _dot_tn(ke.astype(BF16), vh)
            o = inner + cross
            mean = jnp.mean(o, axis=-1, keepdims=True)
            d = o - mean
            var = jnp.mean(d * d, axis=-1, keepdims=True)
            on = d * lax.rsqrt(var + GN_EPS) * gn_ref[:, h * dv:(h + 1) * dv]
            yb_ref[rows, h * dv:(h + 1) * dv] = (jax.nn.silu(gate) * on).astype(yb_ref.dtype)

    @pl.when(i % steps_per_seq == steps_per_seq - 1)
    def _fin():
        ro_ref[0] = st_ref[...]


def _ret_proj(x, g, w_heads, cos_t, sin_t, r0, gn_g, seq_len):
    m, d = x.shape
    heads, dk, dv = r0.shape[1], r0.shape[2], r0.shape[3]
    vw = heads * dv
    tm = _tile(seq_len, 512, 64)
    blk = _tile(tm, 256, 64)
    sps = seq_len // tm
    kern = functools.partial(_ret_proj_kernel, heads=heads, dk=dk, dv=dv, blk=blk, steps_per_seq=sps)
    return pl.pallas_call(
        kern,
        grid=(m // tm,),
        in_specs=[
            pl.BlockSpec((tm, d), lambda i: (i, 0)),
            pl.BlockSpec((1, d), lambda i: (0, 0)),
            pl.BlockSpec(w_heads.shape, lambda i: (0, 0), pipeline_mode=pl.Buffered(1)),
            pl.BlockSpec((tm, dk), lambda i: (i % sps, 0)),
            pl.BlockSpec((tm, dk), lambda i: (i % sps, 0)),
            pl.BlockSpec((1, heads, dk, dv), lambda i: (i // sps, 0, 0, 0)),
            pl.BlockSpec((1, vw), lambda i: (0, 0)),
        ],
        out_specs=[
            pl.BlockSpec((tm, vw), lambda i: (i, 0)),
            pl.BlockSpec((1, heads, dk, dv), lambda i: (i // sps, 0, 0, 0)),
        ],
        out_shape=[
            jax.ShapeDtypeStruct((m, vw), BF16),
            jax.ShapeDtypeStruct((m // seq_len, heads, dk, dv), F32),
        ],
        scratch_shapes=[pltpu.VMEM((heads, dk, dv), F32), pltpu.VMEM((heads, blk, blk), F32),
                        pltpu.VMEM((heads, blk, dk), F32), pltpu.VMEM((heads, blk, dk), F32)],
        compiler_params=_params("arbitrary", vmem_limit=VMEM_LIMIT_RESIDENT_BYTES),
        name="retention_proj",
    )(x, g, w_heads, cos_t, sin_t, r0, gn_g.reshape(1, -1))


def _merge_kernel(ya_ref, yb_ref, zg_ref, x_ref, wba_ref, wbb_ref, wo_ref, g_ref, o_ref):
    d = x_ref.shape[1]
    ga = jax.nn.sigmoid(zg_ref[:, 0:d].astype(F32))
    gb = jax.nn.sigmoid(zg_ref[:, d:2 * d].astype(F32))
    merged = ga * _dot(ya_ref[...], wba_ref[...]) + gb * _dot(yb_ref[...], wbb_ref[...])
    mo = _dot(merged.astype(BF16), wo_ref[...])
    o_ref[...] = x_ref[...] + _rms(mo, g_ref[...])


def _merge(ya, yb, zg, x, wba, wbb, wo, g):
    m, d = x.shape
    tm = _tile(m, 256, 8)
    tok = lambda w: pl.BlockSpec((tm, w), lambda i: (i, 0))
    res = lambda a: pl.BlockSpec(a.shape, lambda i: (0, 0), pipeline_mode=pl.Buffered(1))
    return pl.pallas_call(
        _merge_kernel,
        grid=(m // tm,),
        in_specs=[tok(ya.shape[1]), tok(yb.shape[1]), tok(zg.shape[1]), tok(d),
                  res(wba), res(wbb), res(wo), pl.BlockSpec((1, d), lambda i: (0, 0))],
        out_specs=tok(d),
        out_shape=jax.ShapeDtypeStruct((m, d), F32),
        compiler_params=_params("arbitrary"),
        name="merge_out",
    )(ya, yb, zg, x, wba, wbb, wo, g)


FFN_CHUNK = V7X_MXU_DIM
FFN_PASS_CHUNKS = 6


def _ffn_pass_kernel(x_ref, gpre_ref, wg_ref, wu_ref, wo_ref, *rest, first, last):
    rest = list(rest)
    acc = None if first else rest.pop(0)[...]
    gpost_ref = rest.pop(0) if last else None
    o_ref = rest.pop(0)
    x = x_ref[...]
    h = _rms(x, gpre_ref[...]).astype(BF16)
    for c in range(wg_ref.shape[1] // FFN_CHUNK):
        cols = slice(c * FFN_CHUNK, (c + 1) * FFN_CHUNK)
        act = jax.nn.silu(_dot(h, wg_ref[:, cols])) * _dot(h, wu_ref[:, cols])
        part = _dot(act.astype(BF16), wo_ref[cols, :])
        acc = part if acc is None else acc + part
    o_ref[...] = x + _rms(acc, gpost_ref[...]) if last else acc


def _ffn(x, gpre, ffn_slices, gpost):
    m, d = x.shape
    tm = _tile(m, 512, 8)
    tok = pl.BlockSpec((tm, d), lambda i: (i, 0))
    vec = pl.BlockSpec((1, d), lambda i: (0, 0))
    res = lambda a: pl.BlockSpec(a.shape, lambda i: (0, 0), pipeline_mode=pl.Buffered(1))
    acc = None
    for p, (wg, wu, wo) in enumerate(ffn_slices):
        first, last = p == 0, p == len(ffn_slices) - 1
        args = [x, gpre, wg, wu, wo] + ([] if first else [acc]) + ([gpost] if last else [])
        specs = [tok, vec, res(wg), res(wu), res(wo)] + ([] if first else [tok]) + ([vec] if last else [])
        acc = pl.pallas_call(
            functools.partial(_ffn_pass_kernel, first=first, last=last),
            grid=(m // tm,),
            in_specs=specs,
            out_specs=tok,
            out_shape=jax.ShapeDtypeStruct((m, d), F32),
            compiler_params=_params("arbitrary"),
            name="swiglu_ffn_pass",
        )(*args)
    return acc


def _ffn_slices(w_fi, w_fo):
    f = w_fo.shape[0]
    n_chunks = f // FFN_CHUNK
    assert n_chunks * FFN_CHUNK == f
    n_pass = -(-n_chunks // FFN_PASS_CHUNKS)
    sizes = [n_chunks // n_pass + (1 if p < n_chunks % n_pass else 0) for p in range(n_pass)]
    out, off = [], 0
    for sz in sizes:
        w = sz * FFN_CHUNK
        out.append((w_fi[:, off:off + w].astype(BF16), w_fi[:, f + off:f + off + w].astype(BF16),
                    w_fo[off:off + w, :].astype(BF16)))
        off += w
    return out


def _rope_tables(pos, dk):
    half = dk // 2
    inv = ROPE_BASE ** (-jnp.arange(half, dtype=F32) / half)
    ang = pos[:, None] * inv[None, :]
    cos = jnp.cos(ang)
    sin = jnp.sin(ang)
    return jnp.concatenate([cos, cos], axis=-1), jnp.concatenate([-sin, sin], axis=-1)


def _head_major(w_b, heads, dk, dv):
    d = w_b.shape[0]
    qkw, vw = heads * dk, heads * dv
    parts = [w_b[:, 0:qkw].reshape(d, heads, dk), w_b[:, qkw:2 * qkw].reshape(d, heads, dk),
             w_b[:, 2 * qkw:2 * qkw + vw].reshape(d, heads, dv),
             w_b[:, 2 * qkw + vw:].reshape(d, heads, dv)]
    return jnp.concatenate(parts, axis=2).reshape(d, heads * (2 * dk + 2 * dv))


def _layer(x, rope, shift0, s0, r0, w):
    b, t, d = x.shape
    x2 = x.reshape(b * t, d)
    row = lambda a: a.reshape(1, -1)
    zg = _norm_matmul(x2, row(w["nmp"]), w["w_g"], BF16)
    aw = w["w_dup"].shape[1]
    ya, shift, s_new = _wkv_branch(x, row(w["nmp"]), w["w_a"], _group_major(shift0, aw), s0, w["mu"],
                                   w["w0"], w["w_dup"], w["a0"], w["w_aup"], w["w_gup"], w["k_k"],
                                   w["k_a"], w["r_k"], w["lnx_g"], w["lnx_b"])
    shift = _group_major_inverse(shift.reshape(b, -1), aw)
    yb, r_new = _ret_proj(x2, row(w["nmp"]), w["w_b"], rope[0], rope[1], r0, w["ret_g"], t)
    x1 = _merge(ya.reshape(b * t, -1), yb, zg, x2, w["w_ba"], w["w_bb"],
                w["w_out"], row(w["nmq"]))
    xo = _ffn(x1, row(w["nfp"]), w["ffn"], row(w["nfq"]))
    return xo.reshape(b, t, d), shift, s_new, r_new


def kernel(x_prompt, x_sample, state_wkv, state_ret, state_shift, norm_mix_pre, norm_mix_post, norm_ffn_pre, norm_ffn_post, w_in, shift_mu, decay_base, w_decay_up, iclr_base, w_iclr_up, w_gate_up, key_k, key_a, bonus_rk, lnx_gain, lnx_bias, w_branch_a, ret_norm_gain, w_branch_b, w_out, w_ffn_in, w_ffn_out):
    depth = w_in.shape[0]
    bp, tp, _ = x_prompt.shape
    bs, ts, _ = x_sample.shape
    asw = shift_mu.shape[1]
    heads_a, hd_a = bonus_rk.shape[1], bonus_rk.shape[2]
    heads_b, dk, dv = state_ret.shape[2], state_ret.shape[3], state_ret.shape[4]
    b_in = 2 * heads_b * dk + 2 * heads_b * dv
    dt = x_prompt.dtype

    rope_p = _rope_tables(jnp.arange(tp, dtype=F32), dk)
    rope_s = _rope_tables(jnp.arange(ts, dtype=F32) + jnp.float32(PAST_LEN), dk)
    zero_shift = jnp.zeros((bp, asw), dt)
    zero_wkv = jnp.zeros((bp, heads_a, hd_a, hd_a), dt)
    zero_ret = jnp.zeros((bp, heads_b, dk, dv), dt)

    xp, xs = x_prompt, x_sample
    outs = [[] for _ in range(6)]
    for l in range(depth):
        w = dict(
            nmp=norm_mix_pre[l], nmq=norm_mix_post[l], nfp=norm_ffn_pre[l], nfq=norm_ffn_post[l],
            w_a=_group_major(w_in[l, :, :asw], heads_a * hd_a).astype(BF16),
            w_b=_head_major(w_in[l, :, asw:asw + b_in], heads_b, dk, dv).astype(BF16),
            w_g=w_in[l, :, asw + b_in:].astype(BF16),
            mu=_group_major(shift_mu[l], heads_a * hd_a), w0=decay_base[l], w_dup=w_decay_up[l].astype(BF16), a0=iclr_base[l],
            w_aup=w_iclr_up[l].astype(BF16), w_gup=w_gate_up[l].astype(BF16), k_k=key_k[l],
            k_a=key_a[l], r_k=bonus_rk[l], lnx_g=lnx_gain[l], lnx_b=lnx_bias[l],
            w_ba=w_branch_a[l].astype(BF16), ret_g=ret_norm_gain[l],
            w_bb=w_branch_b[l].astype(BF16), w_out=w_out[l].astype(BF16),
            ffn=_ffn_slices(w_ffn_in[l], w_ffn_out[l]),
        )
        xp, sp, wkp, rtp = _layer(xp, rope_p, zero_shift, zero_wkv, zero_ret, w)
        xs, ss, wks, rts = _layer(xs, rope_s, state_shift[l], state_wkv[l], state_ret[l], w)
        for lst, val in zip(outs, (wkp, rtp, sp, wks, rts, ss)):
            lst.append(val)
    return (xp, xs) + tuple(jnp.stack(o) for o in outs)
```

```python
import functools
import math

import jax
import jax.numpy as jnp
from jax import lax
from jax.experimental import pallas as pl
from jax.experimental.pallas import tpu as pltpu

F32 = jnp.float32
BF16 = jnp.bfloat16

PAST_LEN = 4096
RMS_EPS = 1e-6
LNX_EPS = 64e-5
GN_EPS = 1e-5
ROPE_BASE = 10000.0

V7X_LANES = 128
V7X_MXU_DIM = 256
WKV_GROUP_LANES = V7X_LANES
WKV_BLOCK_LANES = V7X_MXU_DIM
WKV_CHUNK = 64
WKV_PREP_BATCH = 2
V7X_VMEM_BYTES = 64 * 1024 * 1024
VMEM_LIMIT_BYTES = V7X_VMEM_BYTES * 7 // 8
VMEM_LIMIT_RESIDENT_BYTES = V7X_VMEM_BYTES * 15 // 16


def _dot(a, b):
    return jnp.dot(a, b, preferred_element_type=F32)


def _dot_nt(a, b):
    return lax.dot_general(a, b, (((1,), (1,)), ((), ())), preferred_element_type=F32)


def _dot_tn(a, b):
    return lax.dot_general(a, b, (((0,), (0,)), ((), ())), preferred_element_type=F32)


def _rms(x, g):
    return x * lax.rsqrt(jnp.mean(x * x, axis=-1, keepdims=True) + RMS_EPS) * g


def _tile(n, target, mult=V7X_LANES):
    if n <= target:
        return n
    best = None
    for t in range(mult, target + 1, mult):
        if n % t == 0:
            best = t
    assert best is not None, (n, target, mult)
    return best


def _params(*sem, vmem_limit=VMEM_LIMIT_BYTES):
    return pltpu.CompilerParams(dimension_semantics=sem, vmem_limit_bytes=vmem_limit)


def _split_bf16(x):
    hi = x.astype(BF16)
    lo = (x - hi.astype(F32)).astype(BF16)
    return hi, lo


def _wkv_kernel(x_ref, gn_ref, wa_ref, sh0_ref, s0_ref, mu_ref, w0_ref, wdup_ref, a0_ref, waup_ref,
                wgup_ref, kk_ref, ka_ref, rk_ref, lg_ref, lb_ref,
                ya_ref, sho_ref, so_ref,
                carry_ref, s_ref, ar_ref, bk_ref, bkh_ref, v_ref, y_ref, pc_ref, t_ref, av_ref, wcat_ref,
                v32_ref, rk2_ref, gate_ref,
                *, heads, head_dim, ranks):
    ns, tt = x_ref.shape[0], x_ref.shape[1]
    rows = ns * tt
    aw = heads * head_dim
    gw = WKV_GROUP_LANES
    hpg = gw // head_dim
    groups = aw // gw
    c_len = min(WKV_CHUNK, tt)
    n_chunks = tt // c_len
    dr, ir, gr = ranks
    t = pl.program_id(1)

    @pl.when(t == 0)
    def _init():
        s_ref[...] = jnp.zeros_like(s_ref)
        for s in range(ns):
            carry_ref[s] = sh0_ref[s]
            for h in range(heads):
                g, hh = divmod(h, hpg)
                lo = hh * head_dim
                s_ref[s * groups + g, lo:lo + head_dim, lo:lo + head_dim] = s0_ref[s, h]

    hn = _rms(x_ref[...].reshape(rows, x_ref.shape[2]), gn_ref[...]).astype(BF16)
    lw = dr + ir + gr

    def project_shifted(cols):
        z = _dot(hn, wa_ref[:, cols])
        rolled = pltpu.roll(z, 1, 0)
        row8 = lax.broadcasted_iota(jnp.int32, (8, z.shape[1]), 0)
        pieces = []
        for s in range(ns):
            r0 = s * tt
            pieces += [jnp.where(row8 == 0, carry_ref[s, :, cols], rolled[r0:r0 + 8]),
                       rolled[r0 + 8:r0 + tt]]
            last = z[r0 + tt - 1:r0 + tt, :]
            carry_ref[s, :, cols] = last
            sho_ref[s, :, cols] = last
        prev = jnp.concatenate(pieces, axis=0)
        return z + mu_ref[:, cols] * (prev - z)

    zl = project_shifted(slice(0, lw))
    wd = zl[:, 0:dr]
    ad = zl[:, dr:dr + ir]
    gd = zl[:, dr + ir:lw]
    xw = w0_ref[...] + _dot(jnp.tanh(wd).astype(BF16), wdup_ref[...])
    ld = (-math.exp(-0.5)) * jax.nn.sigmoid(xw)
    a_ic = jax.nn.sigmoid(a0_ref[...] + _dot(ad.astype(BF16), waup_ref[...]))
    gate_ref[...] = _dot(jax.nn.sigmoid(gd).astype(BF16), wgup_ref[...])

    gi = lax.broadcasted_iota(jnp.int32, (gw, gw), 0) // head_dim
    gj = lax.broadcasted_iota(jnp.int32, (gw, gw), 1) // head_dim
    same_head = gi == gj

    sw = min(aw, WKV_BLOCK_LANES)
    si = lax.broadcasted_iota(jnp.int32, (sw, sw), 0) // head_dim
    sj = lax.broadcasted_iota(jnp.int32, (sw, sw), 1) // head_dim
    ones_sw = (si == sj).astype(BF16)

    def head_sum(x):
        xb = x.astype(BF16)
        parts = [_dot(xb[:, q * sw:(q + 1) * sw], ones_sw) for q in range(aw // sw)]
        return jnp.concatenate(parts, axis=1) if len(parts) > 1 else parts[0]

    tb = min(tt, V7X_MXU_DIM)
    n_all = rows // c_len
    ti = lax.broadcasted_iota(jnp.int32, (tb, tb), 0)
    tj = lax.broadcasted_iota(jnp.int32, (tb, tb), 1)
    tri = (((ti // c_len) == (tj // c_len)) & (tj <= ti)).astype(BF16)
    ld_hi, ld_lo = _split_bf16(ld)
    cum_all = jnp.concatenate(
        [_dot(tri, ld_hi[i * tb:(i + 1) * tb]) + _dot(tri, ld_lo[i * tb:(i + 1) * tb])
         for i in range(rows // tb)], axis=0)

    for q in range(aw // sw):
        gl = slice(q * sw, (q + 1) * sw)
        z3 = project_shifted(slice(lw + 3 * q * sw, lw + 3 * (q + 1) * sw))
        r = z3[:, 0:sw]
        k = z3[:, sw:2 * sw]
        v = z3[:, 2 * sw:3 * sw]
        a_g = a_ic[:, gl]
        ld_g = ld[:, gl]
        kk = k * kk_ref[:, gl]
        k2 = k * (1.0 + (a_g - 1.0) * ka_ref[:, gl])
        kkn = kk * jnp.minimum(lax.rsqrt(_dot((kk * kk).astype(BF16), ones_sw)), 1e12)
        cum = cum_all[:, gl]
        p_inc = jnp.exp(cum)
        p_inv = jnp.exp(-cum)
        a_t = -(kkn * jnp.exp(cum - ld_g))
        r_t = r * p_inc
        b_t = (kkn * a_g) * p_inv
        k_t = k2 * p_inv
        for c in range(n_all):
            rs = slice(c * c_len, (c + 1) * c_len)
            p_end = p_inc[(c + 1) * c_len - 1:(c + 1) * c_len, :]
            ar_ref[c, 0:c_len, gl] = a_t[rs].astype(BF16)
            ar_ref[c, c_len:2 * c_len, gl] = r_t[rs].astype(BF16)
            bk_ref[c, 0:c_len, gl] = b_t[rs].astype(BF16)
            bk_ref[c, c_len:2 * c_len, gl] = k_t[rs].astype(BF16)
            bkh_ref[c, 0:c_len, gl] = (b_t[rs] * p_end).astype(BF16)
            bkh_ref[c, c_len:2 * c_len, gl] = (k_t[rs] * p_end).astype(BF16)
            pc_ref[c, :, gl] = p_end
        v_ref[:, gl] = v.astype(BF16)
        v32_ref[:, gl] = v
        rk2_ref[:, gl] = r * k2 * rk_ref[:, gl]

    lane_head = lax.broadcasted_iota(jnp.int32, (1, gw), 1) // head_dim
    ci = lax.broadcasted_iota(jnp.int32, (c_len, gw), 0)
    cj = lax.broadcasted_iota(jnp.int32, (c_len, gw), 1) % head_dim
    assert c_len == head_dim
    strict = cj < ci
    incl = cj <= ci
    eye = (cj == ci).astype(F32)
    n_doubling = int(math.log2(c_len))

    def bd(m):
        zero = jnp.zeros_like(m)
        return jnp.concatenate([jnp.where(lane_head == h, m, zero) for h in range(hpg)], axis=0)

    gs = range(groups)
    sls = [slice(g * gw, (g + 1) * gw) for g in gs]

    def prep(chunks):
        items = [(c, g) for c in chunks for g in gs]
        n = range(len(items))
        ar = [ar_ref[c, :, sls[g]] for c, g in items]
        bk = [bk_ref[c, :, sls[g]] for c, g in items]
        vv = [v_ref[c * c_len:(c + 1) * c_len, sls[g]] for c, g in items]
        s4 = [_dot_nt(ar[i], jnp.concatenate([bd(bk[i][0:c_len]), bd(bk[i][c_len:2 * c_len])],
                                              axis=0)) for i in n]
        a_ab = [jnp.where(strict, s4[i][0:c_len, 0:gw], 0.0) for i in n]
        a_ak = [jnp.where(strict, s4[i][0:c_len, gw:2 * gw], 0.0).astype(BF16) for i in n]
        for i, (c, g) in enumerate(items):
            wcat_ref[c, g, :, 0:gw] = jnp.where(incl, s4[i][c_len:2 * c_len, 0:gw], 0.0).astype(BF16)
            wcat_ref[c, g, :, gw:2 * gw] = jnp.where(incl, s4[i][c_len:2 * c_len, gw:2 * gw],
                                                     0.0).astype(BF16)
        av = [_dot(a_ak[i], bd(vv[i])) for i in n]
        for i, (c, g) in enumerate(items):
            av_ref[c, g] = av[i]
        tk = [eye + a_ab[i] for i in n]
        mb = [a_ab[i].astype(BF16) for i in n]
        mk = [_dot(mb[i], bd(mb[i])) for i in n]
        for step in range(1, n_doubling - 1):
            mb = [mk[i].astype(BF16) for i in n]
            rr = [_dot(mb[i], jnp.concatenate([bd(mb[i]), bd(tk[i].astype(BF16))], axis=1))
                  for i in n]
            mk = [rr[i][:, 0:gw] for i in n]
            tk = [tk[i] + rr[i][:, gw:2 * gw] for i in n]
        tk = [tk[i] + _dot(mk[i].astype(BF16), bd(tk[i].astype(BF16))) for i in n]
        for i, (c, g) in enumerate(items):
            t_ref[c, g] = tk[i].astype(BF16)

    def recur(lc):
        items = [(s * n_chunks + lc, g, s * groups + g) for s in range(ns) for g in gs]
        n = range(len(items))
        ar = [ar_ref[c, :, sls[g]] for c, g, _ in items]
        vv = [v_ref[c * c_len:(c + 1) * c_len, sls[g]] for c, g, _ in items]
        s0 = [s_ref[si] for _, _, si in items]
        m1 = [_dot_nt(ar[i], s0[i].astype(BF16)) for i in n]
        xb = [(m1[i][0:c_len] + av_ref[c, g]).astype(BF16) for i, (c, g, _) in enumerate(items)]
        ub = [_dot(t_ref[c, g], bd(xb[i])).astype(BF16) for i, (c, g, _) in enumerate(items)]
        ds = [_dot_tn(jnp.concatenate([ub[i], vv[i]], axis=0), bkh_ref[c, :, sls[g]])
              for i, (c, g, _) in enumerate(items)]
        for i, (c, g, si) in enumerate(items):
            s_ref[si] = s0[i] * pc_ref[c][:, sls[g]] + jnp.where(same_head, ds[i], 0.0)
        yy = [_dot(wcat_ref[c, g], jnp.concatenate([bd(ub[i]), bd(vv[i])], axis=0))
              for i, (c, g, _) in enumerate(items)]
        for i, (c, g, _) in enumerate(items):
            y_ref[c * c_len:(c + 1) * c_len, sls[g]] = m1[i][c_len:2 * c_len] + yy[i]

    for c0 in range(0, n_all, WKV_PREP_BATCH):
        prep(range(c0, min(c0 + WKV_PREP_BATCH, n_all)))
    for lc in range(n_chunks):
        recur(lc)

    y = y_ref[...]
    inv_n = 1.0 / head_dim
    mean = head_sum(y) * inv_n
    d = y - mean
    var = head_sum(d * d) * inv_n
    yn = d * lax.rsqrt(var + LNX_EPS) * lg_ref[...] + lb_ref[...]
    bonus = head_sum(rk2_ref[...]) * v32_ref[...]
    ya_ref[...] = ((yn + bonus) * gate_ref[...]).astype(ya_ref.dtype).reshape(ya_ref.shape)

    @pl.when(t == pl.num_programs(1) - 1)
    def _fin():
        for s in range(ns):
            for h in range(heads):
                g, hh = divmod(h, hpg)
                lo = hh * head_dim
                so_ref[s, h] = s_ref[s * groups + g, lo:lo + head_dim, lo:lo + head_dim]


def _group_major(a, aw):
    lead = a.shape[:-1]
    bw = min(aw, WKV_BLOCK_LANES)
    g = aw // bw
    rkv = a[..., :3 * aw].reshape(*lead, 3, g, bw)
    rkv = jnp.swapaxes(rkv, -3, -2).reshape(*lead, 3 * aw)
    return jnp.concatenate([a[..., 3 * aw:], rkv], axis=-1)


def _group_major_inverse(a, aw):
    lead = a.shape[:-1]
    bw = min(aw, WKV_BLOCK_LANES)
    g = aw // bw
    lw = a.shape[-1] - 3 * aw
    rkv = a[..., lw:].reshape(*lead, g, 3, bw)
    rkv = jnp.swapaxes(rkv, -3, -2).reshape(*lead, 3 * aw)
    return jnp.concatenate([rkv, a[..., :lw]], axis=-1)


def _wkv_branch(x, gn, wa, shift0, s0, mu, w0, wdup, a0, waup, wgup, k_k, k_a, r_k, lnx_g, lnx_b):
    b, t, d = x.shape
    asw = wa.shape[1]
    heads, head_dim = s0.shape[1], s0.shape[2]
    aw = heads * head_dim
    ranks = (wdup.shape[0], waup.shape[0], wgup.shape[0])
    ns = 2 if b % 2 == 0 else 1
    tt = _tile(t, 512 // ns, WKV_CHUNK)
    c_len = min(WKV_CHUNK, tt)
    n_all = ns * tt // c_len
    groups = aw // WKV_GROUP_LANES
    row = lambda a: a.reshape(1, -1)
    const = lambda shape: pl.BlockSpec(shape, lambda i, j: (0,) * len(shape))
    kern = functools.partial(_wkv_kernel, heads=heads, head_dim=head_dim, ranks=ranks)
    return pl.pallas_call(
        kern,
        grid=(b // ns, t // tt),
        in_specs=[
            pl.BlockSpec((ns, tt, d), lambda i, j: (i, j, 0)),
            const((1, d)),
            pl.BlockSpec(wa.shape, lambda i, j: (0, 0), pipeline_mode=pl.Buffered(1)),
            pl.BlockSpec((ns, 1, asw), lambda i, j: (i, 0, 0)),
            pl.BlockSpec((ns, heads, head_dim, head_dim), lambda i, j: (i, 0, 0, 0)),
            const((1, asw)), const((1, aw)), const(wdup.shape), const((1, aw)), const(waup.shape),
            const(wgup.shape), const((1, aw)), const((1, aw)), const((1, aw)), const((1, aw)),
            const((1, aw)),
        ],
        out_specs=[
            pl.BlockSpec((ns, tt, aw), lambda i, j: (i, j, 0)),
            pl.BlockSpec((ns, 1, asw), lambda i, j: (i, 0, 0)),
            pl.BlockSpec((ns, heads, head_dim, head_dim), lambda i, j: (i, 0, 0, 0)),
        ],
        out_shape=[
            jax.ShapeDtypeStruct((b, t, aw), BF16),
            jax.ShapeDtypeStruct((b, 1, asw), F32),
            jax.ShapeDtypeStruct((b, heads, head_dim, head_dim), F32),
        ],
        scratch_shapes=[
            pltpu.VMEM((ns, 1, asw), F32),
            pltpu.VMEM((ns * groups, WKV_GROUP_LANES, WKV_GROUP_LANES), F32),
            pltpu.VMEM((n_all, 2 * c_len, aw), BF16),
            pltpu.VMEM((n_all, 2 * c_len, aw), BF16),
            pltpu.VMEM((n_all, 2 * c_len, aw), BF16),
            pltpu.VMEM((ns * tt, aw), BF16),
            pltpu.VMEM((ns * tt, aw), F32),
            pltpu.VMEM((n_all, 1, aw), F32),
            pltpu.VMEM((n_all, groups, c_len, WKV_GROUP_LANES), BF16),
            pltpu.VMEM((n_all, groups, c_len, WKV_GROUP_LANES), F32),
            pltpu.VMEM((n_all, groups, c_len, 2 * WKV_GROUP_LANES), BF16),
            pltpu.VMEM((ns * tt, aw), F32),
            pltpu.VMEM((ns * tt, aw), F32),
            pltpu.VMEM((ns * tt, aw), F32),
        ],
        compiler_params=_params("arbitrary", "arbitrary"),
        name="wkv7_chunked",
    )(x, gn, wa, shift0.reshape(b, 1, asw), s0, row(mu), row(w0), wdup, row(a0), waup, wgup,
      row(k_k), row(k_a), row(r_k), row(lnx_g), row(lnx_b))


def _ret_proj_kernel(x_ref, g_ref, w_ref, cos_ref, sin_ref, r0_ref, gn_ref, yb_ref, ro_ref,
                     st_ref, dm_ref, qd_ref, kd_ref, *, heads, dk, dv, blk, steps_per_seq):
    tm = x_ref.shape[0]
    hw = 2 * dk + 2 * dv
    i = pl.program_id(0)
    log_gammas = [math.log1p(-(2.0 ** (-5.0 - h))) for h in range(heads)]

    @pl.when(i == 0)
    def _tables():
        diff = (lax.broadcasted_iota(jnp.int32, (blk, blk), 0)
                - lax.broadcasted_iota(jnp.int32, (blk, blk), 1)).astype(F32)
        pos = lax.broadcasted_iota(jnp.int32, (blk, dk), 0).astype(F32)
        for h in range(heads):
            lg = log_gammas[h]
            dm_ref[h] = jnp.where(diff >= 0, jnp.exp(jnp.maximum(diff, 0.0) * lg), 0.0)
            qd_ref[h] = jnp.exp((pos + 1.0) * lg) * (dk ** -0.5)
            kd_ref[h] = jnp.exp((blk - 1.0 - pos) * lg)

    @pl.when(i % steps_per_seq == 0)
    def _init():
        st_ref[...] = r0_ref[0]

    hn = _rms(x_ref[...], g_ref[...]).astype(BF16)
    for h in range(heads):
        zh = _dot(hn, w_ref[:, h * hw:(h + 1) * hw])
        for s in range(tm // blk):
            rows = slice(s * blk, (s + 1) * blk)
            cosv = cos_ref[rows, :]
            sinv = sin_ref[rows, :]
            qh = zh[rows, 0:dk]
            kh = zh[rows, dk:2 * dk]
            vh = zh[rows, 2 * dk:2 * dk + dv].astype(BF16)
            gate = zh[rows, 2 * dk + dv:hw]
            qrot = qh * cosv + pltpu.roll(qh, dk // 2, 1) * sinv
            kr = kh * cosv + pltpu.roll(kh, dk // 2, 1) * sinv
            scores = _dot_nt((qrot * (dk ** -0.5)).astype(BF16), kr.astype(BF16)) * dm_ref[h]
            inner = _dot(scores.astype(BF16), vh)
            st = st_ref[h]
            cross = _dot((qrot * qd_ref[h]).astype(BF16), st.astype(BF16))
            ke = kr * kd_ref[h]
            st_ref[h] = math.exp(blk * log_gammas[h]) * st + _dot_tn(ke.astype(BF16), vh)
            o = inner + cross
            mean = jnp.mean(o, axis=-1, keepdims=True)
            d = o - mean
            var = jnp.mean(d * d, axis=-1, keepdims=True)
            on = d * lax.rsqrt(var + GN_EPS) * gn_ref[:, h * dv:(h + 1) * dv]
            yb_ref[rows, h * dv:(h + 1) * dv] = (jax.nn.silu(gate) * on).astype(yb_ref.dtype)

    @pl.when(i % steps_per_seq == steps_per_seq - 1)
    def _fin():
        ro_ref[0] = st_ref[...]


def _ret_proj(x, g, w_heads, cos_t, sin_t, r0, gn_g, seq_len):
    m, d = x.shape
    heads, dk, dv = r0.shape[1], r0.shape[2], r0.shape[3]
    vw = heads * dv
    tm = _tile(seq_len, 512, 64)
    blk = _tile(tm, 256, 64)
    sps = seq_len // tm
    kern = functools.partial(_ret_proj_kernel, heads=heads, dk=dk, dv=dv, blk=blk, steps_per_seq=sps)
    return pl.pallas_call(
        kern,
        grid=(m // tm,),
        in_specs=[
            pl.BlockSpec((tm, d), lambda i: (i, 0)),
            pl.BlockSpec((1, d), lambda i: (0, 0)),
            pl.BlockSpec(w_heads.shape, lambda i: (0, 0), pipeline_mode=pl.Buffered(1)),
            pl.BlockSpec((tm, dk), lambda i: (i % sps, 0)),
            pl.BlockSpec((tm, dk), lambda i: (i % sps, 0)),
            pl.BlockSpec((1, heads, dk, dv), lambda i: (i // sps, 0, 0, 0)),
            pl.BlockSpec((1, vw), lambda i: (0, 0)),
        ],
        out_specs=[
            pl.BlockSpec((tm, vw), lambda i: (i, 0)),
            pl.BlockSpec((1, heads, dk, dv), lambda i: (i // sps, 0, 0, 0)),
        ],
        out_shape=[
            jax.ShapeDtypeStruct((m, vw), BF16),
            jax.ShapeDtypeStruct((m // seq_len, heads, dk, dv), F32),
        ],
        scratch_shapes=[pltpu.VMEM((heads, dk, dv), F32), pltpu.VMEM((heads, blk, blk), F32),
                        pltpu.VMEM((heads, blk, dk), F32), pltpu.VMEM((heads, blk, dk), F32)],
        compiler_params=_params("arbitrary", vmem_limit=VMEM_LIMIT_RESIDENT_BYTES),
        name="retention_proj",
    )(x, g, w_heads, cos_t, sin_t, r0, gn_g.reshape(1, -1))


def _merge_kernel(ya_ref, yb_ref, x_ref, gpre_ref, wg_ref, wba_ref, wbb_ref, wo_ref, g_ref, o_ref):
    d = x_ref.shape[1]
    x = x_ref[...]
    h = _rms(x, gpre_ref[...]).astype(BF16)
    ga = jax.nn.sigmoid(_dot(h, wg_ref[:, 0:d]))
    gb = jax.nn.sigmoid(_dot(h, wg_ref[:, d:2 * d]))
    merged = ga * _dot(ya_ref[...], wba_ref[...]) + gb * _dot(yb_ref[...], wbb_ref[...])
    mo = _dot(merged.astype(BF16), wo_ref[...])
    o_ref[...] = x + _rms(mo, g_ref[...])


def _merge(ya, yb, x, gpre, wg, wba, wbb, wo, g):
    m, d = x.shape
    tm = _tile(m, 256, 8)
    tok = lambda w: pl.BlockSpec((tm, w), lambda i: (i, 0))
    res = lambda a: pl.BlockSpec(a.shape, lambda i: (0, 0), pipeline_mode=pl.Buffered(1))
    vec = pl.BlockSpec((1, d), lambda i: (0, 0))
    return pl.pallas_call(
        _merge_kernel,
        grid=(m // tm,),
        in_specs=[tok(ya.shape[1]), tok(yb.shape[1]), tok(d), vec,
                  res(wg), res(wba), res(wbb), res(wo), vec],
        out_specs=tok(d),
        out_shape=jax.ShapeDtypeStruct((m, d), F32),
        compiler_params=_params("arbitrary", vmem_limit=VMEM_LIMIT_RESIDENT_BYTES),
        name="merge_out",
    )(ya, yb, x, gpre, wg, wba, wbb, wo, g)


FFN_CHUNK = V7X_MXU_DIM
FFN_PASS_CHUNKS = 6


def _ffn_pass_kernel(x_ref, gpre_ref, wg_ref, wu_ref, wo_ref, *rest, first, last):
    rest = list(rest)
    acc = None if first else rest.pop(0)[...]
    gpost_ref = rest.pop(0) if last else None
    o_ref = rest.pop(0)
    x = x_ref[...]
    h = _rms(x, gpre_ref[...]).astype(BF16)
    for c in range(wg_ref.shape[1] // FFN_CHUNK):
        cols = slice(c * FFN_CHUNK, (c + 1) * FFN_CHUNK)
        act = jax.nn.silu(_dot(h, wg_ref[:, cols])) * _dot(h, wu_ref[:, cols])
        part = _dot(act.astype(BF16), wo_ref[cols, :])
        acc = part if acc is None else acc + part
    o_ref[...] = x + _rms(acc, gpost_ref[...]) if last else acc


def _ffn(x, gpre, ffn_slices, gpost):
    m, d = x.shape
    tm = _tile(m, 512, 8)
    tok = pl.BlockSpec((tm, d), lambda i: (i, 0))
    vec = pl.BlockSpec((1, d), lambda i: (0, 0))
    res = lambda a: pl.BlockSpec(a.shape, lambda i: (0, 0), pipeline_mode=pl.Buffered(1))
    acc = None
    for p, (wg, wu, wo) in enumerate(ffn_slices):
        first, last = p == 0, p == len(ffn_slices) - 1
        args = [x, gpre, wg, wu, wo] + ([] if first else [acc]) + ([gpost] if last else [])
        specs = [tok, vec, res(wg), res(wu), res(wo)] + ([] if first else [tok]) + ([vec] if last else [])
        acc = pl.pallas_call(
            functools.partial(_ffn_pass_kernel, first=first, last=last),
            grid=(m // tm,),
            in_specs=specs,
            out_specs=tok,
            out_shape=jax.ShapeDtypeStruct((m, d), F32),
            compiler_params=_params("arbitrary"),
            name="swiglu_ffn_pass",
        )(*args)
    return acc


def _ffn_slices(w_fi, w_fo):
    f = w_fo.shape[0]
    n_chunks = f // FFN_CHUNK
    assert n_chunks * FFN_CHUNK == f
    n_pass = -(-n_chunks // FFN_PASS_CHUNKS)
    sizes = [n_chunks // n_pass + (1 if p < n_chunks % n_pass else 0) for p in range(n_pass)]
    out, off = [], 0
    for sz in sizes:
        w = sz * FFN_CHUNK
        out.append((w_fi[:, off:off + w].astype(BF16), w_fi[:, f + off:f + off + w].astype(BF16),
                    w_fo[off:off + w, :].astype(BF16)))
        off += w
    return out


def _rope_tables(pos, dk):
    half = dk // 2
    inv = ROPE_BASE ** (-jnp.arange(half, dtype=F32) / half)
    ang = pos[:, None] * inv[None, :]
    cos = jnp.cos(ang)
    sin = jnp.sin(ang)
    return jnp.concatenate([cos, cos], axis=-1), jnp.concatenate([-sin, sin], axis=-1)


def _head_major(w_b, heads, dk, dv):
    d = w_b.shape[0]
    qkw, vw = heads * dk, heads * dv
    parts = [w_b[:, 0:qkw].reshape(d, heads, dk), w_b[:, qkw:2 * qkw].reshape(d, heads, dk),
             w_b[:, 2 * qkw:2 * qkw + vw].reshape(d, heads, dv),
             w_b[:, 2 * qkw + vw:].reshape(d, heads, dv)]
    return jnp.concatenate(parts, axis=2).reshape(d, heads * (2 * dk + 2 * dv))


def _layer(x, rope, shift0, s0, r0, w):
    b, t, d = x.shape
    x2 = x.reshape(b * t, d)
    row = lambda a: a.reshape(1, -1)
    aw = w["w_dup"].shape[1]
    ya, shift, s_new = _wkv_branch(x, row(w["nmp"]), w["w_a"], _group_major(shift0, aw), s0, w["mu"],
                                   w["w0"], w["w_dup"], w["a0"], w["w_aup"], w["w_gup"], w["k_k"],
                                   w["k_a"], w["r_k"], w["lnx_g"], w["lnx_b"])
    shift = _group_major_inverse(shift.reshape(b, -1), aw)
    yb, r_new = _ret_proj(x2, row(w["nmp"]), w["w_b"], rope[0], rope[1], r0, w["ret_g"], t)
    x1 = _merge(ya.reshape(b * t, -1), yb, x2, row(w["nmp"]), w["w_g"], w["w_ba"], w["w_bb"],
                w["w_out"], row(w["nmq"]))
    xo = _ffn(x1, row(w["nfp"]), w["ffn"], row(w["nfq"]))
    return xo.reshape(b, t, d), shift, s_new, r_new


def kernel(x_prompt, x_sample, state_wkv, state_ret, state_shift, norm_mix_pre, norm_mix_post, norm_ffn_pre, norm_ffn_post, w_in, shift_mu, decay_base, w_decay_up, iclr_base, w_iclr_up, w_gate_up, key_k, key_a, bonus_rk, lnx_gain, lnx_bias, w_branch_a, ret_norm_gain, w_branch_b, w_out, w_ffn_in, w_ffn_out):
    depth = w_in.shape[0]
    bp, tp, _ = x_prompt.shape
    bs, ts, _ = x_sample.shape
    asw = shift_mu.shape[1]
    heads_a, hd_a = bonus_rk.shape[1], bonus_rk.shape[2]
    heads_b, dk, dv = state_ret.shape[2], state_ret.shape[3], state_ret.shape[4]
    b_in = 2 * heads_b * dk + 2 * heads_b * dv
    dt = x_prompt.dtype

    rope_p = _rope_tables(jnp.arange(tp, dtype=F32), dk)
    rope_s = _rope_tables(jnp.arange(ts, dtype=F32) + jnp.float32(PAST_LEN), dk)
    zero_shift = jnp.zeros((bp, asw), dt)
    zero_wkv = jnp.zeros((bp, heads_a, hd_a, hd_a), dt)
    zero_ret = jnp.zeros((bp, heads_b, dk, dv), dt)

    xp, xs = x_prompt, x_sample
    outs = [[] for _ in range(6)]
    for l in range(depth):
        w = dict(
            nmp=norm_mix_pre[l], nmq=norm_mix_post[l], nfp=norm_ffn_pre[l], nfq=norm_ffn_post[l],
            w_a=_group_major(w_in[l, :, :asw], heads_a * hd_a).astype(BF16),
            w_b=_head_major(w_in[l, :, asw:asw + b_in], heads_b, dk, dv).astype(BF16),
            w_g=w_in[l, :, asw + b_in:].astype(BF16),
            mu=_group_major(shift_mu[l], heads_a * hd_a), w0=decay_base[l], w_dup=w_decay_up[l].astype(BF16), a0=iclr_base[l],
            w_aup=w_iclr_up[l].astype(BF16), w_gup=w_gate_up[l].astype(BF16), k_k=key_k[l],
            k_a=key_a[l], r_k=bonus_rk[l], lnx_g=lnx_gain[l], lnx_b=lnx_bias[l],
            w_ba=w_branch_a[l].astype(BF16), ret_g=ret_norm_gain[l],
            w_bb=w_branch_b[l].astype(BF16), w_out=w_out[l].astype(BF16),
            ffn=_ffn_slices(w_ffn_in[l], w_ffn_out[l]),
        )
        xp, sp, wkp, rtp = _layer(xp, rope_p, zero_shift, zero_wkv, zero_ret, w)
        xs, ss, wks, rts = _layer(xs, rope_s, state_shift[l], state_wkv[l], state_ret[l], w)
        for lst, val in zip(outs, (wkp, rtp, sp, wks, rts, ss)):
            lst.append(val)
    return (xp, xs) + tuple(jnp.stack(o) for o in outs)
```
